```python
import numpy as np
import jax
import jax.numpy as jnp
from jax import lax

D_MODEL = 1024
BATCH = 4
SEQ = 4096
DEPTH = 1

NSA_HEADS = 8
NSA_KV_GROUPS = 2
NSA_HEAD_DIM = 64
CMP_STRIDE = 16
CMP_LEN = 2 * CMP_STRIDE
CMP_HIDDEN = 128
SEL_BLOCK = 64
SEL_TOPN = 16
WINDOW = 512
Q_BLOCK = 128
ML_HEADS = 4
ML_HEAD_DIM = 128
ML_CHUNK = 64
CONV_WIDTH = 4
N_EXPERTS = 32
TOP_K = 4
D_EXPERT = 1024
SWIGLU_LIMIT = 7.0
SWIGLU_ALPHA = 1.702
MOE_ROWS = 256
NORM_EPS = 1e-6
NEG_INF = -1e30
FORCE_BONUS = 1e4

NSA_QW = NSA_HEADS * NSA_HEAD_DIM
NSA_KVW = NSA_KV_GROUPS * NSA_HEAD_DIM
ML_W = ML_HEADS * ML_HEAD_DIM
IN_SPLITS = (NSA_QW, 6 * NSA_KVW, 3 * NSA_HEADS, 2 * ML_W, ML_W, 2 * ML_HEADS, ML_W, 2 * D_MODEL)
IN_W = sum(IN_SPLITS)

kernel_name = 'hybrid_nsa_mlstm_moe_block'


def rms_norm(x, g):
    xf = x.astype(jnp.float32)
    y = xf * lax.rsqrt(jnp.mean(xf * xf, axis=-1, keepdims=True) + NORM_EPS)
    return (y * g.astype(jnp.float32)).astype(x.dtype)


def split_cols(t, sizes):
    cuts = [int(v) for v in np.cumsum(sizes)[:-1]]
    return jnp.split(t, cuts, axis=-1)


def causal_conv(x, w, b):
    C = x.shape[-1]
    y = lax.conv_general_dilated(x, w[:, None, :].astype(x.dtype), window_strides=(1,),
                                 padding=((CONV_WIDTH - 1, 0),),
                                 dimension_numbers=('NWC', 'WIO', 'NWC'),
                                 feature_group_count=C)
    return y + b


def nsa_compress(t, pe, w1, w2):
    B, G, S, dh = t.shape
    nc = S // CMP_STRIDE - 1
    t16 = t.reshape(B, G, S // CMP_STRIDE, CMP_STRIDE, dh)
    blocks = jnp.concatenate([t16[:, :, :-1], t16[:, :, 1:]], axis=3) + pe
    return jax.nn.gelu(blocks.reshape(B, G, nc, CMP_LEN * dh) @ w1) @ w2


def nsa_mixer(q, kv, gate_pre, pe_k, pe_v, w1_k, w2_k, w1_v, w2_v):
    B, S, _ = q.shape
    G, R, dh = NSA_KV_GROUPS, NSA_HEADS // NSA_KV_GROUPS, NSA_HEAD_DIM
    scale = dh ** -0.5
    f32 = jnp.float32
    qh = q.reshape(B, S, G, R, dh).transpose(0, 2, 3, 1, 4)
    kvh = kv.reshape(B, S, 6, G, dh).transpose(2, 0, 3, 1, 4)
    k_cmp, v_cmp, k_sel, v_sel, k_win, v_win = kvh[0], kvh[1], kvh[2], kvh[3], kvh[4], kvh[5]
    tpos = jnp.arange(S)

    kc = nsa_compress(k_cmp, pe_k, w1_k, w2_k)
    vc = nsa_compress(v_cmp, pe_v, w1_v, w2_v)
    nc = kc.shape[2]
    cmp_mask = (jnp.arange(nc) * CMP_STRIDE + CMP_LEN - 1)[None, :] <= tpos[:, None]
    s_c = jnp.einsum('bgrsd,bgnd->bgrsn', qh, kc).astype(f32) * scale
    p_c = jax.nn.softmax(jnp.where(cmp_mask, s_c, NEG_INF), axis=-1) * cmp_mask
    o_cmp = jnp.einsum('bgrsn,bgnd->bgrsd', p_c.astype(vc.dtype), vc)

    ns = S // SEL_BLOCK
    n_sel = min(SEL_TOPN, ns)
    c0 = np.arange(nc) * CMP_STRIDE
    s0 = np.arange(ns) * SEL_BLOCK
    overlap = np.clip(np.minimum(c0[:, None] + CMP_LEN, s0[None, :] + SEL_BLOCK)
                      - np.maximum(c0[:, None], s0[None, :]), 0, None) / CMP_LEN
    imp = jnp.einsum('bgsn,nj->bgsj', p_c.sum(axis=2), jnp.asarray(overlap, f32))
    cur = (tpos // SEL_BLOCK)[:, None]
    blk = jnp.arange(ns)[None, :]
    forced = (blk == 0) | (blk == cur) | (blk == cur - 1)
    imp = jnp.where(blk > cur, NEG_INF, imp + FORCE_BONUS * forced)
    _, sel_idx = lax.top_k(imp, n_sel)

    nqb = S // Q_BLOCK
    kb_sel = k_sel.reshape(B, G, ns, SEL_BLOCK, dh)
    vb_sel = v_sel.reshape(B, G, ns, SEL_BLOCK, dh)
    pad = jnp.zeros((B, G, WINDOW, dh), k_win.dtype)
    k_win_p = jnp.concatenate([pad, k_win], axis=2)
    v_win_p = jnp.concatenate([pad, v_win], axis=2)
    q_blocks = qh.reshape(B, G, R, nqb, Q_BLOCK, dh).transpose(3, 0, 1, 2, 4, 5)
    idx_blocks = sel_idx.reshape(B, G, nqb, Q_BLOCK, n_sel).transpose(2, 0, 1, 3, 4)
    gather_blocks = jax.vmap(jax.vmap(lambda blocks, idx: blocks[idx]))
    n_keys = n_sel * SEL_BLOCK

    def block_attn(args):
        qb, ib, bi = args
        tq = bi * Q_BLOCK + jnp.arange(Q_BLOCK)
        ks = gather_blocks(kb_sel, ib).reshape(B, G, Q_BLOCK, n_keys, dh)
        vs = gather_blocks(vb_sel, ib).reshape(B, G, Q_BLOCK, n_keys, dh)
        pos = (ib[..., None] * SEL_BLOCK + jnp.arange(SEL_BLOCK)).reshape(B, G, Q_BLOCK, n_keys)
        smask = (pos <= tq[:, None])[:, :, None]
        s = jnp.einsum('bgrqd,bgqkd->bgrqk', qb, ks).astype(f32) * scale
        p = jax.nn.softmax(jnp.where(smask, s, NEG_INF), axis=-1)
        o_s = jnp.einsum('bgrqk,bgqkd->bgrqd', p.astype(vs.dtype), vs)
        kw = lax.dynamic_slice_in_dim(k_win_p, bi * Q_BLOCK, Q_BLOCK + WINDOW, axis=2)
        vw = lax.dynamic_slice_in_dim(v_win_p, bi * Q_BLOCK, Q_BLOCK + WINDOW, axis=2)
        kpos = bi * Q_BLOCK - WINDOW + jnp.arange(Q_BLOCK + WINDOW)
        rel = tq[:, None] - kpos[None, :]
        wmask = (rel >= 0) & (rel < WINDOW) & (kpos[None, :] >= 0)
        s = jnp.einsum('bgrqd,bgkd->bgrqk', qb, kw).astype(f32) * scale
        p = jax.nn.softmax(jnp.where(wmask, s, NEG_INF), axis=-1)
        o_w = jnp.einsum('bgrqk,bgkd->bgrqd', p.astype(vw.dtype), vw)
        return o_s, o_w

    o_sel, o_win = lax.map(block_attn, (q_blocks, idx_blocks, jnp.arange(nqb)))
    o_sel = o_sel.transpose(1, 2, 3, 0, 4, 5).reshape(B, G, R, S, dh)
    o_win = o_win.transpose(1, 2, 3, 0, 4, 5).reshape(B, G, R, S, dh)

    g = jax.nn.sigmoid(gate_pre.astype(f32)).astype(q.dtype)
    g = g.reshape(B, S, 3, G, R).transpose(2, 0, 3, 4, 1)[..., None]
    o = g[0] * o_cmp + g[1] * o_sel + g[2] * o_win
    return o.transpose(0, 3, 1, 2, 4).reshape(B, S, NSA_QW)


def mlstm_mixer(q, k, v, i_pre, f_pre, o_pre, norm_g):
    B, S, _ = q.shape
    H, d, L = ML_HEADS, ML_HEAD_DIM, ML_CHUNK
    nch = S // L
    f32 = jnp.float32

    def heads(t):
        return t.reshape(B, nch, L, H, d).transpose(1, 0, 3, 2, 4).astype(f32)

    def gates(t):
        return t.reshape(B, nch, L, H).transpose(1, 0, 3, 2).astype(f32)

    qc, kc, vc = heads(q), heads(k) * (d ** -0.5), heads(v)
    li = gates(i_pre)
    bcum = jnp.cumsum(jax.nn.log_sigmoid(gates(f_pre)), axis=-1)
    causal = jnp.tril(jnp.ones((L, L), dtype=bool))

    def step(carry, xs):
        C, n, m = carry
        q_, k_, v_, li_, b_ = xs
        D = b_[..., :, None] - b_[..., None, :] + li_[..., None, :]
        D = jnp.where(causal, D, NEG_INF)
        inter = b_ + m[..., None]
        m_t = jnp.maximum(inter, D.max(axis=-1))
        Dw = jnp.exp(D - m_t[..., None])
        inter_w = jnp.exp(inter - m_t)
        qk = jnp.einsum('bhtd,bhsd->bhts', q_, k_) * Dw
        num = jnp.einsum('bhts,bhsd->bhtd', qk, v_) + inter_w[..., None] * jnp.einsum('bhtd,bhde->bhte', q_, C)
        den = qk.sum(axis=-1) + inter_w * jnp.einsum('bhtd,bhd->bht', q_, n)
        h = num / jnp.maximum(jnp.abs(den), jnp.exp(-m_t))[..., None]
        bL = b_[..., -1]
        a = bL[..., None] - b_ + li_
        m_new = jnp.maximum(bL + m, a.max(axis=-1))
        aw = jnp.exp(a - m_new[..., None])
        decay = jnp.exp(bL + m - m_new)
        C_new = decay[..., None, None] * C + jnp.einsum('bhs,bhsd,bhse->bhde', aw, k_, v_)
        n_new = decay[..., None] * n + jnp.einsum('bhs,bhsd->bhd', aw, k_)
        return (C_new, n_new, m_new), h

    init = (jnp.zeros((B, H, d, d), f32), jnp.zeros((B, H, d), f32), jnp.zeros((B, H), f32))
    _, h = lax.scan(step, init, (qc, kc, vc, li, bcum))
    h = h.transpose(1, 0, 3, 2, 4).reshape(B, S, H, d)
    h = h * lax.rsqrt(jnp.mean(h * h, axis=-1, keepdims=True) + NORM_EPS) * norm_g.reshape(H, d).astype(f32)
    h = h * jax.nn.sigmoid(o_pre.astype(f32)).reshape(B, S, H, d)
    return h.reshape(B, S, ML_W).astype(q.dtype)


def token_mixer(n, w_in, pe_k, pe_v, cw1_k, cw2_k, cw1_v, cw2_v, conv_w, conv_b, gate_b, norm_g,
                w_up_nsa, w_up_ml, w_out):
    proj = n @ w_in
    nsa_q, nsa_kv, nsa_g, ml_qk, ml_v, ml_if, ml_o, merge = split_cols(proj, IN_SPLITS)
    o_nsa = nsa_mixer(nsa_q, nsa_kv, nsa_g, pe_k, pe_v, cw1_k, cw2_k, cw1_v, cw2_v)
    ml_qk = jax.nn.silu(causal_conv(ml_qk, conv_w, conv_b))
    ml_q, ml_k = jnp.split(ml_qk, 2, axis=-1)
    ml_if = ml_if + gate_b
    o_ml = mlstm_mixer(ml_q, ml_k, ml_v, ml_if[..., :ML_HEADS], ml_if[..., ML_HEADS:], ml_o, norm_g)
    g_nsa, g_ml = jnp.split(jax.nn.sigmoid(merge.astype(jnp.float32)).astype(n.dtype), 2, axis=-1)
    return (g_nsa * (o_nsa @ w_up_nsa) + g_ml * (o_ml @ w_up_ml)) @ w_out


def moe_ffn(x, w_router, b_router, w1, b1, w2, b2):
    B, S, D = x.shape
    T = B * S
    A = T * TOP_K
    xt = x.reshape(T, D)
    logits = (xt @ w_router + b_router).astype(jnp.float32)
    top_v, top_i = lax.top_k(logits, TOP_K)
    gate = jax.nn.softmax(top_v, axis=-1)
    e_flat = top_i.reshape(A)
    tok_flat = jnp.arange(A) // TOP_K
    order = jnp.argsort(e_flat)
    e_s, tok_s, g_s = e_flat[order], tok_flat[order], gate.reshape(A)[order]
    counts = jnp.bincount(e_flat, length=N_EXPERTS)
    padded = (counts + MOE_ROWS - 1) // MOE_ROWS * MOE_ROWS
    grp_start = jnp.cumsum(counts) - counts
    pad_end = jnp.cumsum(padded)
    pad_start = pad_end - padded
    dest = pad_start[e_s] + jnp.arange(A) - grp_start[e_s]
    n_blk = -(-(A + N_EXPERTS * (MOE_ROWS - 1)) // MOE_ROWS)
    P = n_blk * MOE_ROWS
    row_tok = jnp.full((P,), T, dtype=jnp.int32).at[dest].set(tok_s.astype(jnp.int32))
    row_gate = jnp.zeros((P,), jnp.float32).at[dest].set(g_s)
    blk_expert = jnp.minimum(jnp.searchsorted(pad_end, jnp.arange(n_blk) * MOE_ROWS, side='right'),
                             N_EXPERTS - 1)
    x_pad = jnp.concatenate([xt, jnp.zeros((1, D), xt.dtype)], axis=0)
    x_rows = x_pad[row_tok].reshape(n_blk, MOE_ROWS, D)

    def expert_block(args):
        xb, e = args
        hcat = xb @ w1[e] + b1[e]
        glu = jnp.minimum(hcat[:, :D_EXPERT], SWIGLU_LIMIT)
        lin = jnp.clip(hcat[:, D_EXPERT:], -SWIGLU_LIMIT, SWIGLU_LIMIT)
        act = glu * jax.nn.sigmoid(SWIGLU_ALPHA * glu) * (lin + 1.0)
        return act @ w2[e] + b2[e]

    y_rows = lax.map(expert_block, (x_rows, blk_expert)).reshape(P, D)
    y_rows = y_rows * row_gate[:, None].astype(y_rows.dtype)
    y = jax.ops.segment_sum(y_rows, row_tok, num_segments=T + 1)[:T]
    return y.reshape(B, S, D)


def setup_inputs(seed: int = 0) -> dict:
    key = jax.random.key(seed)
    ks = jax.random.split(key, 32)
    f32 = jnp.float32

    def nrm(k, shape, scale):
        return jax.random.normal(k, shape, f32) * scale

    dh = NSA_HEAD_DIM
    return {
        'x': nrm(ks[0], (BATCH, SEQ, D_MODEL), 1.0),
        'c': nrm(ks[1], (BATCH, D_MODEL), 1.0),
        'ada_w': nrm(ks[2], (DEPTH, D_MODEL, 6 * D_MODEL), 0.5 * D_MODEL ** -0.5),
        'ada_b': nrm(ks[3], (DEPTH, 6 * D_MODEL), 0.02),
        'norm1_g': 1.0 + nrm(ks[4], (DEPTH, D_MODEL), 0.02),
        'w_in': nrm(ks[5], (DEPTH, D_MODEL, IN_W), D_MODEL ** -0.5),
        'nsa_pe_k': nrm(ks[6], (DEPTH, CMP_LEN, dh), 0.02),
        'nsa_pe_v': nrm(ks[7], (DEPTH, CMP_LEN, dh), 0.02),
        'nsa_cmp_w1_k': nrm(ks[8], (DEPTH, CMP_LEN * dh, CMP_HIDDEN), (CMP_LEN * dh) ** -0.5),
        'nsa_cmp_w2_k': nrm(ks[9], (DEPTH, CMP_HIDDEN, dh), CMP_HIDDEN ** -0.5),
        'nsa_cmp_w1_v': nrm(ks[10], (DEPTH, CMP_LEN * dh, CMP_HIDDEN), (CMP_LEN * dh) ** -0.5),
        'nsa_cmp_w2_v': nrm(ks[11], (DEPTH, CMP_HIDDEN, dh), CMP_HIDDEN ** -0.5),
        'ml_conv_w': nrm(ks[12], (DEPTH, CONV_WIDTH, 2 * ML_W), CONV_WIDTH ** -0.5),
        'ml_conv_b': nrm(ks[13], (DEPTH, 2 * ML_W), 0.02),
        'ml_gate_b': jnp.concatenate([nrm(ks[14], (DEPTH, ML_HEADS), 0.1),
                                      3.0 + nrm(ks[15], (DEPTH, ML_HEADS), 0.5)], axis=-1),
        'ml_norm_g': 1.0 + nrm(ks[16], (DEPTH, ML_W), 0.02),
        'w_up_nsa': nrm(ks[17], (DEPTH, NSA_QW, D_MODEL), NSA_QW ** -0.5),
        'w_up_ml': nrm(ks[18], (DEPTH, ML_W, D_MODEL), ML_W ** -0.5),
        'w_out': nrm(ks[19], (DEPTH, D_MODEL, D_MODEL), D_MODEL ** -0.5),
        'norm2_g': 1.0 + nrm(ks[20], (DEPTH, D_MODEL), 0.02),
        'router_w': nrm(ks[21], (DEPTH, D_MODEL, N_EXPERTS), D_MODEL ** -0.5),
        'router_b': nrm(ks[22], (DEPTH, N_EXPERTS), 0.01),
        'exp_w1': nrm(ks[23], (DEPTH, N_EXPERTS, D_MODEL, 2 * D_EXPERT), D_MODEL ** -0.5),
        'exp_b1': nrm(ks[24], (DEPTH, N_EXPERTS, 2 * D_EXPERT), 0.02),
        'exp_w2': nrm(ks[25], (DEPTH, N_EXPERTS, D_EXPERT, D_MODEL), D_EXPERT ** -0.5),
        'exp_b2': nrm(ks[26], (DEPTH, N_EXPERTS, D_MODEL), 0.02),
        'final_g': 1.0 + nrm(ks[27], (D_MODEL,), 0.02),
    }


def reference(x, c, ada_w, ada_b, norm1_g, w_in, nsa_pe_k, nsa_pe_v, nsa_cmp_w1_k, nsa_cmp_w2_k,
              nsa_cmp_w1_v, nsa_cmp_w2_v, ml_conv_w, ml_conv_b, ml_gate_b, ml_norm_g, w_up_nsa, w_up_ml,
              w_out, norm2_g, router_w, router_b, exp_w1, exp_b1, exp_w2, exp_b2, final_g):
    h = x
    c_act = jax.nn.silu(c)
    for layer in range(DEPTH):
        mod = (c_act @ ada_w[layer] + ada_b[layer])[:, None, :]
        sh1, sc1, gt1, sh2, sc2, gt2 = jnp.split(mod, 6, axis=-1)
        n = rms_norm(h, norm1_g[layer]) * (1.0 + sc1) + sh1
        mix = token_mixer(n, w_in[layer], nsa_pe_k[layer], nsa_pe_v[layer], nsa_cmp_w1_k[layer],
                          nsa_cmp_w2_k[layer], nsa_cmp_w1_v[layer], nsa_cmp_w2_v[layer], ml_conv_w[layer],
                          ml_conv_b[layer], ml_gate_b[layer], ml_norm_g[layer], w_up_nsa[layer],
                          w_up_ml[layer], w_out[layer])
        h = h + (1.0 + gt1) * mix
        n = rms_norm(h, norm2_g[layer]) * (1.0 + sc2) + sh2
        h = h + (1.0 + gt2) * moe_ffn(n, router_w[layer], router_b[layer], exp_w1[layer], exp_b1[layer],
                                      exp_w2[layer], exp_b2[layer])
    return rms_norm(h, final_g)
```

```python
import functools

import numpy as np
import jax
import jax.numpy as jnp
from jax import lax
from jax.experimental import pallas as pl
from jax.experimental.pallas import tpu as pltpu

F32 = jnp.float32
BF16 = jnp.bfloat16
I32 = jnp.int32
HIGHEST = lax.Precision.HIGHEST

D_MODEL = 1024
NSA_HEADS = 8
NSA_GROUPS = 2
NSA_REP = NSA_HEADS // NSA_GROUPS
NSA_DH = 64
NSA_SCALE = NSA_DH ** -0.5
CMP_STRIDE = 16
CMP_LEN = 32
CMP_HIDDEN = 128
SEL_BLOCK = 64
SEL_TOPN = 16
WINDOW = 512
Q_BLOCK = 128
ML_HEADS = 4
ML_DH = 128
ML_CHUNK = 64
CONV_WIDTH = 4
N_EXPERTS = 32
TOP_K = 4
D_EXPERT = 1024
SWIGLU_LIMIT = 7.0
SWIGLU_ALPHA = 1.702
MOE_ROWS = 256
NORM_EPS = 1e-6
NEG_INF = -1e30
FORCE_BONUS = 1e4

NSA_QW = NSA_HEADS * NSA_DH
NSA_KVW = NSA_GROUPS * NSA_DH
ML_W = ML_HEADS * ML_DH
LANES = 128
SUBLANES = 8
VMEM_LIMIT = 52 * 1024 * 1024

_C_Q = (0, 512)
_C_KVC = (512, 768)
_C_KVS = (768, 1280)
_C_MQK = (1280, 2304)
_C_MV = (2304, 2816)
_C_MO = (2816, 3328)
_C_MG = (3328, 5376)
_C_SM = (5376, 5504)
IN_WR = 5504


def _sigmoid(x):
    return 1.0 / (1.0 + jnp.exp(-x))


def _iota(shape, dim):
    return lax.broadcasted_iota(I32, shape, dim)


def _cparams(sem, vmem=None):
    return pltpu.CompilerParams(dimension_semantics=sem, vmem_limit_bytes=vmem)


def _mod_kernel(c_ref, w_ref, b_ref, o_ref):
    c = c_ref[...]
    ca = c * _sigmoid(c)
    o_ref[...] = jnp.dot(ca, w_ref[...], precision=HIGHEST, preferred_element_type=F32) + b_ref[...]


def _modulation(c, ada_w, ada_b):
    B, D = c.shape
    n = ada_w.shape[1] // D
    return pl.pallas_call(
        _mod_kernel,
        out_shape=jax.ShapeDtypeStruct((B, n * D), F32),
        grid=(n,),
        in_specs=[pl.BlockSpec((B, D), lambda j: (0, 0)),
                  pl.BlockSpec((D, D), lambda j: (0, j)),
                  pl.BlockSpec((1, D), lambda j: (0, j))],
        out_specs=pl.BlockSpec((B, D), lambda j: (0, j)),
        compiler_params=_cparams(("parallel",)),
        name="modulation",
    )(c, ada_w, ada_b.reshape(1, n * D))


def _inproj_kernel(x_ref, g_ref, sc_ref, sh_ref, w_ref,
                   q_ref, kvc_ref, kvs_ref, mqk_ref, mv_ref, mo_ref, mg_ref, sm_ref):
    x = x_ref[0]
    y = x * lax.rsqrt(jnp.mean(x * x, axis=-1, keepdims=True) + NORM_EPS) * g_ref[...]
    n = (y * (1.0 + sc_ref[0]) + sh_ref[0]).astype(BF16)

    def proj(c):
        return jnp.dot(n, w_ref[:, c[0]:c[1]], preferred_element_type=F32)

    q_ref[0] = (proj(_C_Q) * NSA_SCALE).astype(BF16)
    kvc_ref[0] = proj(_C_KVC)
    kvs_ref[0] = proj(_C_KVS).astype(BF16)
    mqk_ref[0] = proj(_C_MQK)
    mv_ref[0] = proj(_C_MV)
    mo_ref[0] = proj(_C_MO)
    mg_ref[0] = proj(_C_MG)
    sm_ref[0] = proj(_C_SM)


def _in_projection(x, g, sc, sh, w_r, tm=256):
    B, S, D = x.shape
    widths = [c[1] - c[0] for c in (_C_Q, _C_KVC, _C_KVS, _C_MQK, _C_MV, _C_MO, _C_MG, _C_SM)]
    dtypes = [BF16, F32, BF16, F32, F32, F32, F32, F32]
    vec = pl.BlockSpec((1, 1, D), lambda b, i: (b, 0, 0))
    return pl.pallas_call(
        _inproj_kernel,
        out_shape=[jax.ShapeDtypeStruct((B, S, w), dt) for w, dt in zip(widths, dtypes)],
        grid=(B, S // tm),
        in_specs=[pl.BlockSpec((1, tm, D), lambda b, i: (b, i, 0)),
                  pl.BlockSpec((1, D), lambda b, i: (0, 0)),
                  vec, vec,
                  pl.BlockSpec((D, IN_WR), lambda b, i: (0, 0))],
        out_specs=[pl.BlockSpec((1, tm, w), lambda b, i: (b, i, 0)) for w in widths],
        compiler_params=_cparams(("parallel", "parallel"), VMEM_LIMIT),
        name="in_projection",
    )(x, g, sc, sh, w_r)


def _compress_kernel(t_ref, pe_ref, w1_ref, w2_ref, o_ref):
    half = CMP_STRIDE * NSA_DH
    t = t_ref[0, 0]
    pe = pe_ref[0]
    nb = t.shape[0]
    a = jnp.dot((t + pe[0:1]).astype(BF16), w1_ref[0, :half, :].astype(BF16), preferred_element_type=F32)
    b = jnp.dot((t + pe[1:2]).astype(BF16), w1_ref[0, half:, :].astype(BF16), preferred_element_type=F32)
    h = a + pltpu.roll(b, nb - 1, 0)
    gl = 0.5 * h * (1.0 + jnp.tanh(np.sqrt(2.0 / np.pi).astype(np.float32) * (h + 0.044715 * (h * h * h))))
    o_ref[0, 0] = jnp.dot(gl.astype(BF16), w2_ref[0].astype(BF16), preferred_element_type=F32)


def _compress(t16, pe, w1, w2):
    B, _, nb, wdt = t16.shape
    return pl.pallas_call(
        _compress_kernel,
        out_shape=jax.ShapeDtypeStruct((B, 4, nb, NSA_DH), F32),
        grid=(B, 4),
        in_specs=[pl.BlockSpec((1, 1, nb, wdt), lambda b, j: (b, j, 0, 0)),
                  pl.BlockSpec((1, 2, wdt), lambda b, j: (j // 2, 0, 0)),
                  pl.BlockSpec((1, CMP_LEN * NSA_DH, CMP_HIDDEN), lambda b, j: (j // 2, 0, 0)),
                  pl.BlockSpec((1, CMP_HIDDEN, NSA_DH), lambda b, j: (j // 2, 0, 0))],
        out_specs=pl.BlockSpec((1, 1, nb, NSA_DH), lambda b, j: (b, j, 0, 0)),
        compiler_params=_cparams(("parallel", "parallel")),
        name="nsa_compress",
    )(t16, pe, w1, w2)


def _cmpattn_kernel(q_ref, kc_ref, ov_ref, oc_ref, sel_ref, *, n_sel):
    tq = q_ref.shape[1]
    ncp = kc_ref.shape[2]
    ns = ov_ref.shape[1]
    q0 = pl.program_id(1) * tq
    q = q_ref[0]
    t = q0 + _iota((tq, 1), 0)
    cmask = (_iota((tq, ncp), 1) * CMP_STRIDE + (CMP_LEN - 1)) <= t
    blk = _iota((tq, ns), 1)
    cur = t >> 6
    forced = (blk == 0) | (blk == cur) | (blk == cur - 1)
    for g in range(NSA_GROUPS):
        kc = kc_ref[0, g].astype(BF16)
        vc = kc_ref[0, NSA_GROUPS + g].astype(BF16)
        psum = jnp.zeros((tq, ncp), F32)
        for r in range(NSA_REP):
            h = g * NSA_REP + r
            qh = q[:, h * NSA_DH:(h + 1) * NSA_DH]
            s = lax.dot_general(qh, kc, (((1,), (1,)), ((), ())), preferred_element_type=F32)
            s = jnp.where(cmask, s, NEG_INF)
            e = jnp.exp(s - jnp.max(s, axis=-1, keepdims=True))
            p = e / jnp.sum(e, axis=-1, keepdims=True)
            p = jnp.where(cmask, p, 0.0)
            oc_ref[0, :, h * NSA_DH:(h + 1) * NSA_DH] = jnp.dot(p.astype(BF16), vc, preferred_element_type=F32)
            psum = psum + p
        imp = jnp.dot(psum, ov_ref[...], precision=HIGHEST, preferred_element_type=F32)
        imp = jnp.where(blk > cur, NEG_INF, imp + jnp.where(forced, FORCE_BONUS, 0.0))
        rank = jnp.zeros((tq, ns), F32)
        for i in range(ns):
            col = imp[:, i:i + 1]
            beats = (col > imp) | ((col == imp) & (blk > i))
            rank = rank + jnp.where(beats, 1.0, 0.0)
        sel_ref[0, g] = jnp.where(rank < n_sel, 1.0, 0.0).astype(BF16)


def _cmp_attention(q, cmp, ov, n_sel, tq=128):
    B, S, _ = q.shape
    ncp, ns = ov.shape
    return pl.pallas_call(
        functools.partial(_cmpattn_kernel, n_sel=n_sel),
        out_shape=[jax.ShapeDtypeStruct((B, S, NSA_QW), F32),
                   jax.ShapeDtypeStruct((B, NSA_GROUPS, S, ns), BF16)],
        grid=(B, S // tq),
        in_specs=[pl.BlockSpec((1, tq, NSA_QW), lambda b, i: (b, i, 0)),
                  pl.BlockSpec((1, 4, ncp, NSA_DH), lambda b, i: (b, 0, 0, 0)),
                  pl.BlockSpec((ncp, ns), lambda b, i: (0, 0))],
        out_specs=[pl.BlockSpec((1, tq, NSA_QW), lambda b, i: (b, i, 0)),
                   pl.BlockSpec((1, NSA_GROUPS, tq, ns), lambda b, i: (b, 0, i, 0))],
        compiler_params=_cparams(("parallel", "parallel")),
        name="nsa_cmp_attention",
    )(q, cmp, ov)


def _selwin_kernel(q_ref, ks_ref, vs_ref, kw_ref, vw_ref, sel_ref, oc_ref, sm_ref, o_ref, mask_ref, *, tk):
    qb = Q_BLOCK
    ns = sel_ref.shape[3]
    q0 = pl.program_id(1) * qb
    nt = (q0 + qb + tk - 1) // tk
    tq = q0 + _iota((qb, 1), 0)
    gates = _sigmoid(sm_ref[0])
    wlen = WINDOW + qb
    wstart = pl.multiple_of(jnp.maximum(q0 - WINDOW, 0), LANES)
    kw = kw_ref[0, pl.ds(wstart, wlen), :]
    vw = vw_ref[0, pl.ds(wstart, wlen), :]
    rel = tq - (wstart + _iota((1, wlen), 1))
    wmask = (rel >= 0) & (rel < WINDOW)
    lane = _iota((qb, LANES), 1)
    nt_dims = (((1,), (1,)), ((), ()))
    for g in range(NSA_GROUPS):
        selg = sel_ref[0, g]

        def make_mask(j, carry):
            kpos = j * tk + _iota((1, tk), 1)
            expand = jnp.where(_iota((ns, tk), 0) == ((j * tk + _iota((ns, tk), 1)) >> 6), 1.0, 0.0).astype(BF16)
            hit = jnp.dot(selg, expand, preferred_element_type=F32)
            mask_ref[j] = jnp.where((hit > 0.5) & (kpos <= tq), 1.0, 0.0)
            return carry

        lax.fori_loop(0, nt, make_mask, 0)
        for r in range(NSA_REP):
            h = g * NSA_REP + r
            qf = q_ref[0, :, (h // 2) * LANES:(h // 2 + 1) * LANES].astype(F32)
            if (h % 2) != g:
                qf = pltpu.roll(qf, NSA_DH, 1)
            qp = jnp.where((lane >> 6) == g, qf, 0.0).astype(BF16)

            def sel_step(j, carry):
                m, l, acc = carry
                r0 = pl.multiple_of(j * tk, tk)
                s = lax.dot_general(qp, ks_ref[0, pl.ds(r0, tk), :], nt_dims, preferred_element_type=F32)
                s = jnp.where(mask_ref[j] > 0.5, s, NEG_INF)
                m_new = jnp.maximum(m, jnp.max(s, axis=-1, keepdims=True))
                alpha = jnp.exp(m - m_new)
                p = jnp.exp(s - m_new)
                l = alpha * l + jnp.sum(p, axis=-1, keepdims=True)
                acc = alpha * acc + jnp.dot(p.astype(BF16), vs_ref[0, pl.ds(r0, tk), :], preferred_element_type=F32)
                return m_new, l, acc

            init = (jnp.full((qb, 1), NEG_INF, F32), jnp.zeros((qb, 1), F32), jnp.zeros((qb, LANES), F32))
            _, l, acc = lax.fori_loop(0, nt, sel_step, init)
            o_sel = acc / l

            sw = lax.dot_general(qp, kw, nt_dims, preferred_element_type=F32)
            sw = jnp.where(wmask, sw, NEG_INF)
            pw = jnp.exp(sw - jnp.max(sw, axis=-1, keepdims=True))
            o_win = jnp.dot(pw.astype(BF16), vw, preferred_element_type=F32) / jnp.sum(pw, axis=-1, keepdims=True)

            lo = g * NSA_DH
            cs = slice(h * NSA_DH, (h + 1) * NSA_DH)
            o_ref[0, :, cs] = (gates[:, h:h + 1] * oc_ref[0, :, cs]
                               + gates[:, NSA_HEADS + h:NSA_HEADS + h + 1] * o_sel[:, lo:lo + NSA_DH]
                               + gates[:, 2 * NSA_HEADS + h:2 * NSA_HEADS + h + 1] * o_win[:, lo:lo + NSA_DH])


def _sel_win_attention(q, kvs, sel, o_cmp, small, tk=256):
    B, S, _ = q.shape
    ns = sel.shape[3]
    tk = min(tk, S)

    def kv_spec(c):
        return pl.BlockSpec((1, S, LANES), lambda b, i: (b, 0, c))

    return pl.pallas_call(
        functools.partial(_selwin_kernel, tk=tk),
        out_shape=jax.ShapeDtypeStruct((B, S, NSA_QW), F32),
        grid=(B, S // Q_BLOCK),
        in_specs=[pl.BlockSpec((1, Q_BLOCK, NSA_QW), lambda b, i: (b, i, 0)),
                  kv_spec(0), kv_spec(1), kv_spec(2), kv_spec(3),
                  pl.BlockSpec((1, NSA_GROUPS, Q_BLOCK, ns), lambda b, i: (b, 0, i, 0)),
                  pl.BlockSpec((1, Q_BLOCK, NSA_QW), lambda b, i: (b, i, 0)),
                  pl.BlockSpec((1, Q_BLOCK, LANES), lambda b, i: (b, i, 0))],
        out_specs=pl.BlockSpec((1, Q_BLOCK, NSA_QW), lambda b, i: (b, i, 0)),
        scratch_shapes=[pltpu.VMEM((S // tk, Q_BLOCK, tk), F32)],
        compiler_params=_cparams(("parallel", "parallel"), VMEM_LIMIT),
        name="nsa_sel_win_attention",
    )(q, kvs, kvs, kvs, kvs, sel, o_cmp, small)


def _mlstm_kernel(gb_ref, q_ref, k_ref, v_ref, o_ref, irow_ref, frow_ref, icol_ref, fcol_ref,
                  cwq_ref, cwk_ref, cbq_ref, cbk_ref, ng_ref, out_ref):
    L = ML_CHUNK
    S = q_ref.shape[1]
    nch = S // L
    hd = pl.program_id(1)
    gb_i = gb_ref[hd]
    gb_f = gb_ref[ML_HEADS + hd]
    kscale = ML_DH ** -0.5
    row = _iota((L, L), 0)
    col = _iota((L, L), 1)
    tri = row >= col
    halo_rows = SUBLANES

    def conv_silu(x_ref, w_ref, b_ref, c, r0):
        main = x_ref[0, pl.ds(r0, L), :]
        h0 = pl.multiple_of(jnp.maximum(r0 - halo_rows, 0), halo_rows)
        halo = jnp.where(c > 0, x_ref[0, pl.ds(h0, halo_rows), :], 0.0)
        cat = jnp.concatenate([halo, main], axis=0)
        y = main * w_ref[CONV_WIDTH - 1:CONV_WIDTH, :] + b_ref[...]
        for w in range(CONV_WIDTH - 1):
            sft = CONV_WIDTH - 1 - w
            y = y + pltpu.roll(cat, sft, 0)[halo_rows:, :] * w_ref[w:w + 1, :]
        return y * _sigmoid(y)

    def logsig(x):
        return -(jnp.maximum(-x, 0.0) + jnp.log(1.0 + jnp.exp(-jnp.abs(x))))

    def step(c, carry):
        C, n, m = carry
        r0 = pl.multiple_of(c * L, L)
        qc = conv_silu(q_ref, cwq_ref, cbq_ref, c, r0)
        kc = conv_silu(k_ref, cwk_ref, cbk_ref, c, r0) * kscale
        vc = v_ref[0, pl.ds(r0, L), :]
        li_row = irow_ref[0, 0, pl.ds(c, 1), :] + gb_i
        lf_row = logsig(frow_ref[0, 0, pl.ds(c, 1), :] + gb_f)
        li_col = icol_ref[0, 0, pl.ds(r0, L), :] + gb_i
        lf_col = logsig(fcol_ref[0, 0, pl.ds(r0, L), :] + gb_f)
        b_col = jnp.sum(jnp.where(tri, lf_row, 0.0), axis=1, keepdims=True)
        b_row = jnp.sum(jnp.where(row <= col, lf_col, 0.0), axis=0, keepdims=True)
        Dm = jnp.where(tri, b_col - b_row + li_row, NEG_INF)
        inter = b_col + m
        m_t = jnp.maximum(inter, jnp.max(Dm, axis=1, keepdims=True))
        Dw = jnp.exp(Dm - m_t)
        inter_w = jnp.exp(inter - m_t)
        qb16 = qc.astype(BF16)
        kb16 = kc.astype(BF16)
        vb16 = vc.astype(BF16)
        qk = lax.dot_general(qb16, kb16, (((1,), (1,)), ((), ())), preferred_element_type=F32) * Dw
        num = (jnp.dot(qk.astype(BF16), vb16, preferred_element_type=F32)
               + inter_w * jnp.dot(qb16, C.astype(BF16), preferred_element_type=F32))
        den = jnp.sum(qk, axis=1, keepdims=True) + inter_w * jnp.sum(qc * n, axis=1, keepdims=True)
        hc = num / jnp.maximum(jnp.abs(den), jnp.exp(-m_t))
        bL = b_col[L - 1:L, :]
        a_col = bL - b_col + li_col
        m_new = jnp.maximum(bL + m, jnp.max(a_col, axis=0, keepdims=True))
        aw = jnp.exp(a_col - m_new)
        decay = jnp.exp(bL + m - m_new)
        awk = aw * kc
        C_new = decay * C + jnp.dot(awk.T.astype(BF16), vb16, preferred_element_type=F32)
        n_new = decay * n + jnp.sum(awk, axis=0, keepdims=True)
        hn = hc * lax.rsqrt(jnp.mean(hc * hc, axis=1, keepdims=True) + NORM_EPS) * ng_ref[...]
        out_ref[0, pl.ds(r0, L), :] = hn * _sigmoid(o_ref[0, pl.ds(r0, L), :])
        return C_new, n_new, m_new

    init = (jnp.zeros((ML_DH, ML_DH), F32), jnp.zeros((1, ML_DH), F32), jnp.zeros((1, 1), F32))
    lax.fori_loop(0, nch, step, init)


def _mlstm(ml_qk, ml_v, ml_o, gate_row, gate_col, gate_b, conv_w, conv_b, norm_g):
    B, S, _ = ml_v.shape
    H = ML_HEADS
    nch = S // ML_CHUNK

    def col_spec(off):
        return pl.BlockSpec((1, S, ML_DH), lambda b, h, gb: (b, 0, off + h))

    def grow(off):
        return pl.BlockSpec((1, 1, nch, ML_CHUNK), lambda b, h, gb: (b, off + h, 0, 0))

    def gcol(off):
        return pl.BlockSpec((1, 1, S, 1), lambda b, h, gb: (b, off + h, 0, 0))

    grid_spec = pltpu.PrefetchScalarGridSpec(
        num_scalar_prefetch=1,
        grid=(B, H),
        in_specs=[col_spec(0), col_spec(H), col_spec(0), col_spec(0),
                  grow(0), grow(H), gcol(0), gcol(H),
                  pl.BlockSpec((CONV_WIDTH, ML_DH), lambda b, h, gb: (0, h)),
                  pl.BlockSpec((CONV_WIDTH, ML_DH), lambda b, h, gb: (0, H + h)),
                  pl.BlockSpec((1, ML_DH), lambda b, h, gb: (0, h)),
                  pl.BlockSpec((1, ML_DH), lambda b, h, gb: (0, H + h)),
                  pl.BlockSpec((1, ML_DH), lambda b, h, gb: (0, h))],
        out_specs=pl.BlockSpec((1, S, ML_DH), lambda b, h, gb: (b, 0, h)),
    )
    return pl.pallas_call(
        _mlstm_kernel,
        out_shape=jax.ShapeDtypeStruct((B, S, ML_W), F32),
        grid_spec=grid_spec,
        compiler_params=_cparams(("parallel", "parallel"), VMEM_LIMIT),
        name="mlstm",
    )(gate_b, ml_qk, ml_qk, ml_v, ml_o, gate_row, gate_row, gate_col, gate_col,
      conv_w, conv_w, conv_b, conv_b, norm_g)


def _pack_cols(cols, dtype):
    tm = cols[0].shape[0]
    lane = _iota((tm, LANES), 1)
    out = jnp.zeros((tm, LANES), dtype)
    for k, c in enumerate(cols):
        out = jnp.where(lane == k, c, out)
    return out


def _outproj_kernel(a_ref, b_ref, mg_ref, x_ref, gt_ref, sc_ref, sh_ref, g2_ref,
                    wa_ref, wb_ref, wo_ref, wr_ref, br_ref,
                    h_ref, n_ref, ti_ref, tg_ref):
    D = x_ref.shape[2]
    ua = jnp.dot(a_ref[0].astype(BF16), wa_ref[...], preferred_element_type=F32)
    ub = jnp.dot(b_ref[0].astype(BF16), wb_ref[...], preferred_element_type=F32)
    u = _sigmoid(mg_ref[0, :, :D]) * ua + _sigmoid(mg_ref[0, :, D:]) * ub
    mix = jnp.dot(u.astype(BF16), wo_ref[...], preferred_element_type=F32)
    h = x_ref[0] + (1.0 + gt_ref[0]) * mix
    h_ref[0] = h
    y = h * lax.rsqrt(jnp.mean(h * h, axis=-1, keepdims=True) + NORM_EPS) * g2_ref[...]
    n = y * (1.0 + sc_ref[0]) + sh_ref[0]
    n_ref[0] = n
    logits = jnp.dot(n, wr_ref[...], precision=HIGHEST, preferred_element_type=F32) + br_ref[...]
    tm, ne = logits.shape
    lane = _iota((tm, ne), 1).astype(F32)
    work = logits
    vals, idxs = [], []
    for _ in range(TOP_K):
        mx = jnp.max(work, axis=-1, keepdims=True)
        ix = jnp.min(jnp.where(work == mx, lane, float(ne)), axis=-1, keepdims=True)
        vals.append(mx)
        idxs.append(ix)
        work = jnp.where(lane == ix, -jnp.inf, work)
    es = [jnp.exp(v - vals[0]) for v in vals]
    tot = es[0] + es[1] + es[2] + es[3]
    ti_ref[0] = _pack_cols(idxs, F32).astype(I32)
    tg_ref[0] = _pack_cols([e / tot for e in es], F32)


def _out_projection(o_nsa, o_ml, merge, x, gt1, sc2, sh2, g2, wa, wb, wo, wr, br, tm=256):
    B, S, D = x.shape
    vec = pl.BlockSpec((1, 1, D), lambda b, i: (b, 0, 0))

    def full(shape):
        return pl.BlockSpec(shape, lambda b, i: (0,) * len(shape))

    def rows(w):
        return pl.BlockSpec((1, tm, w), lambda b, i: (b, i, 0))

    return pl.pallas_call(
        _outproj_kernel,
        out_shape=[jax.ShapeDtypeStruct((B, S, D), F32), jax.ShapeDtypeStruct((B, S, D), F32),
                   jax.ShapeDtypeStruct((B, S, LANES), I32), jax.ShapeDtypeStruct((B, S, LANES), F32)],
        grid=(B, S // tm),
        in_specs=[rows(NSA_QW), rows(ML_W), rows(2 * D), rows(D), vec, vec, vec, full((1, D)),
                  full(wa.shape), full(wb.shape), full(wo.shape), full(wr.shape), full((1, N_EXPERTS))],
        out_specs=[rows(D), rows(D), rows(LANES), rows(LANES)],
        compiler_params=_cparams(("parallel", "parallel"), VMEM_LIMIT),
        name="out_projection_router",
    )(o_nsa, o_ml, merge, x, gt1, sc2, sh2, g2, wa, wb, wo, wr, br)


def _rank_kernel(ti_ref, rank_ref, cnt_ref, carry_ref):
    tt = ti_ref.shape[0]

    @pl.when(pl.program_id(0) == 0)
    def _():
        carry_ref[...] = jnp.zeros_like(carry_ref)

    ti = ti_ref[...]
    lane = _iota((tt, LANES), 1)
    hots = [ti[:, k:k + 1] == lane for k in range(TOP_K)]
    member = jnp.zeros((tt, LANES), F32)
    for hot in hots:
        member = member + jnp.where(hot, 1.0, 0.0)
    strict = jnp.where(_iota((tt, tt), 0) > _iota((tt, tt), 1), 1.0, 0.0).astype(BF16)
    before = jnp.dot(strict, member.astype(BF16), preferred_element_type=F32) + carry_ref[0:1, :]
    ranks = [jnp.sum(jnp.where(hot, before, 0.0), axis=-1, keepdims=True) for hot in hots]
    rank_ref[...] = _pack_cols(ranks, F32)
    carry_ref[...] = carry_ref[...] + jnp.sum(member, axis=0, keepdims=True)
    cnt_ref[...] = carry_ref[...]


def _expert_rank(ti, tt=256):
    T = ti.shape[0]
    return pl.pallas_call(
        _rank_kernel,
        out_shape=[jax.ShapeDtypeStruct((T, LANES), F32), jax.ShapeDtypeStruct((SUBLANES, LANES), F32)],
        grid=(T // tt,),
        in_specs=[pl.BlockSpec((tt, LANES), lambda i: (i, 0))],
        out_specs=[pl.BlockSpec((tt, LANES), lambda i: (i, 0)),
                   pl.BlockSpec((SUBLANES, LANES), lambda i: (0, 0))],
        scratch_shapes=[pltpu.VMEM((SUBLANES, LANES), F32)],
        compiler_params=_cparams(("arbitrary",)),
        name="moe_rank",
    )(ti)


def _dest_kernel(cnt_ref, ti_ref, rank_ref, dest_ref, be_ref):
    tt = ti_ref.shape[0]
    nbp = be_ref.shape[0]
    cnt = cnt_ref[...].astype(I32)
    padded = ((cnt + (MOE_ROWS - 1)) >> 8) << 8
    lane8 = _iota((SUBLANES, LANES), 1)
    pad_end = padded
    sft = 1
    while sft < LANES:
        pad_end = pad_end + jnp.where(lane8 >= sft, pltpu.roll(pad_end, sft, 1), 0)
        sft *= 2
    pad_start = (pad_end - padded)[0:1, :].astype(F32)
    ti = ti_ref[...]
    lane = _iota((tt, LANES), 1)
    rank = rank_ref[...]
    dests = [jnp.sum(jnp.where(ti[:, k:k + 1] == lane, pad_start, 0.0), axis=-1, keepdims=True) + rank[:, k:k + 1]
             for k in range(TOP_K)]
    dest_ref[...] = _pack_cols(dests, F32).astype(I32)
    first_row = _iota((nbp, LANES), 0) * MOE_ROWS
    done = (pad_end[0:1, :] <= first_row) & (_iota((nbp, LANES), 1) < N_EXPERTS)
    be = jnp.minimum(jnp.sum(jnp.where(done, 1.0, 0.0), axis=-1, keepdims=True), float(N_EXPERTS - 1))
    be_ref[...] = jnp.broadcast_to(be, (nbp, LANES)).astype(I32)


def _expert_dest(cnt, ti, rank, n_blk, tt=256):
    T = ti.shape[0]
    return pl.pallas_call(
        _dest_kernel,
        out_shape=[jax.ShapeDtypeStruct((T, LANES), I32), jax.ShapeDtypeStruct((n_blk, LANES), I32)],
        grid=(T // tt,),
        in_specs=[pl.BlockSpec((SUBLANES, LANES), lambda i: (0, 0)),
                  pl.BlockSpec((tt, LANES), lambda i: (i, 0)),
                  pl.BlockSpec((tt, LANES), lambda i: (i, 0))],
        out_specs=[pl.BlockSpec((tt, LANES), lambda i: (i, 0)),
                   pl.BlockSpec((n_blk, LANES), lambda i: (0, 0))],
        compiler_params=_cparams(("arbitrary",)),
        name="moe_dest",
    )(cnt, ti, rank)


def _scatter_kernel(dest_ref, src_ref, init_ref, dst_ref, sem):
    del init_ref
    n = dest_ref.shape[0]
    tt = n // TOP_K
    t0 = pl.program_id(0) * tt

    def issue(t, c):
        for k in range(TOP_K):
            pltpu.make_async_copy(src_ref.at[t0 + t], dst_ref.at[dest_ref[t * TOP_K + k]], sem).start()
        return c

    lax.fori_loop(0, tt, issue, 0)

    def drain(t, c):
        for k in range(TOP_K):
            pltpu.make_async_copy(src_ref.at[t0 + t], dst_ref.at[dest_ref[t * TOP_K + k]], sem).wait()
        return c

    lax.fori_loop(0, tt, drain, 0)


def _scatter_rows(dest_flat, rows, init, tt=512):
    T = rows.shape[0]
    return pl.pallas_call(
        _scatter_kernel,
        out_shape=jax.ShapeDtypeStruct(init.shape, init.dtype),
        grid=(T // tt,),
        in_specs=[pl.BlockSpec((tt * TOP_K,), lambda i: (i,), memory_space=pltpu.SMEM),
                  pl.BlockSpec(memory_space=pl.ANY),
                  pl.BlockSpec(memory_space=pl.ANY)],
        out_specs=pl.BlockSpec(memory_space=pl.ANY),
        scratch_shapes=[pltpu.SemaphoreType.DMA],
        input_output_aliases={2: 0},
        compiler_params=_cparams(("arbitrary",)),
        name="moe_scatter_rows",
    )(dest_flat, rows, init)


def _expert_kernel(be_ref, x_ref, w1_ref, b1_ref, w2_ref, b2_ref, y_ref):
    del be_ref
    h = jnp.dot(x_ref[...].astype(BF16), w1_ref[0], preferred_element_type=F32) + b1_ref[0]
    glu = jnp.minimum(h[:, :D_EXPERT], SWIGLU_LIMIT)
    lin = jnp.clip(h[:, D_EXPERT:], -SWIGLU_LIMIT, SWIGLU_LIMIT)
    act = glu * _sigmoid(SWIGLU_ALPHA * glu) * (lin + 1.0)
    y_ref[...] = jnp.dot(act.astype(BF16), w2_ref[0], preferred_element_type=F32) + b2_ref[0]


def _expert_ffn(blk_expert, x_rows, w1, b1, w2, b2):
    P, D = x_rows.shape
    nb = P // MOE_ROWS
    grid_spec = pltpu.PrefetchScalarGridSpec(
        num_scalar_prefetch=1,
        grid=(nb,),
        in_specs=[pl.BlockSpec((MOE_ROWS, D), lambda i, be: (i, 0)),
                  pl.BlockSpec((1, D, 2 * D_EXPERT), lambda i, be: (be[i], 0, 0)),
                  pl.BlockSpec((1, 1, 2 * D_EXPERT), lambda i, be: (be[i], 0, 0)),
                  pl.BlockSpec((1, D_EXPERT, D), lambda i, be: (be[i], 0, 0)),
                  pl.BlockSpec((1, 1, D), lambda i, be: (be[i], 0, 0))],
        out_specs=pl.BlockSpec((MOE_ROWS, D), lambda i, be: (i, 0)),
    )
    return pl.pallas_call(
        _expert_kernel,
        out_shape=jax.ShapeDtypeStruct((P, D), F32),
        grid_spec=grid_spec,
        compiler_params=_cparams(("parallel",), VMEM_LIMIT),
        name="moe_expert_ffn",
    )(blk_expert, x_rows, w1, b1, w2, b2)


def _combine_kernel(dest_ref, gate_ref, y_ref, o_ref, buf_ref, sem):
    n = dest_ref.shape[0]
    tt = n // TOP_K

    def issue(t, c):
        for k in range(TOP_K):
            pltpu.make_async_copy(y_ref.at[dest_ref[t * TOP_K + k]], buf_ref.at[k, t], sem).start()
        return c

    lax.fori_loop(0, tt, issue, 0)

    def drain(t, c):
        for k in range(TOP_K):
            pltpu.make_async_copy(y_ref.at[dest_ref[t * TOP_K + k]], buf_ref.at[k, t], sem).wait()
        return c

    lax.fori_loop(0, tt, drain, 0)

    def mix(t, c):
        acc = gate_ref[t * TOP_K] * buf_ref[0, t]
        for k in range(1, TOP_K):
            acc = acc + gate_ref[t * TOP_K + k] * buf_ref[k, t]
        o_ref[t] = acc
        return c

    lax.fori_loop(0, tt, mix, 0)


def _combine_rows(dest_flat, gate_flat, y_rows, T, tt=256):
    slab = y_rows.shape[1:]
    return pl.pallas_call(
        _combine_kernel,
        out_shape=jax.ShapeDtypeStruct((T,) + slab, F32),
        grid=(T // tt,),
        in_specs=[pl.BlockSpec((tt * TOP_K,), lambda i: (i,), memory_space=pltpu.SMEM),
                  pl.BlockSpec((tt * TOP_K,), lambda i: (i,), memory_space=pltpu.SMEM),
                  pl.BlockSpec(memory_space=pl.ANY)],
        out_specs=pl.BlockSpec((tt,) + slab, lambda i: (i, 0, 0)),
        scratch_shapes=[pltpu.VMEM((TOP_K, tt) + slab, F32), pltpu.SemaphoreType.DMA],
        compiler_params=_cparams(("arbitrary",)),
        name="moe_combine_rows",
    )(dest_flat, gate_flat, y_rows)


def _final_kernel(h_ref, y_ref, gt_ref, g_ref, o_ref, *, normalize):
    h = h_ref[0] + (1.0 + gt_ref[0]) * y_ref[0]
    if normalize:
        h = h * lax.rsqrt(jnp.mean(h * h, axis=-1, keepdims=True) + NORM_EPS) * g_ref[...]
    o_ref[0] = h


def _residual(h1, y, gt2, g, normalize, tm=512):
    B, S, D = h1.shape
    rows = pl.BlockSpec((1, tm, D), lambda b, i: (b, i, 0))
    return pl.pallas_call(
        functools.partial(_final_kernel, normalize=normalize),
        out_shape=jax.ShapeDtypeStruct((B, S, D), F32),
        grid=(B, S // tm),
        in_specs=[rows, rows, pl.BlockSpec((1, 1, D), lambda b, i: (b, 0, 0)),
                  pl.BlockSpec((1, D), lambda b, i: (0, 0))],
        out_specs=rows,
        compiler_params=_cparams(("parallel", "parallel")),
        name="final_norm",
    )(h1, y, gt2, g)


def _overlap_matrix(S):
    nc = S // CMP_STRIDE - 1
    ns = S // SEL_BLOCK
    c0 = np.arange(nc) * CMP_STRIDE
    s0 = np.arange(ns) * SEL_BLOCK
    ov = np.clip(np.minimum(c0[:, None] + CMP_LEN, s0[None, :] + SEL_BLOCK)
                 - np.maximum(c0[:, None], s0[None, :]), 0, None) / CMP_LEN
    return np.concatenate([ov, np.zeros((1, ns))], axis=0).astype(np.float32)


def _layer(h, mods, norm1_g, w_in, pe_k, pe_v, cw1_k, cw2_k, cw1_v, cw2_v, conv_w, conv_b, gate_b, ml_norm_g,
           w_up_nsa, w_up_ml, w_out, norm2_g, router_w, router_b, exp_w1, exp_b1, exp_w2, exp_b2):
    B, S, D = h.shape
    T = B * S
    sh1, sc1, gt1, sh2, sc2, gt2 = mods

    w_r = jnp.concatenate([w_in[:, 0:1280], w_in[:, 1304:2328], w_in[:, 2328:2840], w_in[:, 2848:3360],
                           w_in[:, 3360:5408], w_in[:, 1280:1304], w_in[:, 2840:2848],
                           jnp.zeros((D, IN_WR - 5408), w_in.dtype)], axis=1).astype(BF16)
    q, kvc, kvs, ml_qk, ml_v, ml_o, merge, small = _in_projection(h, norm1_g.reshape(1, D), sc1, sh1, w_r)

    nb16 = S // CMP_STRIDE
    t16 = kvc.reshape(B, S, 4, NSA_DH).transpose(0, 2, 1, 3).reshape(B, 4, nb16, CMP_STRIDE * NSA_DH)
    pe = jnp.stack([pe_k.reshape(2, CMP_STRIDE * NSA_DH), pe_v.reshape(2, CMP_STRIDE * NSA_DH)])
    cmp = _compress(t16, pe, jnp.stack([cw1_k, cw1_v]), jnp.stack([cw2_k, cw2_v]))
    n_sel = min(SEL_TOPN, S // SEL_BLOCK)
    o_cmp, sel = _cmp_attention(q, cmp, jnp.asarray(_overlap_matrix(S)), n_sel)
    o_nsa = _sel_win_attention(q, kvs, sel, o_cmp, small)

    gates = small[:, :, 3 * NSA_HEADS:3 * NSA_HEADS + 2 * ML_HEADS].transpose(0, 2, 1)
    gate_row = gates.reshape(B, 2 * ML_HEADS, S // ML_CHUNK, ML_CHUNK)
    gate_col = gates.reshape(B, 2 * ML_HEADS, S, 1)
    o_ml = _mlstm(ml_qk, ml_v, ml_o, gate_row, gate_col, gate_b, conv_w, conv_b.reshape(1, -1),
                  ml_norm_g.reshape(1, -1))

    h1, n2, ti, tg = _out_projection(o_nsa, o_ml, merge, h, gt1, sc2, sh2, norm2_g.reshape(1, D),
                                     w_up_nsa.astype(BF16), w_up_ml.astype(BF16), w_out.astype(BF16),
                                     router_w, router_b.reshape(1, N_EXPERTS))

    A = T * TOP_K
    n_blk = -(-(A + N_EXPERTS * (MOE_ROWS - 1)) // MOE_ROWS)
    P = n_blk * MOE_ROWS
    ti2 = ti.reshape(T, LANES)
    rank, cnt = _expert_rank(ti2)
    dest, blk_e = _expert_dest(cnt, ti2, rank, n_blk)
    dest_flat = dest[:, :TOP_K].reshape(A)
    gate_flat = tg.reshape(T, LANES)[:, :TOP_K].reshape(A)
    slab = (SUBLANES, D // SUBLANES)
    x_rows = _scatter_rows(dest_flat, n2.reshape((T,) + slab), jnp.zeros((P,) + slab, F32))
    y_rows = _expert_ffn(blk_e[:, 0], x_rows.reshape(P, D), exp_w1.astype(BF16), exp_b1.reshape(N_EXPERTS, 1, -1),
                         exp_w2.astype(BF16), exp_b2.reshape(N_EXPERTS, 1, -1))
    y = _combine_rows(dest_flat, gate_flat, y_rows.reshape((P,) + slab), T)
    return h1, y.reshape(B, S, D), gt2


def kernel(x, c, ada_w, ada_b, norm1_g, w_in, nsa_pe_k, nsa_pe_v, nsa_cmp_w1_k, nsa_cmp_w2_k, nsa_cmp_w1_v, nsa_cmp_w2_v, ml_conv_w, ml_conv_b, ml_gate_b, ml_norm_g, w_up_nsa, w_up_ml, w_out, norm2_g, router_w, router_b, exp_w1, exp_b1, exp_w2, exp_b2, final_g):
    B, S, D = x.shape
    depth = ada_w.shape[0]
    h = x
    for layer in range(depth):
        mod = _modulation(c, ada_w[layer], ada_b[layer])
        mods = [mod[:, i * D:(i + 1) * D].reshape(B, 1, D) for i in range(6)]
        h1, y, gt2 = _layer(h, mods, norm1_g[layer], w_in[layer], nsa_pe_k[layer], nsa_pe_v[layer],
                            nsa_cmp_w1_k[layer], nsa_cmp_w2_k[layer], nsa_cmp_w1_v[layer], nsa_cmp_w2_v[layer],
                            ml_conv_w[layer], ml_conv_b[layer], ml_gate_b[layer], ml_norm_g[layer],
                            w_up_nsa[layer], w_up_ml[layer], w_out[layer], norm2_g[layer], router_w[layer],
                            router_b[layer], exp_w1[layer], exp_b1[layer], exp_w2[layer], exp_b2[layer])
        h = _residual(h1, y, gt2, final_g.reshape(1, D), normalize=(layer + 1 == depth))
    return h
```

```python
import functools

import numpy as np
import jax
import jax.numpy as jnp
from jax import lax
from jax.experimental import pallas as pl
from jax.experimental.pallas import tpu as pltpu

F32 = jnp.float32
BF16 = jnp.bfloat16
I32 = jnp.int32
HIGHEST = lax.Precision.HIGHEST

D_MODEL = 1024
NSA_HEADS = 8
NSA_GROUPS = 2
NSA_REP = NSA_HEADS // NSA_GROUPS
NSA_DH = 64
NSA_SCALE = NSA_DH ** -0.5
CMP_STRIDE = 16
CMP_LEN = 32
CMP_HIDDEN = 128
SEL_BLOCK = 64
SEL_TOPN = 16
WINDOW = 512
Q_BLOCK = 128
ML_HEADS = 4
ML_DH = 128
ML_CHUNK = 64
CONV_WIDTH = 4
N_EXPERTS = 32
TOP_K = 4
D_EXPERT = 1024
SWIGLU_LIMIT = 7.0
SWIGLU_ALPHA = 1.702
MOE_ROWS = 256
NORM_EPS = 1e-6
NEG_INF = -1e30
FORCE_BONUS = 1e4

NSA_QW = NSA_HEADS * NSA_DH
NSA_KVW = NSA_GROUPS * NSA_DH
ML_W = ML_HEADS * ML_DH
LANES = 128
SUBLANES = 8
VMEM_LIMIT = 52 * 1024 * 1024

_C_Q = (0, 512)
_C_KVC = (512, 768)
_C_KVS = (768, 1280)
_C_MQK = (1280, 2304)
_C_MV = (2304, 2816)
_C_MO = (2816, 3328)
_C_MG = (3328, 5376)
_C_SM = (5376, 5504)
IN_WR = 5504


def _sigmoid(x):
    return 1.0 / (1.0 + jnp.exp(-x))


def _iota(shape, dim):
    return lax.broadcasted_iota(I32, shape, dim)


def _cparams(sem, vmem=None):
    return pltpu.CompilerParams(dimension_semantics=sem, vmem_limit_bytes=vmem)


def _mod_kernel(c_ref, w_ref, b_ref, o_ref):
    c = c_ref[...]
    ca = c * _sigmoid(c)
    o_ref[...] = jnp.dot(ca, w_ref[...], precision=HIGHEST, preferred_element_type=F32) + b_ref[...]


def _modulation(c, ada_w, ada_b):
    B, D = c.shape
    n = ada_w.shape[1] // D
    return pl.pallas_call(
        _mod_kernel,
        out_shape=jax.ShapeDtypeStruct((B, n * D), F32),
        grid=(n,),
        in_specs=[pl.BlockSpec((B, D), lambda j: (0, 0)),
                  pl.BlockSpec((D, D), lambda j: (0, j)),
                  pl.BlockSpec((1, D), lambda j: (0, j))],
        out_specs=pl.BlockSpec((B, D), lambda j: (0, j)),
        compiler_params=_cparams(("parallel",)),
        name="modulation",
    )(c, ada_w, ada_b.reshape(1, n * D))


def _inproj_kernel(x_ref, g_ref, sc_ref, sh_ref, w_ref,
                   q_ref, kvc_ref, kvs_ref, mqk_ref, mv_ref, mo_ref, mg_ref, sm_ref):
    x = x_ref[0]
    y = x * lax.rsqrt(jnp.mean(x * x, axis=-1, keepdims=True) + NORM_EPS) * g_ref[...]
    n = (y * (1.0 + sc_ref[0]) + sh_ref[0]).astype(BF16)

    def proj(c):
        return jnp.dot(n, w_ref[:, c[0]:c[1]], preferred_element_type=F32)

    q_ref[0] = (proj(_C_Q) * NSA_SCALE).astype(BF16)
    kvc_ref[0] = proj(_C_KVC)
    kvs_ref[0] = proj(_C_KVS).astype(BF16)
    mqk_ref[0] = proj(_C_MQK)
    mv_ref[0] = proj(_C_MV)
    mo_ref[0] = proj(_C_MO)
    mg_ref[0] = proj(_C_MG)
    sm_ref[0] = proj(_C_SM)


def _in_projection(x, g, sc, sh, w_r, tm=256):
    B, S, D = x.shape
    widths = [c[1] - c[0] for c in (_C_Q, _C_KVC, _C_KVS, _C_MQK, _C_MV, _C_MO, _C_MG, _C_SM)]
    dtypes = [BF16, F32, BF16, F32, F32, F32, F32, F32]
    vec = pl.BlockSpec((1, 1, D), lambda b, i: (b, 0, 0))
    return pl.pallas_call(
        _inproj_kernel,
        out_shape=[jax.ShapeDtypeStruct((B, S, w), dt) for w, dt in zip(widths, dtypes)],
        grid=(B, S // tm),
        in_specs=[pl.BlockSpec((1, tm, D), lambda b, i: (b, i, 0)),
                  pl.BlockSpec((1, D), lambda b, i: (0, 0)),
                  vec, vec,
                  pl.BlockSpec((D, IN_WR), lambda b, i: (0, 0))],
        out_specs=[pl.BlockSpec((1, tm, w), lambda b, i: (b, i, 0)) for w in widths],
        compiler_params=_cparams(("parallel", "parallel"), VMEM_LIMIT),
        name="in_projection",
    )(x, g, sc, sh, w_r)


def _compress_kernel(t_ref, pe_ref, w1_ref, w2_ref, o_ref):
    half = CMP_STRIDE * NSA_DH
    t = t_ref[0, 0]
    pe = pe_ref[0]
    nb = t.shape[0]
    a = jnp.dot((t + pe[0:1]).astype(BF16), w1_ref[0, :half, :].astype(BF16), preferred_element_type=F32)
    b = jnp.dot((t + pe[1:2]).astype(BF16), w1_ref[0, half:, :].astype(BF16), preferred_element_type=F32)
    h = a + pltpu.roll(b, nb - 1, 0)
    gl = 0.5 * h * (1.0 + jnp.tanh(np.sqrt(2.0 / np.pi).astype(np.float32) * (h + 0.044715 * (h * h * h))))
    o_ref[0, 0] = jnp.dot(gl.astype(BF16), w2_ref[0].astype(BF16), preferred_element_type=F32)


def _compress(t16, pe, w1, w2):
    B, _, nb, wdt = t16.shape
    return pl.pallas_call(
        _compress_kernel,
        out_shape=jax.ShapeDtypeStruct((B, 4, nb, NSA_DH), F32),
        grid=(B, 4),
        in_specs=[pl.BlockSpec((1, 1, nb, wdt), lambda b, j: (b, j, 0, 0)),
                  pl.BlockSpec((1, 2, wdt), lambda b, j: (j // 2, 0, 0)),
                  pl.BlockSpec((1, CMP_LEN * NSA_DH, CMP_HIDDEN), lambda b, j: (j // 2, 0, 0)),
                  pl.BlockSpec((1, CMP_HIDDEN, NSA_DH), lambda b, j: (j // 2, 0, 0))],
        out_specs=pl.BlockSpec((1, 1, nb, NSA_DH), lambda b, j: (b, j, 0, 0)),
        compiler_params=_cparams(("parallel", "parallel")),
        name="nsa_compress",
    )(t16, pe, w1, w2)


def _nsa_kernel(q_ref, kc_ref, vct_ref, ovt_ref, ks_ref, vst_ref, kw_ref, vwt_ref, sm_ref, o_ref, *, n_sel, tk):
    qb = Q_BLOCK
    ncp = kc_ref.shape[1]
    ns = ovt_ref.shape[0]
    q0 = pl.program_id(1) * qb
    t_l = q0 + _iota((1, qb), 1)
    sub = _iota((LANES, qb), 0)
    gT = _sigmoid(sm_ref[0].T)

    qT = []
    for c in range(NSA_HEADS // 2):
        qc = q_ref[0, :, c * LANES:(c + 1) * LANES].astype(F32).T
        for e in range(2):
            g = (2 * c + e) // NSA_REP
            x = qc if e == g else pltpu.roll(qc, NSA_DH, 0)
            qT.append(jnp.where((sub >> 6) == g, x, 0.0).astype(BF16))

    kc = kc_ref[0]
    vct = vct_ref[0]
    cmask = (_iota((ncp, qb), 0) * CMP_STRIDE + (CMP_LEN - 1)) <= t_l
    blk = _iota((ns, qb), 0)
    cur = t_l >> 6
    forced = (blk == 0) | (blk == cur) | (blk == cur - 1)
    ocT = []
    selT = []
    for g in range(NSA_GROUPS):
        psum = jnp.zeros((ncp, qb), F32)
        for r in range(NSA_REP):
            s = jnp.dot(kc, qT[g * NSA_REP + r], preferred_element_type=F32)
            s = jnp.where(cmask, s, NEG_INF)
            e = jnp.exp(s - jnp.max(s, axis=0, keepdims=True))
            p = e * (1.0 / jnp.sum(e, axis=0, keepdims=True))
            p = jnp.where(cmask, p, 0.0)
            ocT.append(jnp.dot(vct, p.astype(BF16), preferred_element_type=F32))
            psum = psum + p
        imp = jnp.dot(ovt_ref[...], psum, precision=HIGHEST, preferred_element_type=F32)
        imp = jnp.where(blk > cur, NEG_INF, imp + jnp.where(forced, FORCE_BONUS, 0.0))
        rank = jnp.zeros((ns, qb), F32)
        for i in range(ns):
            row = imp[i:i + 1, :]
            beats = (row > imp) | ((row == imp) & (blk > i))
            rank = rank + jnp.where(beats, 1.0, 0.0)
        selT.append(jnp.where(rank < n_sel, 1.0, 0.0).astype(BF16))

    wlen = WINDOW + qb
    wt0 = jnp.maximum(q0 - WINDOW, 0) // LANES
    wstart = pl.multiple_of(wt0 * LANES, LANES)
    kwin = kw_ref[0, pl.ds(wstart, wlen), :]
    rel = t_l - (wstart + _iota((wlen, qb), 0))
    wmask = (rel >= 0) & (rel < WINDOW)
    partial = []
    for h in range(NSA_HEADS):
        lo = (h // NSA_REP) * NSA_DH
        s = jnp.dot(kwin, qT[h], preferred_element_type=F32)
        s = jnp.where(wmask, s, NEG_INF)
        p = jnp.exp(s - jnp.max(s, axis=0, keepdims=True))
        pb = p.astype(BF16)
        ow = jnp.dot(vwt_ref[0, wt0], pb[:LANES], preferred_element_type=F32)
        for i in range(1, wlen // LANES):
            ow = ow + jnp.dot(vwt_ref[0, wt0 + i], pb[i * LANES:(i + 1) * LANES], preferred_element_type=F32)
        ow = ow * (1.0 / jnp.sum(p, axis=0, keepdims=True))
        partial.append(gT[h:h + 1] * ocT[h][lo:lo + NSA_DH]
                       + gT[2 * NSA_HEADS + h:2 * NSA_HEADS + h + 1] * ow[lo:lo + NSA_DH])

    nt = (q0 + qb + tk - 1) // tk
    vt_per = tk // LANES

    def step(j, carry):
        r0 = pl.multiple_of(j * tk, tk)
        kt = ks_ref[0, pl.ds(r0, tk), :]
        causal = (r0 + _iota((tk, qb), 0)) <= t_l
        expand = jnp.where(((r0 + _iota((tk, ns), 0)) >> 6) == _iota((tk, ns), 1), 1.0, 0.0).astype(BF16)
        new = []
        for g in range(NSA_GROUPS):
            ok = (jnp.dot(expand, selT[g], preferred_element_type=F32) > 0.5) & causal
            for r in range(NSA_REP):
                m, l, acc = carry[g * NSA_REP + r]
                s = jnp.dot(kt, qT[g * NSA_REP + r], preferred_element_type=F32)
                s = jnp.where(ok, s, NEG_INF)
                m_new = jnp.maximum(m, jnp.max(s, axis=0, keepdims=True))
                alpha = jnp.exp(m - m_new)
                p = jnp.exp(s - m_new)
                pb = p.astype(BF16)
                pv = jnp.dot(vst_ref[0, j * vt_per], pb[:LANES], preferred_element_type=F32)
                for i in range(1, vt_per):
                    pv = pv + jnp.dot(vst_ref[0, j * vt_per + i], pb[i * LANES:(i + 1) * LANES],
                                      preferred_element_type=F32)
                new.append((m_new, alpha * l + jnp.sum(p, axis=0, keepdims=True), alpha * acc + pv))
        return tuple(new)

    init = tuple((jnp.full((1, qb), NEG_INF, F32), jnp.zeros((1, qb), F32), jnp.zeros((LANES, qb), F32))
                 for _ in range(NSA_HEADS))
    final = lax.fori_loop(0, nt, step, init)

    for c in range(NSA_HEADS // 2):
        parts = []
        for e in range(2):
            h = 2 * c + e
            lo = (h // NSA_REP) * NSA_DH
            _, l, acc = final[h]
            o_sel = acc[lo:lo + NSA_DH] * (1.0 / l)
            parts.append(partial[h] + gT[NSA_HEADS + h:NSA_HEADS + h + 1] * o_sel)
        o_ref[0, :, c * LANES:(c + 1) * LANES] = jnp.concatenate(parts, axis=0).T


def _nsa_attention(q, kc, vct, ovt, kvs, vst, vwt, small, n_sel, tk=256):
    B, S, _ = q.shape
    ns, ncp = ovt.shape
    tk = min(tk, S)
    assert S >= WINDOW + Q_BLOCK and S % tk == 0 and tk % LANES == 0
    nt128 = S // LANES

    def whole(shape):
        return pl.BlockSpec((1,) + shape, lambda b, i: (b,) + (0,) * len(shape))

    return pl.pallas_call(
        functools.partial(_nsa_kernel, n_sel=n_sel, tk=tk),
        out_shape=jax.ShapeDtypeStruct((B, S, NSA_QW), F32),
        grid=(B, S // Q_BLOCK),
        in_specs=[pl.BlockSpec((1, Q_BLOCK, NSA_QW), lambda b, i: (b, i, 0)),
                  whole((ncp, LANES)), whole((LANES, ncp)),
                  pl.BlockSpec((ns, ncp), lambda b, i: (0, 0)),
                  pl.BlockSpec((1, S, LANES), lambda b, i: (b, 0, 0)),
                  whole((nt128, LANES, LANES)),
                  pl.BlockSpec((1, S, LANES), lambda b, i: (b, 0, 2)),
                  whole((nt128, LANES, LANES)),
                  pl.BlockSpec((1, Q_BLOCK, LANES), lambda b, i: (b, i, 0))],
        out_specs=pl.BlockSpec((1, Q_BLOCK, NSA_QW), lambda b, i: (b, i, 0)),
        compiler_params=_cparams(("parallel", "parallel"), VMEM_LIMIT),
        name="nsa_attention",
    )(q, kc, vct, ovt, kvs, vst, kvs, vwt, small)


def _mlstm_kernel(gb_ref, q_ref, k_ref, v_ref, o_ref, irow_ref, frow_ref, icol_ref, fcol_ref,
                  cwq_ref, cwk_ref, cbq_ref, cbk_ref, ng_ref, out_ref):
    L = ML_CHUNK
    S = q_ref.shape[1]
    nch = S // L
    hd = pl.program_id(1)
    gb_i = gb_ref[hd]
    gb_f = gb_ref[ML_HEADS + hd]
    kscale = ML_DH ** -0.5
    row = _iota((L, L), 0)
    col = _iota((L, L), 1)
    tri = row >= col
    halo_rows = SUBLANES

    def conv_silu(x_ref, w_ref, b_ref, c, r0):
        main = x_ref[0, pl.ds(r0, L), :]
        h0 = pl.multiple_of(jnp.maximum(r0 - halo_rows, 0), halo_rows)
        halo = jnp.where(c > 0, x_ref[0, pl.ds(h0, halo_rows), :], 0.0)
        cat = jnp.concatenate([halo, main], axis=0)
        y = main * w_ref[CONV_WIDTH - 1:CONV_WIDTH, :] + b_ref[...]
        for w in range(CONV_WIDTH - 1):
            sft = CONV_WIDTH - 1 - w
            y = y + pltpu.roll(cat, sft, 0)[halo_rows:, :] * w_ref[w:w + 1, :]
        return y * _sigmoid(y)

    def logsig(x):
        return -(jnp.maximum(-x, 0.0) + jnp.log(1.0 + jnp.exp(-jnp.abs(x))))

    def step(c, carry):
        C, n, m = carry
        r0 = pl.multiple_of(c * L, L)
        qc = conv_silu(q_ref, cwq_ref, cbq_ref, c, r0)
        kc = conv_silu(k_ref, cwk_ref, cbk_ref, c, r0) * kscale
        vc = v_ref[0, pl.ds(r0, L), :]
        li_row = irow_ref[0, 0, pl.ds(c, 1), :] + gb_i
        lf_row = logsig(frow_ref[0, 0, pl.ds(c, 1), :] + gb_f)
        li_col = icol_ref[0, 0, pl.ds(r0, L), :] + gb_i
        lf_col = logsig(fcol_ref[0, 0, pl.ds(r0, L), :] + gb_f)
        b_col = jnp.sum(jnp.where(tri, lf_row, 0.0), axis=1, keepdims=True)
        b_row = jnp.sum(jnp.where(row <= col, lf_col, 0.0), axis=0, keepdims=True)
        Dm = jnp.where(tri, b_col - b_row + li_row, NEG_INF)
        inter = b_col + m
        m_t = jnp.maximum(inter, jnp.max(Dm, axis=1, keepdims=True))
        Dw = jnp.exp(Dm - m_t)
        inter_w = jnp.exp(inter - m_t)
        qb16 = qc.astype(BF16)
        kb16 = kc.astype(BF16)
        vb16 = vc.astype(BF16)
        qk = lax.dot_general(qb16, kb16, (((1,), (1,)), ((), ())), preferred_element_type=F32) * Dw
        num = (jnp.dot(qk.astype(BF16), vb16, preferred_element_type=F32)
               + inter_w * jnp.dot(qb16, C.astype(BF16), preferred_element_type=F32))
        den = jnp.sum(qk, axis=1, keepdims=True) + inter_w * jnp.sum(qc * n, axis=1, keepdims=True)
        hc = num / jnp.maximum(jnp.abs(den), jnp.exp(-m_t))
        bL = b_col[L - 1:L, :]
        a_col = bL - b_col + li_col
        m_new = jnp.maximum(bL + m, jnp.max(a_col, axis=0, keepdims=True))
        aw = jnp.exp(a_col - m_new)
        decay = jnp.exp(bL + m - m_new)
        awk = aw * kc
        C_new = decay * C + jnp.dot(awk.T.astype(BF16), vb16, preferred_element_type=F32)
        n_new = decay * n + jnp.sum(awk, axis=0, keepdims=True)
        hn = hc * lax.rsqrt(jnp.mean(hc * hc, axis=1, keepdims=True) + NORM_EPS) * ng_ref[...]
        out_ref[0, pl.ds(r0, L), :] = hn * _sigmoid(o_ref[0, pl.ds(r0, L), :])
        return C_new, n_new, m_new

    init = (jnp.zeros((ML_DH, ML_DH), F32), jnp.zeros((1, ML_DH), F32), jnp.zeros((1, 1), F32))
    lax.fori_loop(0, nch, step, init)


def _mlstm(ml_qk, ml_v, ml_o, gate_row, gate_col, gate_b, conv_w, conv_b, norm_g):
    B, S, _ = ml_v.shape
    H = ML_HEADS
    nch = S // ML_CHUNK

    def col_spec(off):
        return pl.BlockSpec((1, S, ML_DH), lambda b, h, gb: (b, 0, off + h))

    def grow(off):
        return pl.BlockSpec((1, 1, nch, ML_CHUNK), lambda b, h, gb: (b, off + h, 0, 0))

    def gcol(off):
        return pl.BlockSpec((1, 1, S, 1), lambda b, h, gb: (b, off + h, 0, 0))

    grid_spec = pltpu.PrefetchScalarGridSpec(
        num_scalar_prefetch=1,
        grid=(B, H),
        in_specs=[col_spec(0), col_spec(H), col_spec(0), col_spec(0),
                  grow(0), grow(H), gcol(0), gcol(H),
                  pl.BlockSpec((CONV_WIDTH, ML_DH), lambda b, h, gb: (0, h)),
                  pl.BlockSpec((CONV_WIDTH, ML_DH), lambda b, h, gb: (0, H + h)),
                  pl.BlockSpec((1, ML_DH), lambda b, h, gb: (0, h)),
                  pl.BlockSpec((1, ML_DH), lambda b, h, gb: (0, H + h)),
                  pl.BlockSpec((1, ML_DH), lambda b, h, gb: (0, h))],
        out_specs=pl.BlockSpec((1, S, ML_DH), lambda b, h, gb: (b, 0, h)),
    )
    return pl.pallas_call(
        _mlstm_kernel,
        out_shape=jax.ShapeDtypeStruct((B, S, ML_W), F32),
        grid_spec=grid_spec,
        compiler_params=_cparams(("parallel", "parallel"), VMEM_LIMIT),
        name="mlstm",
    )(gate_b, ml_qk, ml_qk, ml_v, ml_o, gate_row, gate_row, gate_col, gate_col,
      conv_w, conv_w, conv_b, conv_b, norm_g)


def _pack_cols(cols, dtype):
    tm = cols[0].shape[0]
    lane = _iota((tm, LANES), 1)
    out = jnp.zeros((tm, LANES), dtype)
    for k, c in enumerate(cols):
        out = jnp.where(lane == k, c, out)
    return out


def _outproj_kernel(a_ref, b_ref, mg_ref, x_ref, gt_ref, sc_ref, sh_ref, g2_ref,
                    wa_ref, wb_ref, wo_ref, wr_ref, br_ref,
                    h_ref, n_ref, ti_ref, tg_ref):
    D = x_ref.shape[2]
    ua = jnp.dot(a_ref[0].astype(BF16), wa_ref[...], preferred_element_type=F32)
    ub = jnp.dot(b_ref[0].astype(BF16), wb_ref[...], preferred_element_type=F32)
    u = _sigmoid(mg_ref[0, :, :D]) * ua + _sigmoid(mg_ref[0, :, D:]) * ub
    mix = jnp.dot(u.astype(BF16), wo_ref[...], preferred_element_type=F32)
    h = x_ref[0] + (1.0 + gt_ref[0]) * mix
    h_ref[0] = h
    y = h * lax.rsqrt(jnp.mean(h * h, axis=-1, keepdims=True) + NORM_EPS) * g2_ref[...]
    n = y * (1.0 + sc_ref[0]) + sh_ref[0]
    n_ref[0] = n
    logits = jnp.dot(n, wr_ref[...], precision=HIGHEST, preferred_element_type=F32) + br_ref[...]
    tm, ne = logits.shape
    lane = _iota((tm, ne), 1).astype(F32)
    work = logits
    vals, idxs = [], []
    for _ in range(TOP_K):
        mx = jnp.max(work, axis=-1, keepdims=True)
        ix = jnp.min(jnp.where(work == mx, lane, float(ne)), axis=-1, keepdims=True)
        vals.append(mx)
        idxs.append(ix)
        work = jnp.where(lane == ix, -jnp.inf, work)
    es = [jnp.exp(v - vals[0]) for v in vals]
    tot = es[0] + es[1] + es[2] + es[3]
    ti_ref[0] = _pack_cols(idxs, F32).astype(I32)
    tg_ref[0] = _pack_cols([e / tot for e in es], F32)


def _out_projection(o_nsa, o_ml, merge, x, gt1, sc2, sh2, g2, wa, wb, wo, wr, br, tm=256):
    B, S, D = x.shape
    vec = pl.BlockSpec((1, 1, D), lambda b, i: (b, 0, 0))

    def full(shape):
        return pl.BlockSpec(shape, lambda b, i: (0,) * len(shape))

    def rows(w):
        return pl.BlockSpec((1, tm, w), lambda b, i: (b, i, 0))

    return pl.pallas_call(
        _outproj_kernel,
        out_shape=[jax.ShapeDtypeStruct((B, S, D), F32), jax.ShapeDtypeStruct((B, S, D), F32),
                   jax.ShapeDtypeStruct((B, S, LANES), I32), jax.ShapeDtypeStruct((B, S, LANES), F32)],
        grid=(B, S // tm),
        in_specs=[rows(NSA_QW), rows(ML_W), rows(2 * D), rows(D), vec, vec, vec, full((1, D)),
                  full(wa.shape), full(wb.shape), full(wo.shape), full(wr.shape), full((1, N_EXPERTS))],
        out_specs=[rows(D), rows(D), rows(LANES), rows(LANES)],
        compiler_params=_cparams(("parallel", "parallel"), VMEM_LIMIT),
        name="out_projection_router",
    )(o_nsa, o_ml, merge, x, gt1, sc2, sh2, g2, wa, wb, wo, wr, br)


def _rank_kernel(ti_ref, rank_ref, cnt_ref, carry_ref):
    tt = ti_ref.shape[0]

    @pl.when(pl.program_id(0) == 0)
    def _():
        carry_ref[...] = jnp.zeros_like(carry_ref)

    ti = ti_ref[...]
    lane = _iota((tt, LANES), 1)
    hots = [ti[:, k:k + 1] == lane for k in range(TOP_K)]
    member = jnp.zeros((tt, LANES), F32)
    for hot in hots:
        member = member + jnp.where(hot, 1.0, 0.0)
    strict = jnp.where(_iota((tt, tt), 0) > _iota((tt, tt), 1), 1.0, 0.0).astype(BF16)
    before = jnp.dot(strict, member.astype(BF16), preferred_element_type=F32) + carry_ref[0:1, :]
    ranks = [jnp.sum(jnp.where(hot, before, 0.0), axis=-1, keepdims=True) for hot in hots]
    rank_ref[...] = _pack_cols(ranks, F32)
    carry_ref[...] = carry_ref[...] + jnp.sum(member, axis=0, keepdims=True)
    cnt_ref[...] = carry_ref[...]


def _expert_rank(ti, tt=256):
    T = ti.shape[0]
    return pl.pallas_call(
        _rank_kernel,
        out_shape=[jax.ShapeDtypeStruct((T, LANES), F32), jax.ShapeDtypeStruct((SUBLANES, LANES), F32)],
        grid=(T // tt,),
        in_specs=[pl.BlockSpec((tt, LANES), lambda i: (i, 0))],
        out_specs=[pl.BlockSpec((tt, LANES), lambda i: (i, 0)),
                   pl.BlockSpec((SUBLANES, LANES), lambda i: (0, 0))],
        scratch_shapes=[pltpu.VMEM((SUBLANES, LANES), F32)],
        compiler_params=_cparams(("arbitrary",)),
        name="moe_rank",
    )(ti)


def _dest_kernel(cnt_ref, ti_ref, rank_ref, dest_ref, be_ref):
    tt = ti_ref.shape[0]
    nbp = be_ref.shape[0]
    cnt = cnt_ref[...].astype(I32)
    padded = ((cnt + (MOE_ROWS - 1)) >> 8) << 8
    lane8 = _iota((SUBLANES, LANES), 1)
    pad_end = padded
    sft = 1
    while sft < LANES:
        pad_end = pad_end + jnp.where(lane8 >= sft, pltpu.roll(pad_end, sft, 1), 0)
        sft *= 2
    pad_start = (pad_end - padded)[0:1, :].astype(F32)
    ti = ti_ref[...]
    lane = _iota((tt, LANES), 1)
    rank = rank_ref[...]
    dests = [jnp.sum(jnp.where(ti[:, k:k + 1] == lane, pad_start, 0.0), axis=-1, keepdims=True) + rank[:, k:k + 1]
             for k in range(TOP_K)]
    dest_ref[...] = _pack_cols(dests, F32).astype(I32)
    first_row = _iota((nbp, LANES), 0) * MOE_ROWS
    done = (pad_end[0:1, :] <= first_row) & (_iota((nbp, LANES), 1) < N_EXPERTS)
    be = jnp.minimum(jnp.sum(jnp.where(done, 1.0, 0.0), axis=-1, keepdims=True), float(N_EXPERTS - 1))
    be_ref[...] = jnp.broadcast_to(be, (nbp, LANES)).astype(I32)


def _expert_dest(cnt, ti, rank, n_blk, tt=256):
    T = ti.shape[0]
    return pl.pallas_call(
        _dest_kernel,
        out_shape=[jax.ShapeDtypeStruct((T, LANES), I32), jax.ShapeDtypeStruct((n_blk, LANES), I32)],
        grid=(T // tt,),
        in_specs=[pl.BlockSpec((SUBLANES, LANES), lambda i: (0, 0)),
                  pl.BlockSpec((tt, LANES), lambda i: (i, 0)),
                  pl.BlockSpec((tt, LANES), lambda i: (i, 0))],
        out_specs=[pl.BlockSpec((tt, LANES), lambda i: (i, 0)),
                   pl.BlockSpec((n_blk, LANES), lambda i: (0, 0))],
        compiler_params=_cparams(("arbitrary",)),
        name="moe_dest",
    )(cnt, ti, rank)


def _scatter_kernel(dest_ref, src_ref, init_ref, dst_ref, sem):
    del init_ref
    tt = src_ref.shape[0]

    def copy(t, k):
        return pltpu.make_async_copy(src_ref.at[t], dst_ref.at[dest_ref[t * TOP_K + k]], sem)

    def issue(t, c):
        for k in range(TOP_K):
            copy(t, k).start()
        return c

    lax.fori_loop(0, tt, issue, 0)

    def drain(t, c):
        for k in range(TOP_K):
            copy(t, k).wait()
        return c

    lax.fori_loop(0, tt, drain, 0)


def _scatter_rows(dest_flat, rows, init, tt=512):
    T = rows.shape[0]
    slab = rows.shape[1:]
    return pl.pallas_call(
        _scatter_kernel,
        out_shape=jax.ShapeDtypeStruct(init.shape, init.dtype),
        grid=(T // tt,),
        in_specs=[pl.BlockSpec((tt * TOP_K,), lambda i: (i,), memory_space=pltpu.SMEM),
                  pl.BlockSpec((tt,) + slab, lambda i: (i, 0, 0)),
                  pl.BlockSpec(memory_space=pl.ANY)],
        out_specs=pl.BlockSpec(memory_space=pl.ANY),
        scratch_shapes=[pltpu.SemaphoreType.DMA],
        input_output_aliases={2: 0},
        compiler_params=_cparams(("arbitrary",)),
        name="moe_scatter_rows",
    )(dest_flat, rows, init)


def _expert_kernel(be_ref, x_ref, w1_ref, b1_ref, w2_ref, b2_ref, y_ref):
    del be_ref
    h = jnp.dot(x_ref[...].astype(BF16), w1_ref[0], preferred_element_type=F32) + b1_ref[0]
    glu = jnp.minimum(h[:, :D_EXPERT], SWIGLU_LIMIT)
    lin = jnp.clip(h[:, D_EXPERT:], -SWIGLU_LIMIT, SWIGLU_LIMIT)
    act = glu * _sigmoid(SWIGLU_ALPHA * glu) * (lin + 1.0)
    y_ref[...] = jnp.dot(act.astype(BF16), w2_ref[0], preferred_element_type=F32) + b2_ref[0]


def _expert_ffn(blk_expert, x_rows, w1, b1, w2, b2):
    P, D = x_rows.shape
    nb = P // MOE_ROWS
    grid_spec = pltpu.PrefetchScalarGridSpec(
        num_scalar_prefetch=1,
        grid=(nb,),
        in_specs=[pl.BlockSpec((MOE_ROWS, D), lambda i, be: (i, 0)),
                  pl.BlockSpec((1, D, 2 * D_EXPERT), lambda i, be: (be[i], 0, 0)),
                  pl.BlockSpec((1, 1, 2 * D_EXPERT), lambda i, be: (be[i], 0, 0)),
                  pl.BlockSpec((1, D_EXPERT, D), lambda i, be: (be[i], 0, 0)),
                  pl.BlockSpec((1, 1, D), lambda i, be: (be[i], 0, 0))],
        out_specs=pl.BlockSpec((MOE_ROWS, D), lambda i, be: (i, 0)),
    )
    return pl.pallas_call(
        _expert_kernel,
        out_shape=jax.ShapeDtypeStruct((P, D), F32),
        grid_spec=grid_spec,
        compiler_params=_cparams(("parallel",), VMEM_LIMIT),
        name="moe_expert_ffn",
    )(blk_expert, x_rows, w1, b1, w2, b2)


def _combine_kernel(dest_ref, gate_ref, y_ref, o_ref, buf_ref, sem):
    n = dest_ref.shape[0]
    tt = n // TOP_K

    def copy(t, k):
        return pltpu.make_async_copy(y_ref.at[dest_ref[t * TOP_K + k]], buf_ref.at[k, t], sem)

    def issue(t, c):
        for k in range(TOP_K):
            copy(t, k).start()
        return c

    lax.fori_loop(0, tt, issue, 0)

    def drain(t, c):
        for k in range(TOP_K):
            copy(t, k).wait()
        return c

    lax.fori_loop(0, tt, drain, 0)

    def mix(t, c):
        acc = gate_ref[t * TOP_K] * buf_ref[0, t]
        for k in range(1, TOP_K):
            acc = acc + gate_ref[t * TOP_K + k] * buf_ref[k, t]
        o_ref[t] = acc
        return c

    lax.fori_loop(0, tt, mix, 0)


def _combine_rows(dest_flat, gate_flat, y_rows, T, tt=256):
    slab = y_rows.shape[1:]
    return pl.pallas_call(
        _combine_kernel,
        out_shape=jax.ShapeDtypeStruct((T,) + slab, F32),
        grid=(T // tt,),
        in_specs=[pl.BlockSpec((tt * TOP_K,), lambda i: (i,), memory_space=pltpu.SMEM),
                  pl.BlockSpec((tt * TOP_K,), lambda i: (i,), memory_space=pltpu.SMEM),
                  pl.BlockSpec(memory_space=pl.ANY)],
        out_specs=pl.BlockSpec((tt,) + slab, lambda i: (i, 0, 0)),
        scratch_shapes=[pltpu.VMEM((TOP_K, tt) + slab, F32), pltpu.SemaphoreType.DMA],
        compiler_params=_cparams(("arbitrary",)),
        name="moe_combine_rows",
    )(dest_flat, gate_flat, y_rows)


def _final_kernel(h_ref, y_ref, gt_ref, g_ref, o_ref, *, normalize):
    h = h_ref[0] + (1.0 + gt_ref[0]) * y_ref[0]
    if normalize:
        h = h * lax.rsqrt(jnp.mean(h * h, axis=-1, keepdims=True) + NORM_EPS) * g_ref[...]
    o_ref[0] = h


def _residual(h1, y, gt2, g, normalize, tm=512):
    B, S, D = h1.shape
    rows = pl.BlockSpec((1, tm, D), lambda b, i: (b, i, 0))
    return pl.pallas_call(
        functools.partial(_final_kernel, normalize=normalize),
        out_shape=jax.ShapeDtypeStruct((B, S, D), F32),
        grid=(B, S // tm),
        in_specs=[rows, rows, pl.BlockSpec((1, 1, D), lambda b, i: (b, 0, 0)),
                  pl.BlockSpec((1, D), lambda b, i: (0, 0))],
        out_specs=rows,
        compiler_params=_cparams(("parallel", "parallel")),
        name="final_norm",
    )(h1, y, gt2, g)


def _overlap_matrix(S):
    nc = S // CMP_STRIDE - 1
    ns = S // SEL_BLOCK
    c0 = np.arange(nc) * CMP_STRIDE
    s0 = np.arange(ns) * SEL_BLOCK
    ov = np.clip(np.minimum(c0[:, None] + CMP_LEN, s0[None, :] + SEL_BLOCK)
                 - np.maximum(c0[:, None], s0[None, :]), 0, None) / CMP_LEN
    return np.concatenate([ov, np.zeros((1, ns))], axis=0).astype(np.float32)


def _layer(h, mods, norm1_g, w_in, pe_k, pe_v, cw1_k, cw2_k, cw1_v, cw2_v, conv_w, conv_b, gate_b, ml_norm_g,
           w_up_nsa, w_up_ml, w_out, norm2_g, router_w, router_b, exp_w1, exp_b1, exp_w2, exp_b2):
    B, S, D = h.shape
    T = B * S
    sh1, sc1, gt1, sh2, sc2, gt2 = mods

    w_r = jnp.concatenate([w_in[:, 0:1280], w_in[:, 1304:2328], w_in[:, 2328:2840], w_in[:, 2848:3360],
                           w_in[:, 3360:5408], w_in[:, 1280:1304], w_in[:, 2840:2848],
                           jnp.zeros((D, IN_WR - 5408), w_in.dtype)], axis=1).astype(BF16)
    q, kvc, kvs, ml_qk, ml_v, ml_o, merge, small = _in_projection(h, norm1_g.reshape(1, D), sc1, sh1, w_r)

    nb16 = S // CMP_STRIDE
    t16 = kvc.reshape(B, S, 4, NSA_DH).transpose(0, 2, 1, 3).reshape(B, 4, nb16, CMP_STRIDE * NSA_DH)
    pe = jnp.stack([pe_k.reshape(2, CMP_STRIDE * NSA_DH), pe_v.reshape(2, CMP_STRIDE * NSA_DH)])
    cmp = _compress(t16, pe, jnp.stack([cw1_k, cw1_v]), jnp.stack([cw2_k, cw2_v]))
    n_sel = min(SEL_TOPN, S // SEL_BLOCK)
    kc = cmp[:, :NSA_GROUPS].transpose(0, 2, 1, 3).reshape(B, nb16, NSA_KVW).astype(BF16)
    vct = cmp[:, NSA_GROUPS:].transpose(0, 1, 3, 2).reshape(B, NSA_KVW, nb16).astype(BF16)

    def key_major_tiles(v):
        return v.reshape(B, S // LANES, LANES, LANES).swapaxes(-1, -2)

    o_nsa = _nsa_attention(q, kc, vct, jnp.asarray(_overlap_matrix(S).T), kvs,
                           key_major_tiles(kvs[:, :, LANES:2 * LANES]), key_major_tiles(kvs[:, :, 3 * LANES:]),
                           small, n_sel)

    gates = small[:, :, 3 * NSA_HEADS:3 * NSA_HEADS + 2 * ML_HEADS].transpose(0, 2, 1)
    gate_row = gates.reshape(B, 2 * ML_HEADS, S // ML_CHUNK, ML_CHUNK)
    gate_col = gates.reshape(B, 2 * ML_HEADS, S, 1)
    o_ml = _mlstm(ml_qk, ml_v, ml_o, gate_row, gate_col, gate_b, conv_w, conv_b.reshape(1, -1),
                  ml_norm_g.reshape(1, -1))

    h1, n2, ti, tg = _out_projection(o_nsa, o_ml, merge, h, gt1, sc2, sh2, norm2_g.reshape(1, D),
                                     w_up_nsa.astype(BF16), w_up_ml.astype(BF16), w_out.astype(BF16),
                                     router_w, router_b.reshape(1, N_EXPERTS))

    A = T * TOP_K
    n_blk = -(-(A + N_EXPERTS * (MOE_ROWS - 1)) // MOE_ROWS)
    P = n_blk * MOE_ROWS
    ti2 = ti.reshape(T, LANES)
    rank, cnt = _expert_rank(ti2)
    dest, blk_e = _expert_dest(cnt, ti2, rank, n_blk)
    dest_flat = dest[:, :TOP_K].reshape(A)
    gate_flat = tg.reshape(T, LANES)[:, :TOP_K].reshape(A)
    slab = (SUBLANES, D // SUBLANES)
    x_rows = _scatter_rows(dest_flat, n2.reshape((T,) + slab), jnp.zeros((P,) + slab, F32))
    y_rows = _expert_ffn(blk_e[:, 0], x_rows.reshape(P, D), exp_w1.astype(BF16), exp_b1.reshape(N_EXPERTS, 1, -1),
                         exp_w2.astype(BF16), exp_b2.reshape(N_EXPERTS, 1, -1))
    y = _combine_rows(dest_flat, gate_flat, y_rows.reshape((P,) + slab), T)
    return h1, y.reshape(B, S, D), gt2


def kernel(x, c, ada_w, ada_b, norm1_g, w_in, nsa_pe_k, nsa_pe_v, nsa_cmp_w1_k, nsa_cmp_w2_k, nsa_cmp_w1_v, nsa_cmp_w2_v, ml_conv_w, ml_conv_b, ml_gate_b, ml_norm_g, w_up_nsa, w_up_ml, w_out, norm2_g, router_w, router_b, exp_w1, exp_b1, exp_w2, exp_b2, final_g):
    B, S, D = x.shape
    depth = ada_w.shape[0]
    h = x
    for layer in range(depth):
        mod = _modulation(c, ada_w[layer], ada_b[layer])
        mods = [mod[:, i * D:(i + 1) * D].reshape(B, 1, D) for i in range(6)]
        h1, y, gt2 = _layer(h, mods, norm1_g[layer], w_in[layer], nsa_pe_k[layer], nsa_pe_v[layer],
                            nsa_cmp_w1_k[layer], nsa_cmp_w2_k[layer], nsa_cmp_w1_v[layer], nsa_cmp_w2_v[layer],
                            ml_conv_w[layer], ml_conv_b[layer], ml_gate_b[layer], ml_norm_g[layer],
                            w_up_nsa[layer], w_up_ml[layer], w_out[layer], norm2_g[layer], router_w[layer],
                            router_b[layer], exp_w1[layer], exp_b1[layer], exp_w2[layer], exp_b2[layer])
        h = _residual(h1, y, gt2, final_g.reshape(1, D), normalize=(layer + 1 == depth))
    return h
```

```python
import functools

import numpy as np
import jax
import jax.numpy as jnp
from jax import lax
from jax.experimental import pallas as pl
from jax.experimental.pallas import tpu as pltpu

F32 = jnp.float32
BF16 = jnp.bfloat16
I32 = jnp.int32
HIGHEST = lax.Precision.HIGHEST

D_MODEL = 1024
NSA_HEADS = 8
NSA_GROUPS = 2
NSA_REP = NSA_HEADS // NSA_GROUPS
NSA_DH = 64
NSA_SCALE = NSA_DH ** -0.5
CMP_STRIDE = 16
CMP_LEN = 32
CMP_HIDDEN = 128
SEL_BLOCK = 64
SEL_TOPN = 16
WINDOW = 512
Q_BLOCK = 128
ML_HEADS = 4
ML_DH = 128
ML_CHUNK = 64
CONV_WIDTH = 4
N_EXPERTS = 32
TOP_K = 4
D_EXPERT = 1024
SWIGLU_LIMIT = 7.0
SWIGLU_ALPHA = 1.702
MOE_ROWS = 256
NORM_EPS = 1e-6
NEG_INF = -1e30
FORCE_BONUS = 1e4

NSA_QW = NSA_HEADS * NSA_DH
NSA_KVW = NSA_GROUPS * NSA_DH
ML_W = ML_HEADS * ML_DH
LANES = 128
SUBLANES = 8
VMEM_LIMIT = 52 * 1024 * 1024

_C_Q = (0, 512)
_C_KVC = (512, 768)
_C_KVS = (768, 1280)
_C_MQK = (1280, 2304)
_C_MV = (2304, 2816)
_C_MO = (2816, 3328)
_C_MG = (3328, 5376)
_C_SM = (5376, 5504)
IN_WR = 5504


def _sigmoid(x):
    return 1.0 / (1.0 + jnp.exp(-x))


def _iota(shape, dim):
    return lax.broadcasted_iota(I32, shape, dim)


def _cparams(sem, vmem=None):
    return pltpu.CompilerParams(dimension_semantics=sem, vmem_limit_bytes=vmem)


def _mod_kernel(c_ref, w_ref, b_ref, o_ref):
    c = c_ref[...]
    ca = c * _sigmoid(c)
    o_ref[...] = jnp.dot(ca, w_ref[...], precision=HIGHEST, preferred_element_type=F32) + b_ref[...]


def _modulation(c, ada_w, ada_b):
    B, D = c.shape
    n = ada_w.shape[1] // D
    return pl.pallas_call(
        _mod_kernel,
        out_shape=jax.ShapeDtypeStruct((B, n * D), F32),
        grid=(n,),
        in_specs=[pl.BlockSpec((B, D), lambda j: (0, 0)),
                  pl.BlockSpec((D, D), lambda j: (0, j)),
                  pl.BlockSpec((1, D), lambda j: (0, j))],
        out_specs=pl.BlockSpec((B, D), lambda j: (0, j)),
        compiler_params=_cparams(("parallel",)),
        name="modulation",
    )(c, ada_w, ada_b.reshape(1, n * D))


def _inproj_kernel(x_ref, g_ref, sc_ref, sh_ref, w_ref,
                   q_ref, kvc_ref, kvs_ref, mqk_ref, mv_ref, mo_ref, mg_ref, sm_ref):
    x = x_ref[0]
    y = x * lax.rsqrt(jnp.mean(x * x, axis=-1, keepdims=True) + NORM_EPS) * g_ref[...]
    n = (y * (1.0 + sc_ref[0]) + sh_ref[0]).astype(BF16)

    def proj(c):
        return jnp.dot(n, w_ref[:, c[0]:c[1]], preferred_element_type=F32)

    q_ref[0] = (proj(_C_Q) * NSA_SCALE).astype(BF16)
    kvc_ref[0] = proj(_C_KVC)
    kvs_ref[0] = proj(_C_KVS).astype(BF16)
    mqk_ref[0] = proj(_C_MQK)
    mv_ref[0] = proj(_C_MV)
    mo_ref[0] = proj(_C_MO)
    mg_ref[0] = proj(_C_MG)
    sm_ref[0] = proj(_C_SM)


def _in_projection(x, g, sc, sh, w_r, tm=256):
    B, S, D = x.shape
    widths = [c[1] - c[0] for c in (_C_Q, _C_KVC, _C_KVS, _C_MQK, _C_MV, _C_MO, _C_MG, _C_SM)]
    dtypes = [BF16, F32, BF16, F32, F32, F32, F32, F32]
    vec = pl.BlockSpec((1, 1, D), lambda b, i: (b, 0, 0))
    return pl.pallas_call(
        _inproj_kernel,
        out_shape=[jax.ShapeDtypeStruct((B, S, w), dt) for w, dt in zip(widths, dtypes)],
        grid=(B, S // tm),
        in_specs=[pl.BlockSpec((1, tm, D), lambda b, i: (b, i, 0)),
                  pl.BlockSpec((1, D), lambda b, i: (0, 0)),
                  vec, vec,
                  pl.BlockSpec((D, IN_WR), lambda b, i: (0, 0))],
        out_specs=[pl.BlockSpec((1, tm, w), lambda b, i: (b, i, 0)) for w in widths],
        compiler_params=_cparams(("parallel", "parallel"), VMEM_LIMIT),
        name="in_projection",
    )(x, g, sc, sh, w_r)


def _compress_kernel(t_ref, pe_ref, w1_ref, w2_ref, o_ref):
    half = CMP_STRIDE * NSA_DH
    t = t_ref[0, 0]
    pe = pe_ref[0]
    nb = t.shape[0]
    a = jnp.dot((t + pe[0:1]).astype(BF16), w1_ref[0, :half, :].astype(BF16), preferred_element_type=F32)
    b = jnp.dot((t + pe[1:2]).astype(BF16), w1_ref[0, half:, :].astype(BF16), preferred_element_type=F32)
    h = a + pltpu.roll(b, nb - 1, 0)
    gl = 0.5 * h * (1.0 + jnp.tanh(np.sqrt(2.0 / np.pi).astype(np.float32) * (h + 0.044715 * (h * h * h))))
    o_ref[0, 0] = jnp.dot(gl.astype(BF16), w2_ref[0].astype(BF16), preferred_element_type=F32)


def _compress(t16, pe, w1, w2):
    B, _, nb, wdt = t16.shape
    return pl.pallas_call(
        _compress_kernel,
        out_shape=jax.ShapeDtypeStruct((B, 4, nb, NSA_DH), F32),
        grid=(B, 4),
        in_specs=[pl.BlockSpec((1, 1, nb, wdt), lambda b, j: (b, j, 0, 0)),
                  pl.BlockSpec((1, 2, wdt), lambda b, j: (j // 2, 0, 0)),
                  pl.BlockSpec((1, CMP_LEN * NSA_DH, CMP_HIDDEN), lambda b, j: (j // 2, 0, 0)),
                  pl.BlockSpec((1, CMP_HIDDEN, NSA_DH), lambda b, j: (j // 2, 0, 0))],
        out_specs=pl.BlockSpec((1, 1, nb, NSA_DH), lambda b, j: (b, j, 0, 0)),
        compiler_params=_cparams(("parallel", "parallel")),
        name="nsa_compress",
    )(t16, pe, w1, w2)


def _nsa_kernel(q_ref, kc_ref, vct_ref, ovt_ref, ks_ref, vst_ref, kw_ref, vwt_ref, sm_ref, o_ref, *, n_sel, tk):
    qb = Q_BLOCK
    ncp = kc_ref.shape[1]
    ns = ovt_ref.shape[0]
    q0 = pl.program_id(1) * qb
    t_l = q0 + _iota((1, qb), 1)
    sub = _iota((LANES, qb), 0)
    gT = _sigmoid(sm_ref[0].T)

    qT = []
    for c in range(NSA_HEADS // 2):
        qc = q_ref[0, :, c * LANES:(c + 1) * LANES].astype(F32).T
        for e in range(2):
            g = (2 * c + e) // NSA_REP
            x = qc if e == g else pltpu.roll(qc, NSA_DH, 0)
            qT.append(jnp.where((sub >> 6) == g, x, 0.0).astype(BF16))

    kc = kc_ref[0]
    vct = vct_ref[0]
    cmask = (_iota((ncp, qb), 0) * CMP_STRIDE + (CMP_LEN - 1)) <= t_l
    blk = _iota((ns, qb), 0)
    cur = t_l >> 6
    forced = (blk == 0) | (blk == cur) | (blk == cur - 1)
    ocT = []
    selT = []
    for g in range(NSA_GROUPS):
        psum = jnp.zeros((ncp, qb), F32)
        for r in range(NSA_REP):
            s = jnp.dot(kc, qT[g * NSA_REP + r], preferred_element_type=F32)
            s = jnp.where(cmask, s, NEG_INF)
            e = jnp.exp(s - jnp.max(s, axis=0, keepdims=True))
            p = e * (1.0 / jnp.sum(e, axis=0, keepdims=True))
            p = jnp.where(cmask, p, 0.0)
            ocT.append(jnp.dot(vct, p.astype(BF16), preferred_element_type=F32))
            psum = psum + p
        imp = jnp.dot(ovt_ref[...], psum, precision=HIGHEST, preferred_element_type=F32)
        imp = jnp.where(blk > cur, NEG_INF, imp + jnp.where(forced, FORCE_BONUS, 0.0))
        rank = jnp.zeros((ns, qb), F32)
        for i in range(ns):
            row = imp[i:i + 1, :]
            beats = (row > imp) | ((row == imp) & (blk > i))
            rank = rank + jnp.where(beats, 1.0, 0.0)
        selT.append(jnp.where(rank < n_sel, 1.0, 0.0).astype(BF16))

    wlen = WINDOW + qb
    wt0 = jnp.maximum(q0 - WINDOW, 0) // LANES
    wstart = pl.multiple_of(wt0 * LANES, LANES)
    kwin = kw_ref[0, pl.ds(wstart, wlen), :]
    rel = t_l - (wstart + _iota((wlen, qb), 0))
    wmask = (rel >= 0) & (rel < WINDOW)
    partial = []
    for h in range(NSA_HEADS):
        lo = (h // NSA_REP) * NSA_DH
        s = jnp.dot(kwin, qT[h], preferred_element_type=F32)
        s = jnp.where(wmask, s, NEG_INF)
        p = jnp.exp(s - jnp.max(s, axis=0, keepdims=True))
        pb = p.astype(BF16)
        ow = jnp.dot(vwt_ref[0, wt0], pb[:LANES], preferred_element_type=F32)
        for i in range(1, wlen // LANES):
            ow = ow + jnp.dot(vwt_ref[0, wt0 + i], pb[i * LANES:(i + 1) * LANES], preferred_element_type=F32)
        ow = ow * (1.0 / jnp.sum(p, axis=0, keepdims=True))
        partial.append(gT[h:h + 1] * ocT[h][lo:lo + NSA_DH]
                       + gT[2 * NSA_HEADS + h:2 * NSA_HEADS + h + 1] * ow[lo:lo + NSA_DH])

    nt = (q0 + qb + tk - 1) // tk
    vt_per = tk // LANES

    def step(j, carry):
        r0 = pl.multiple_of(j * tk, tk)
        kt = ks_ref[0, pl.ds(r0, tk), :]
        causal = (r0 + _iota((tk, qb), 0)) <= t_l
        expand = jnp.where(((r0 + _iota((tk, ns), 0)) >> 6) == _iota((tk, ns), 1), 1.0, 0.0).astype(BF16)
        new = []
        for g in range(NSA_GROUPS):
            ok = (jnp.dot(expand, selT[g], preferred_element_type=F32) > 0.5) & causal
            for r in range(NSA_REP):
                m, l, acc = carry[g * NSA_REP + r]
                s = jnp.dot(kt, qT[g * NSA_REP + r], preferred_element_type=F32)
                s = jnp.where(ok, s, NEG_INF)
                m_new = jnp.maximum(m, jnp.max(s, axis=0, keepdims=True))
                alpha = jnp.exp(m - m_new)
                p = jnp.exp(s - m_new)
                pb = p.astype(BF16)
                pv = jnp.dot(vst_ref[0, j * vt_per], pb[:LANES], preferred_element_type=F32)
                for i in range(1, vt_per):
                    pv = pv + jnp.dot(vst_ref[0, j * vt_per + i], pb[i * LANES:(i + 1) * LANES],
                                      preferred_element_type=F32)
                new.append((m_new, alpha * l + jnp.sum(p, axis=0, keepdims=True), alpha * acc + pv))
        return tuple(new)

    init = tuple((jnp.full((1, qb), NEG_INF, F32), jnp.zeros((1, qb), F32), jnp.zeros((LANES, qb), F32))
                 for _ in range(NSA_HEADS))
    final = lax.fori_loop(0, nt, step, init)

    for c in range(NSA_HEADS // 2):
        parts = []
        for e in range(2):
            h = 2 * c + e
            lo = (h // NSA_REP) * NSA_DH
            _, l, acc = final[h]
            o_sel = acc[lo:lo + NSA_DH] * (1.0 / l)
            parts.append(partial[h] + gT[NSA_HEADS + h:NSA_HEADS + h + 1] * o_sel)
        o_ref[0, :, c * LANES:(c + 1) * LANES] = jnp.concatenate(parts, axis=0).T


def _nsa_attention(q, kc, vct, ovt, kvs, vst, vwt, small, n_sel, tk=256):
    B, S, _ = q.shape
    ns, ncp = ovt.shape
    tk = min(tk, S)
    assert S >= WINDOW + Q_BLOCK and S % tk == 0 and tk % LANES == 0
    nt128 = S // LANES

    def whole(shape):
        return pl.BlockSpec((1,) + shape, lambda b, i: (b,) + (0,) * len(shape))

    return pl.pallas_call(
        functools.partial(_nsa_kernel, n_sel=n_sel, tk=tk),
        out_shape=jax.ShapeDtypeStruct((B, S, NSA_QW), F32),
        grid=(B, S // Q_BLOCK),
        in_specs=[pl.BlockSpec((1, Q_BLOCK, NSA_QW), lambda b, i: (b, i, 0)),
                  whole((ncp, LANES)), whole((LANES, ncp)),
                  pl.BlockSpec((ns, ncp), lambda b, i: (0, 0)),
                  pl.BlockSpec((1, S, LANES), lambda b, i: (b, 0, 0)),
                  whole((nt128, LANES, LANES)),
                  pl.BlockSpec((1, S, LANES), lambda b, i: (b, 0, 2)),
                  whole((nt128, LANES, LANES)),
                  pl.BlockSpec((1, Q_BLOCK, LANES), lambda b, i: (b, i, 0))],
        out_specs=pl.BlockSpec((1, Q_BLOCK, NSA_QW), lambda b, i: (b, i, 0)),
        compiler_params=_cparams(("parallel", "parallel"), VMEM_LIMIT),
        name="nsa_attention",
    )(q, kc, vct, ovt, kvs, vst, kvs, vwt, small)


def _mlstm_kernel(gb_ref, q_ref, k_ref, v_ref, o_ref, grow_ref, gcol_ref, cw_ref, cb_ref, ng_ref, out_ref,
                  c_scr, n_scr, m_scr, hq_scr, hk_scr):
    L = ML_CHUNK
    H = ML_HEADS
    ts = q_ref.shape[1]
    kscale = ML_DH ** -0.5
    row = _iota((L, L), 0)
    col = _iota((L, L), 1)
    tri = row >= col
    halo_rows = SUBLANES

    @pl.when(pl.program_id(1) == 0)
    def _():
        c_scr[...] = jnp.zeros_like(c_scr)
        n_scr[...] = jnp.zeros_like(n_scr)
        m_scr[...] = jnp.zeros_like(m_scr)
        hq_scr[...] = jnp.zeros_like(hq_scr)
        hk_scr[...] = jnp.zeros_like(hk_scr)

    def conv_silu(x_ref, prev_ref, woff, h, c, r0):
        lanes = slice(h * ML_DH, (h + 1) * ML_DH)
        wl = slice(woff + h * ML_DH, woff + (h + 1) * ML_DH)
        main = x_ref[0, pl.ds(r0, L), lanes]
        h0 = pl.multiple_of(jnp.maximum(r0 - halo_rows, 0), halo_rows)
        halo = jnp.where(c > 0, x_ref[0, pl.ds(h0, halo_rows), lanes], prev_ref[:, lanes])
        cat = jnp.concatenate([halo, main], axis=0)
        y = main * cw_ref[CONV_WIDTH - 1:CONV_WIDTH, wl] + cb_ref[:, wl]
        for w in range(CONV_WIDTH - 1):
            sft = CONV_WIDTH - 1 - w
            y = y + pltpu.roll(cat, sft, 0)[halo_rows:, :] * cw_ref[w:w + 1, wl]
        return y * _sigmoid(y)

    def logsig(x):
        return -(jnp.maximum(-x, 0.0) + jnp.log(1.0 + jnp.exp(-jnp.abs(x))))

    def head_step(h, c, r0, C, n, m):
        lanes = slice(h * ML_DH, (h + 1) * ML_DH)
        gb_i = gb_ref[h]
        gb_f = gb_ref[H + h]
        qc = conv_silu(q_ref, hq_scr, 0, h, c, r0)
        kc = conv_silu(k_ref, hk_scr, ML_W, h, c, r0) * kscale
        vc = v_ref[0, pl.ds(r0, L), lanes]
        li_row = grow_ref[0, h, pl.ds(c, 1), :] + gb_i
        lf_row = logsig(grow_ref[0, H + h, pl.ds(c, 1), :] + gb_f)
        li_col = gcol_ref[0, h, pl.ds(r0, L), :] + gb_i
        lf_col = logsig(gcol_ref[0, H + h, pl.ds(r0, L), :] + gb_f)
        b_col = jnp.sum(jnp.where(tri, lf_row, 0.0), axis=1, keepdims=True)
        b_row = jnp.sum(jnp.where(row <= col, lf_col, 0.0), axis=0, keepdims=True)
        Dm = jnp.where(tri, b_col - b_row + li_row, NEG_INF)
        inter = b_col + m
        m_t = jnp.maximum(inter, jnp.max(Dm, axis=1, keepdims=True))
        Dw = jnp.exp(Dm - m_t)
        inter_w = jnp.exp(inter - m_t)
        qb16 = qc.astype(BF16)
        kb16 = kc.astype(BF16)
        vb16 = vc.astype(BF16)
        qk = lax.dot_general(qb16, kb16, (((1,), (1,)), ((), ())), preferred_element_type=F32) * Dw
        num = (jnp.dot(qk.astype(BF16), vb16, preferred_element_type=F32)
               + inter_w * jnp.dot(qb16, C.astype(BF16), preferred_element_type=F32))
        den = jnp.sum(qk, axis=1, keepdims=True) + inter_w * jnp.sum(qc * n, axis=1, keepdims=True)
        hc = num / jnp.maximum(jnp.abs(den), jnp.exp(-m_t))
        bL = b_col[L - 1:L, :]
        a_col = bL - b_col + li_col
        m_new = jnp.maximum(bL + m, jnp.max(a_col, axis=0, keepdims=True))
        aw = jnp.exp(a_col - m_new)
        decay = jnp.exp(bL + m - m_new)
        awk = aw * kc
        C_new = decay * C + jnp.dot(awk.T.astype(BF16), vb16, preferred_element_type=F32)
        n_new = decay * n + jnp.sum(awk, axis=0, keepdims=True)
        hn = hc * lax.rsqrt(jnp.mean(hc * hc, axis=1, keepdims=True) + NORM_EPS) * ng_ref[:, lanes]
        out_ref[0, pl.ds(r0, L), lanes] = hn * _sigmoid(o_ref[0, pl.ds(r0, L), lanes])
        return C_new, n_new, m_new

    def step(c, carry):
        r0 = pl.multiple_of(c * L, L)
        return tuple(head_step(h, c, r0, *carry[h]) for h in range(H))

    init = tuple((c_scr[h], n_scr[h], m_scr[h][:, 0:1]) for h in range(H))
    final = lax.fori_loop(0, ts // L, step, init)
    for h in range(H):
        C, n, m = final[h]
        c_scr[h] = C
        n_scr[h] = n
        m_scr[h] = jnp.broadcast_to(m, (1, LANES))
    hq_scr[...] = q_ref[0, ts - halo_rows:ts, :]
    hk_scr[...] = k_ref[0, ts - halo_rows:ts, :]


def _mlstm(ml_qk, ml_v, ml_o, gate_row, gate_col, gate_b, conv_w, conv_b, norm_g, ts=1024):
    B, S, _ = ml_v.shape
    H = ML_HEADS
    ts = min(ts, S)

    def rows(cb):
        return pl.BlockSpec((1, ts, ML_W), lambda b, i, gb: (b, i, cb))

    def full(shape):
        return pl.BlockSpec(shape, lambda b, i, gb: (0,) * len(shape))

    grid_spec = pltpu.PrefetchScalarGridSpec(
        num_scalar_prefetch=1,
        grid=(B, S // ts),
        in_specs=[rows(0), rows(1), rows(0), rows(0),
                  pl.BlockSpec((1, 2 * H, ts // ML_CHUNK, ML_CHUNK), lambda b, i, gb: (b, 0, i, 0)),
                  pl.BlockSpec((1, 2 * H, ts, 1), lambda b, i, gb: (b, 0, i, 0)),
                  full((CONV_WIDTH, 2 * ML_W)), full((1, 2 * ML_W)), full((1, ML_W))],
        out_specs=rows(0),
        scratch_shapes=[pltpu.VMEM((H, ML_DH, ML_DH), F32), pltpu.VMEM((H, 1, ML_DH), F32),
                        pltpu.VMEM((H, 1, LANES), F32),
                        pltpu.VMEM((SUBLANES, ML_W), F32), pltpu.VMEM((SUBLANES, ML_W), F32)],
    )
    return pl.pallas_call(
        _mlstm_kernel,
        out_shape=jax.ShapeDtypeStruct((B, S, ML_W), F32),
        grid_spec=grid_spec,
        compiler_params=_cparams(("parallel", "arbitrary"), VMEM_LIMIT),
        name="mlstm",
    )(gate_b, ml_qk, ml_qk, ml_v, ml_o, gate_row, gate_col, conv_w, conv_b, norm_g)


def _pack_cols(cols, dtype):
    tm = cols[0].shape[0]
    lane = _iota((tm, LANES), 1)
    out = jnp.zeros((tm, LANES), dtype)
    for k, c in enumerate(cols):
        out = jnp.where(lane == k, c, out)
    return out


def _outproj_kernel(a_ref, b_ref, mg_ref, x_ref, gt_ref, sc_ref, sh_ref, g2_ref,
                    wa_ref, wb_ref, wo_ref, wr_ref, br_ref,
                    h_ref, n_ref, ti_ref, tg_ref):
    D = x_ref.shape[2]
    ua = jnp.dot(a_ref[0].astype(BF16), wa_ref[...], preferred_element_type=F32)
    ub = jnp.dot(b_ref[0].astype(BF16), wb_ref[...], preferred_element_type=F32)
    u = _sigmoid(mg_ref[0, :, :D]) * ua + _sigmoid(mg_ref[0, :, D:]) * ub
    mix = jnp.dot(u.astype(BF16), wo_ref[...], preferred_element_type=F32)
    h = x_ref[0] + (1.0 + gt_ref[0]) * mix
    h_ref[0] = h
    y = h * lax.rsqrt(jnp.mean(h * h, axis=-1, keepdims=True) + NORM_EPS) * g2_ref[...]
    n = y * (1.0 + sc_ref[0]) + sh_ref[0]
    for s in range(SUBLANES):
        n_ref[0, :, s, :] = n[:, s * LANES:(s + 1) * LANES]
    logits = jnp.dot(n, wr_ref[...], precision=HIGHEST, preferred_element_type=F32) + br_ref[...]
    tm, ne = logits.shape
    lane = _iota((tm, ne), 1).astype(F32)
    work = logits
    vals, idxs = [], []
    for _ in range(TOP_K):
        mx = jnp.max(work, axis=-1, keepdims=True)
        ix = jnp.min(jnp.where(work == mx, lane, float(ne)), axis=-1, keepdims=True)
        vals.append(mx)
        idxs.append(ix)
        work = jnp.where(lane == ix, -jnp.inf, work)
    es = [jnp.exp(v - vals[0]) for v in vals]
    tot = es[0] + es[1] + es[2] + es[3]
    ti_ref[0] = _pack_cols(idxs, F32).astype(I32)
    tg_ref[0] = _pack_cols([e / tot for e in es], F32)


def _out_projection(o_nsa, o_ml, merge, x, gt1, sc2, sh2, g2, wa, wb, wo, wr, br, tm=256):
    B, S, D = x.shape
    vec = pl.BlockSpec((1, 1, D), lambda b, i: (b, 0, 0))

    def full(shape):
        return pl.BlockSpec(shape, lambda b, i: (0,) * len(shape))

    def rows(w):
        return pl.BlockSpec((1, tm, w), lambda b, i: (b, i, 0))

    slab = (SUBLANES, D // SUBLANES)
    return pl.pallas_call(
        _outproj_kernel,
        out_shape=[jax.ShapeDtypeStruct((B, S, D), F32), jax.ShapeDtypeStruct((B, S) + slab, F32),
                   jax.ShapeDtypeStruct((B, S, LANES), I32), jax.ShapeDtypeStruct((B, S, LANES), F32)],
        grid=(B, S // tm),
        in_specs=[rows(NSA_QW), rows(ML_W), rows(2 * D), rows(D), vec, vec, vec, full((1, D)),
                  full(wa.shape), full(wb.shape), full(wo.shape), full(wr.shape), full((1, N_EXPERTS))],
        out_specs=[rows(D), pl.BlockSpec((1, tm) + slab, lambda b, i: (b, i, 0, 0)), rows(LANES), rows(LANES)],
        compiler_params=_cparams(("parallel", "parallel"), VMEM_LIMIT),
        name="out_projection_router",
    )(o_nsa, o_ml, merge, x, gt1, sc2, sh2, g2, wa, wb, wo, wr, br)


def _rank_kernel(ti_ref, rank_ref, cnt_ref, carry_ref):
    tt = ti_ref.shape[0]

    @pl.when(pl.program_id(0) == 0)
    def _():
        carry_ref[...] = jnp.zeros_like(carry_ref)

    ti = ti_ref[...]
    lane = _iota((tt, LANES), 1)
    hots = [ti[:, k:k + 1] == lane for k in range(TOP_K)]
    member = jnp.zeros((tt, LANES), F32)
    for hot in hots:
        member = member + jnp.where(hot, 1.0, 0.0)
    strict = jnp.where(_iota((tt, tt), 0) > _iota((tt, tt), 1), 1.0, 0.0).astype(BF16)
    before = jnp.dot(strict, member.astype(BF16), preferred_element_type=F32) + carry_ref[0:1, :]
    ranks = [jnp.sum(jnp.where(hot, before, 0.0), axis=-1, keepdims=True) for hot in hots]
    rank_ref[...] = _pack_cols(ranks, F32)
    carry_ref[...] = carry_ref[...] + jnp.sum(member, axis=0, keepdims=True)
    cnt_ref[...] = carry_ref[...]


def _expert_rank(ti, tt=256):
    T = ti.shape[0]
    return pl.pallas_call(
        _rank_kernel,
        out_shape=[jax.ShapeDtypeStruct((T, LANES), F32), jax.ShapeDtypeStruct((SUBLANES, LANES), F32)],
        grid=(T // tt,),
        in_specs=[pl.BlockSpec((tt, LANES), lambda i: (i, 0))],
        out_specs=[pl.BlockSpec((tt, LANES), lambda i: (i, 0)),
                   pl.BlockSpec((SUBLANES, LANES), lambda i: (0, 0))],
        scratch_shapes=[pltpu.VMEM((SUBLANES, LANES), F32)],
        compiler_params=_cparams(("arbitrary",)),
        name="moe_rank",
    )(ti)


def _dest_kernel(cnt_ref, ti_ref, rank_ref, dest_ref, be_ref):
    tt = ti_ref.shape[0]
    nbp = be_ref.shape[0]
    cnt = cnt_ref[...].astype(I32)
    padded = ((cnt + (MOE_ROWS - 1)) >> 8) << 8
    lane8 = _iota((SUBLANES, LANES), 1)
    pad_end = padded
    sft = 1
    while sft < LANES:
        pad_end = pad_end + jnp.where(lane8 >= sft, pltpu.roll(pad_end, sft, 1), 0)
        sft *= 2
    pad_start = (pad_end - padded)[0:1, :].astype(F32)
    ti = ti_ref[...]
    lane = _iota((tt, LANES), 1)
    rank = rank_ref[...]
    dests = [jnp.sum(jnp.where(ti[:, k:k + 1] == lane, pad_start, 0.0), axis=-1, keepdims=True) + rank[:, k:k + 1]
             for k in range(TOP_K)]
    dest_ref[...] = _pack_cols(dests, F32).astype(I32)
    first_row = _iota((nbp, LANES), 0) * MOE_ROWS
    done = (pad_end[0:1, :] <= first_row) & (_iota((nbp, LANES), 1) < N_EXPERTS)
    be = jnp.minimum(jnp.sum(jnp.where(done, 1.0, 0.0), axis=-1, keepdims=True), float(N_EXPERTS - 1))
    used = first_row < pad_end[0:1, N_EXPERTS - 1:N_EXPERTS]
    be_ref[...] = jnp.where(_iota((nbp, LANES), 1) == 1, jnp.where(used, 1, 0), be.astype(I32))


def _expert_dest(cnt, ti, rank, n_blk, tt=256):
    T = ti.shape[0]
    return pl.pallas_call(
        _dest_kernel,
        out_shape=[jax.ShapeDtypeStruct((T, LANES), I32), jax.ShapeDtypeStruct((n_blk, LANES), I32)],
        grid=(T // tt,),
        in_specs=[pl.BlockSpec((SUBLANES, LANES), lambda i: (0, 0)),
                  pl.BlockSpec((tt, LANES), lambda i: (i, 0)),
                  pl.BlockSpec((tt, LANES), lambda i: (i, 0))],
        out_specs=[pl.BlockSpec((tt, LANES), lambda i: (i, 0)),
                   pl.BlockSpec((n_blk, LANES), lambda i: (0, 0))],
        compiler_params=_cparams(("arbitrary",)),
        name="moe_dest",
    )(cnt, ti, rank)


def _scatter_kernel(dest_ref, src_ref, init_ref, dst_ref, sem):
    del init_ref
    tt = src_ref.shape[0]

    def copy(t, k):
        return pltpu.make_async_copy(src_ref.at[t], dst_ref.at[dest_ref[t * TOP_K + k]], sem)

    def issue(t, c):
        for k in range(TOP_K):
            copy(t, k).start()
        return c

    lax.fori_loop(0, tt, issue, 0)

    def drain(t, c):
        for k in range(TOP_K):
            copy(t, k).wait()
        return c

    lax.fori_loop(0, tt, drain, 0)


def _scatter_rows(dest_flat, rows, init, tt=512):
    T = rows.shape[0]
    slab = rows.shape[1:]
    return pl.pallas_call(
        _scatter_kernel,
        out_shape=jax.ShapeDtypeStruct(init.shape, init.dtype),
        grid=(T // tt,),
        in_specs=[pl.BlockSpec((tt * TOP_K,), lambda i: (i,), memory_space=pltpu.SMEM),
                  pl.BlockSpec((tt,) + slab, lambda i: (i, 0, 0)),
                  pl.BlockSpec(memory_space=pl.ANY)],
        out_specs=pl.BlockSpec(memory_space=pl.ANY),
        scratch_shapes=[pltpu.SemaphoreType.DMA],
        input_output_aliases={2: 0},
        compiler_params=_cparams(("arbitrary",)),
        name="moe_scatter_rows",
    )(dest_flat, rows, init)


def _expert_kernel(be_ref, used_ref, x_ref, w1_ref, b1_ref, w2_ref, b2_ref, y_ref, w1b_ref, w2b_ref):
    i = pl.program_id(0)

    @pl.when(jnp.logical_or(i == 0, be_ref[i] != be_ref[jnp.maximum(i - 1, 0)]))
    def _():
        w1b_ref[...] = w1_ref[0].astype(BF16)
        w2b_ref[...] = w2_ref[0].astype(BF16)

    @pl.when(used_ref[i] != 0)
    def _():
        x = jnp.concatenate([x_ref[:, s, :] for s in range(SUBLANES)], axis=1).astype(BF16)
        h = jnp.dot(x, w1b_ref[...], preferred_element_type=F32) + b1_ref[0]
        glu = jnp.minimum(h[:, :D_EXPERT], SWIGLU_LIMIT)
        lin = jnp.clip(h[:, D_EXPERT:], -SWIGLU_LIMIT, SWIGLU_LIMIT)
        act = glu * _sigmoid(SWIGLU_ALPHA * glu) * (lin + 1.0)
        y = jnp.dot(act.astype(BF16), w2b_ref[...], preferred_element_type=F32) + b2_ref[0]
        for s in range(SUBLANES):
            y_ref[:, s, :] = y[:, s * LANES:(s + 1) * LANES]

    @pl.when(used_ref[i] == 0)
    def _():
        y_ref[...] = jnp.zeros_like(y_ref)


def _expert_ffn(blk_expert, blk_used, x_rows, w1, b1, w2, b2):
    P = x_rows.shape[0]
    slab = x_rows.shape[1:]
    D = slab[0] * slab[1]
    nb = P // MOE_ROWS
    grid_spec = pltpu.PrefetchScalarGridSpec(
        num_scalar_prefetch=2,
        grid=(nb,),
        in_specs=[pl.BlockSpec((MOE_ROWS,) + slab, lambda i, be, us: (i, 0, 0)),
                  pl.BlockSpec((1, D, 2 * D_EXPERT), lambda i, be, us: (be[i], 0, 0)),
                  pl.BlockSpec((1, 1, 2 * D_EXPERT), lambda i, be, us: (be[i], 0, 0)),
                  pl.BlockSpec((1, D_EXPERT, D), lambda i, be, us: (be[i], 0, 0)),
                  pl.BlockSpec((1, 1, D), lambda i, be, us: (be[i], 0, 0))],
        out_specs=pl.BlockSpec((MOE_ROWS,) + slab, lambda i, be, us: (i, 0, 0)),
        scratch_shapes=[pltpu.VMEM((D, 2 * D_EXPERT), BF16), pltpu.VMEM((D_EXPERT, D), BF16)],
    )
    return pl.pallas_call(
        _expert_kernel,
        out_shape=jax.ShapeDtypeStruct((P,) + slab, F32),
        grid_spec=grid_spec,
        compiler_params=_cparams(("arbitrary",), VMEM_LIMIT),
        name="moe_expert_ffn",
    )(blk_expert, blk_used, x_rows, w1, b1, w2, b2)


def _combine_kernel(dest_ref, next_ref, gate_ref, y_ref, o_ref, buf_ref, sems):
    n = dest_ref.shape[0]
    tt = n // TOP_K
    i = pl.program_id(0)
    slot = i % 2

    def copy(idx_ref, slot_, t, k):
        return pltpu.make_async_copy(y_ref.at[idx_ref[t * TOP_K + k]], buf_ref.at[slot_, k, t], sems.at[slot_])

    def issue(idx_ref, slot_):
        def body(t, c):
            for k in range(TOP_K):
                copy(idx_ref, slot_, t, k).start()
            return c
        lax.fori_loop(0, tt, body, 0)

    @pl.when(i == 0)
    def _():
        issue(dest_ref, slot)

    @pl.when(i + 1 < pl.num_programs(0))
    def _():
        issue(next_ref, 1 - slot)

    def drain(t, c):
        for k in range(TOP_K):
            copy(dest_ref, slot, t, k).wait()
        return c

    lax.fori_loop(0, tt, drain, 0)

    def mix(t, c):
        acc = gate_ref[t * TOP_K] * buf_ref[slot, 0, t]
        for k in range(1, TOP_K):
            acc = acc + gate_ref[t * TOP_K + k] * buf_ref[slot, k, t]
        o_ref[t] = acc
        return c

    lax.fori_loop(0, tt, mix, 0)


def _combine_rows(dest_flat, gate_flat, y_rows, T, tt=256):
    slab = y_rows.shape[1:]
    last = T // tt - 1
    return pl.pallas_call(
        _combine_kernel,
        out_shape=jax.ShapeDtypeStruct((T,) + slab, F32),
        grid=(T // tt,),
        in_specs=[pl.BlockSpec((tt * TOP_K,), lambda i: (i,), memory_space=pltpu.SMEM),
                  pl.BlockSpec((tt * TOP_K,), lambda i: (jnp.minimum(i + 1, last),), memory_space=pltpu.SMEM),
                  pl.BlockSpec((tt * TOP_K,), lambda i: (i,), memory_space=pltpu.SMEM),
                  pl.BlockSpec(memory_space=pl.ANY)],
        out_specs=pl.BlockSpec((tt,) + slab, lambda i: (i, 0, 0)),
        scratch_shapes=[pltpu.VMEM((2, TOP_K, tt) + slab, F32), pltpu.SemaphoreType.DMA((2,))],
        compiler_params=_cparams(("arbitrary",)),
        name="moe_combine_rows",
    )(dest_flat, dest_flat, gate_flat, y_rows)


def _final_kernel(h_ref, y_ref, gt_ref, g_ref, o_ref, *, normalize):
    y = jnp.concatenate([y_ref[0, :, s, :] for s in range(SUBLANES)], axis=1)
    h = h_ref[0] + (1.0 + gt_ref[0]) * y
    if normalize:
        h = h * lax.rsqrt(jnp.mean(h * h, axis=-1, keepdims=True) + NORM_EPS) * g_ref[...]
    o_ref[0] = h


def _residual(h1, y, gt2, g, normalize, tm=512):
    B, S, D = h1.shape
    rows = pl.BlockSpec((1, tm, D), lambda b, i: (b, i, 0))
    return pl.pallas_call(
        functools.partial(_final_kernel, normalize=normalize),
        out_shape=jax.ShapeDtypeStruct((B, S, D), F32),
        grid=(B, S // tm),
        in_specs=[rows, pl.BlockSpec((1, tm) + y.shape[2:], lambda b, i: (b, i, 0, 0)),
                  pl.BlockSpec((1, 1, D), lambda b, i: (b, 0, 0)),
                  pl.BlockSpec((1, D), lambda b, i: (0, 0))],
        out_specs=rows,
        compiler_params=_cparams(("parallel", "parallel")),
        name="final_norm",
    )(h1, y, gt2, g)


def _overlap_matrix(S):
    nc = S // CMP_STRIDE - 1
    ns = S // SEL_BLOCK
    c0 = np.arange(nc) * CMP_STRIDE
    s0 = np.arange(ns) * SEL_BLOCK
    ov = np.clip(np.minimum(c0[:, None] + CMP_LEN, s0[None, :] + SEL_BLOCK)
                 - np.maximum(c0[:, None], s0[None, :]), 0, None) / CMP_LEN
    return np.concatenate([ov, np.zeros((1, ns))], axis=0).astype(np.float32)


def _layer(h, mods, norm1_g, w_in, pe_k, pe_v, cw1_k, cw2_k, cw1_v, cw2_v, conv_w, conv_b, gate_b, ml_norm_g,
           w_up_nsa, w_up_ml, w_out, norm2_g, router_w, router_b, exp_w1, exp_b1, exp_w2, exp_b2):
    B, S, D = h.shape
    T = B * S
    sh1, sc1, gt1, sh2, sc2, gt2 = mods

    w_r = jnp.concatenate([w_in[:, 0:1280], w_in[:, 1304:2328], w_in[:, 2328:2840], w_in[:, 2848:3360],
                           w_in[:, 3360:5408], w_in[:, 1280:1304], w_in[:, 2840:2848],
                           jnp.zeros((D, IN_WR - 5408), w_in.dtype)], axis=1).astype(BF16)
    q, kvc, kvs, ml_qk, ml_v, ml_o, merge, small = _in_projection(h, norm1_g.reshape(1, D), sc1, sh1, w_r)

    nb16 = S // CMP_STRIDE
    t16 = kvc.reshape(B, S, 4, NSA_DH).transpose(0, 2, 1, 3).reshape(B, 4, nb16, CMP_STRIDE * NSA_DH)
    pe = jnp.stack([pe_k.reshape(2, CMP_STRIDE * NSA_DH), pe_v.reshape(2, CMP_STRIDE * NSA_DH)])
    cmp = _compress(t16, pe, jnp.stack([cw1_k, cw1_v]), jnp.stack([cw2_k, cw2_v]))
    n_sel = min(SEL_TOPN, S // SEL_BLOCK)
    kc = cmp[:, :NSA_GROUPS].transpose(0, 2, 1, 3).reshape(B, nb16, NSA_KVW).astype(BF16)
    vct = cmp[:, NSA_GROUPS:].transpose(0, 1, 3, 2).reshape(B, NSA_KVW, nb16).astype(BF16)

    def key_major_tiles(v):
        return v.reshape(B, S // LANES, LANES, LANES).swapaxes(-1, -2)

    o_nsa = _nsa_attention(q, kc, vct, jnp.asarray(_overlap_matrix(S).T), kvs,
                           key_major_tiles(kvs[:, :, LANES:2 * LANES]), key_major_tiles(kvs[:, :, 3 * LANES:]),
                           small, n_sel)

    gates = small[:, :, 3 * NSA_HEADS:3 * NSA_HEADS + 2 * ML_HEADS].transpose(0, 2, 1)
    gate_row = gates.reshape(B, 2 * ML_HEADS, S // ML_CHUNK, ML_CHUNK)
    gate_col = gates.reshape(B, 2 * ML_HEADS, S, 1)
    o_ml = _mlstm(ml_qk, ml_v, ml_o, gate_row, gate_col, gate_b, conv_w, conv_b.reshape(1, -1),
                  ml_norm_g.reshape(1, -1))

    h1, n2, ti, tg = _out_projection(o_nsa, o_ml, merge, h, gt1, sc2, sh2, norm2_g.reshape(1, D),
                                     w_up_nsa.astype(BF16), w_up_ml.astype(BF16), w_out.astype(BF16),
                                     router_w, router_b.reshape(1, N_EXPERTS))

    A = T * TOP_K
    n_blk = -(-(A + N_EXPERTS * (MOE_ROWS - 1)) // MOE_ROWS)
    P = n_blk * MOE_ROWS
    ti2 = ti.reshape(T, LANES)
    rank, cnt = _expert_rank(ti2)
    dest, blk_e = _expert_dest(cnt, ti2, rank, n_blk)
    dest_flat = dest[:, :TOP_K].reshape(A)
    gate_flat = tg.reshape(T, LANES)[:, :TOP_K].reshape(A)
    slab = n2.shape[2:]
    x_rows = _scatter_rows(dest_flat, n2.reshape((T,) + slab), jnp.zeros((P,) + slab, F32))
    y_rows = _expert_ffn(blk_e[:, 0], blk_e[:, 1], x_rows, exp_w1, exp_b1.reshape(N_EXPERTS, 1, -1),
                         exp_w2, exp_b2.reshape(N_EXPERTS, 1, -1))
    y = _combine_rows(dest_flat, gate_flat, y_rows, T)
    return h1, y.reshape((B, S) + slab), gt2


def kernel(x, c, ada_w, ada_b, norm1_g, w_in, nsa_pe_k, nsa_pe_v, nsa_cmp_w1_k, nsa_cmp_w2_k, nsa_cmp_w1_v, nsa_cmp_w2_v, ml_conv_w, ml_conv_b, ml_gate_b, ml_norm_g, w_up_nsa, w_up_ml, w_out, norm2_g, router_w, router_b, exp_w1, exp_b1, exp_w2, exp_b2, final_g):
    B, S, D = x.shape
    depth = ada_w.shape[0]
    h = x
    for layer in range(depth):
        mod = _modulation(c, ada_w[layer], ada_b[layer])
        mods = [mod[:, i * D:(i + 1) * D].reshape(B, 1, D) for i in range(6)]
        h1, y, gt2 = _layer(h, mods, norm1_g[layer], w_in[layer], nsa_pe_k[layer], nsa_pe_v[layer],
                            nsa_cmp_w1_k[layer], nsa_cmp_w2_k[layer], nsa_cmp_w1_v[layer], nsa_cmp_w2_v[layer],
                            ml_conv_w[layer], ml_conv_b[layer], ml_gate_b[layer], ml_norm_g[layer],
                            w_up_nsa[layer], w_up_ml[layer], w_out[layer], norm2_g[layer], router_w[layer],
                            router_b[layer], exp_w1[layer], exp_b1[layer], exp_w2[layer], exp_b2[layer])
        h = _residual(h1, y, gt2, final_g.reshape(1, D), normalize=(layer + 1 == depth))
    return h
```

```python
import functools

import numpy as np
import jax
import jax.numpy as jnp
from jax import lax
from jax.experimental import pallas as pl
from jax.experimental.pallas import tpu as pltpu

F32 = jnp.float32
BF16 = jnp.bfloat16
I32 = jnp.int32
HIGHEST = lax.Precision.HIGHEST

D_MODEL = 1024
NSA_HEADS = 8
NSA_GROUPS = 2
NSA_REP = NSA_HEADS // NSA_GROUPS
NSA_DH = 64
NSA_SCALE = NSA_DH ** -0.5
CMP_STRIDE = 16
CMP_LEN = 32
CMP_HIDDEN = 128
SEL_BLOCK = 64
SEL_TOPN = 16
WINDOW = 512
Q_BLOCK = 128
ML_HEADS = 4
ML_DH = 128
ML_CHUNK = 64
CONV_WIDTH = 4
N_EXPERTS = 32
TOP_K = 4
D_EXPERT = 1024
SWIGLU_LIMIT = 7.0
SWIGLU_ALPHA = 1.702
MOE_ROWS = 256
NORM_EPS = 1e-6
NEG_INF = -1e30
FORCE_BONUS = 1e4

NSA_QW = NSA_HEADS * NSA_DH
NSA_KVW = NSA_GROUPS * NSA_DH
ML_W = ML_HEADS * ML_DH
LANES = 128
SUBLANES = 8
VMEM_LIMIT = 52 * 1024 * 1024

_C_Q = (0, 512)
_C_KVC = (512, 768)
_C_KVS = (768, 1280)
_C_MQK = (1280, 2304)
_C_MV = (2304, 2816)
_C_MO = (2816, 3328)
_C_MG = (3328, 5376)
_C_SM = (5376, 5504)
IN_WR = 5504


def _sigmoid(x):
    return 1.0 / (1.0 + jnp.exp(-x))


def _iota(shape, dim):
    return lax.broadcasted_iota(I32, shape, dim)


def _cparams(sem, vmem=None):
    return pltpu.CompilerParams(dimension_semantics=sem, vmem_limit_bytes=vmem)


def _tiles_to_rows(ref, n, lead=()):
    return jnp.concatenate([ref[lead + (pl.ds(s, n, stride=SUBLANES), slice(None))] for s in range(SUBLANES)],
                           axis=1)


def _rows_to_tiles(ref, x, lead=()):
    n = x.shape[0]
    for s in range(SUBLANES):
        ref[lead + (pl.ds(s, n, stride=SUBLANES), slice(None))] = x[:, s * LANES:(s + 1) * LANES]


def _mod_kernel(c_ref, w_ref, b_ref, o_ref):
    c = c_ref[...]
    ca = c * _sigmoid(c)
    o_ref[...] = jnp.dot(ca, w_ref[...], precision=HIGHEST, preferred_element_type=F32) + b_ref[...]


def _modulation(c, ada_w, ada_b):
    B, D = c.shape
    n = ada_w.shape[1] // D
    return pl.pallas_call(
        _mod_kernel,
        out_shape=jax.ShapeDtypeStruct((B, n * D), F32),
        grid=(n,),
        in_specs=[pl.BlockSpec((B, D), lambda j: (0, 0)),
                  pl.BlockSpec((D, D), lambda j: (0, j)),
                  pl.BlockSpec((1, D), lambda j: (0, j))],
        out_specs=pl.BlockSpec((B, D), lambda j: (0, j)),
        compiler_params=_cparams(("parallel",)),
        name="modulation",
    )(c, ada_w, ada_b.reshape(1, n * D))


def _inproj_kernel(x_ref, g_ref, sc_ref, sh_ref, w_ref,
                   q_ref, kvc_ref, kvs_ref, mqk_ref, mv_ref, mo_ref, mg_ref, sm_ref):
    x = x_ref[0]
    y = x * lax.rsqrt(jnp.mean(x * x, axis=-1, keepdims=True) + NORM_EPS) * g_ref[...]
    n = (y * (1.0 + sc_ref[0]) + sh_ref[0]).astype(BF16)

    def proj(c):
        return jnp.dot(n, w_ref[:, c[0]:c[1]], preferred_element_type=F32)

    q_ref[0] = (proj(_C_Q) * NSA_SCALE).astype(BF16)
    kvc_ref[0] = proj(_C_KVC)
    kvs_ref[0] = proj(_C_KVS).astype(BF16)
    mqk_ref[0] = proj(_C_MQK)
    mv_ref[0] = proj(_C_MV)
    mo_ref[0] = proj(_C_MO)
    mg_ref[0] = proj(_C_MG)
    sm_ref[0] = proj(_C_SM)


def _in_projection(x, g, sc, sh, w_r, tm=256):
    B, S, D = x.shape
    widths = [c[1] - c[0] for c in (_C_Q, _C_KVC, _C_KVS, _C_MQK, _C_MV, _C_MO, _C_MG, _C_SM)]
    dtypes = [BF16, F32, BF16, F32, F32, F32, F32, F32]
    vec = pl.BlockSpec((1, 1, D), lambda b, i: (b, 0, 0))
    return pl.pallas_call(
        _inproj_kernel,
        out_shape=[jax.ShapeDtypeStruct((B, S, w), dt) for w, dt in zip(widths, dtypes)],
        grid=(B, S // tm),
        in_specs=[pl.BlockSpec((1, tm, D), lambda b, i: (b, i, 0)),
                  pl.BlockSpec((1, D), lambda b, i: (0, 0)),
                  vec, vec,
                  pl.BlockSpec((D, IN_WR), lambda b, i: (0, 0))],
        out_specs=[pl.BlockSpec((1, tm, w), lambda b, i: (b, i, 0)) for w in widths],
        compiler_params=_cparams(("parallel", "parallel"), VMEM_LIMIT),
        name="in_projection",
    )(x, g, sc, sh, w_r)


def _compress_kernel(t_ref, pe_ref, w1_ref, w2_ref, o_ref):
    half = CMP_STRIDE * NSA_DH
    t = t_ref[0, 0]
    pe = pe_ref[0]
    nb = t.shape[0]
    a = jnp.dot((t + pe[0:1]).astype(BF16), w1_ref[0, :half, :].astype(BF16), preferred_element_type=F32)
    b = jnp.dot((t + pe[1:2]).astype(BF16), w1_ref[0, half:, :].astype(BF16), preferred_element_type=F32)
    h = a + pltpu.roll(b, nb - 1, 0)
    gl = 0.5 * h * (1.0 + jnp.tanh(np.sqrt(2.0 / np.pi).astype(np.float32) * (h + 0.044715 * (h * h * h))))
    o_ref[0, 0] = jnp.dot(gl.astype(BF16), w2_ref[0].astype(BF16), preferred_element_type=F32)


def _compress(t16, pe, w1, w2):
    B, _, nb, wdt = t16.shape
    return pl.pallas_call(
        _compress_kernel,
        out_shape=jax.ShapeDtypeStruct((B, 4, nb, NSA_DH), F32),
        grid=(B, 4),
        in_specs=[pl.BlockSpec((1, 1, nb, wdt), lambda b, j: (b, j, 0, 0)),
                  pl.BlockSpec((1, 2, wdt), lambda b, j: (j // 2, 0, 0)),
                  pl.BlockSpec((1, CMP_LEN * NSA_DH, CMP_HIDDEN), lambda b, j: (j // 2, 0, 0)),
                  pl.BlockSpec((1, CMP_HIDDEN, NSA_DH), lambda b, j: (j // 2, 0, 0))],
        out_specs=pl.BlockSpec((1, 1, nb, NSA_DH), lambda b, j: (b, j, 0, 0)),
        compiler_params=_cparams(("parallel", "parallel")),
        name="nsa_compress",
    )(t16, pe, w1, w2)


def _nsa_kernel(q_ref, kc_ref, vct_ref, ovt_ref, ks_ref, vst_ref, kw_ref, vwt_ref, sm_ref, o_ref, *, n_sel, tk):
    qb = Q_BLOCK
    ncp = kc_ref.shape[1]
    ns = ovt_ref.shape[0]
    q0 = pl.program_id(1) * qb
    t_l = q0 + _iota((1, qb), 1)
    sub = _iota((LANES, qb), 0)
    gT = _sigmoid(sm_ref[0].T)

    gw = NSA_REP * qb
    q_heads = []
    for c in range(NSA_HEADS // 2):
        qc = q_ref[0, :, c * LANES:(c + 1) * LANES].astype(F32).T
        for e in range(2):
            g = (2 * c + e) // NSA_REP
            x = qc if e == g else pltpu.roll(qc, NSA_DH, 0)
            q_heads.append(jnp.where((sub >> 6) == g, x, 0.0).astype(BF16))
    qT = [jnp.concatenate(q_heads[g * NSA_REP:(g + 1) * NSA_REP], axis=1) for g in range(NSA_GROUPS)]
    t_g = q0 + (_iota((1, gw), 1) & (qb - 1))

    def head_cols(x, r):
        return x[:, r * qb:(r + 1) * qb]

    def gate_row(branch, g):
        rows = [gT[branch * NSA_HEADS + g * NSA_REP + r:branch * NSA_HEADS + g * NSA_REP + r + 1]
                for r in range(NSA_REP)]
        return jnp.concatenate(rows, axis=1)

    kc = kc_ref[0]
    vct = vct_ref[0]
    cmask = (_iota((ncp, gw), 0) * CMP_STRIDE + (CMP_LEN - 1)) <= t_g
    blk = _iota((ns, qb), 0)
    blk_f = blk.astype(F32)
    cur = t_l >> 6
    forced = (blk == 0) | (blk == cur) | (blk == cur - 1)
    ocT = []
    selT = []
    for g in range(NSA_GROUPS):
        s = jnp.dot(kc, qT[g], preferred_element_type=F32)
        s = jnp.where(cmask, s, NEG_INF)
        e = jnp.exp(s - jnp.max(s, axis=0, keepdims=True))
        p = e * (1.0 / jnp.sum(e, axis=0, keepdims=True))
        p = jnp.where(cmask, p, 0.0)
        ocT.append(jnp.dot(vct, p.astype(BF16), preferred_element_type=F32))
        psum = head_cols(p, 0)
        for r in range(1, NSA_REP):
            psum = psum + head_cols(p, r)
        imp = jnp.dot(ovt_ref[...], psum, precision=HIGHEST, preferred_element_type=F32)
        imp = jnp.where(blk > cur, NEG_INF, imp + jnp.where(forced, FORCE_BONUS, 0.0))
        sel = jnp.zeros((ns, qb), F32)
        for _ in range(n_sel):
            mx = jnp.max(imp, axis=0, keepdims=True)
            first = jnp.min(jnp.where(imp == mx, blk_f, float(ns)), axis=0, keepdims=True)
            pick = blk_f == first
            sel = jnp.where(pick, 1.0, sel)
            imp = jnp.where(pick, -jnp.inf, imp)
        selT.append(sel.astype(BF16))

    wlen = WINDOW + qb
    wt0 = jnp.maximum(q0 - WINDOW, 0) // LANES
    wstart = pl.multiple_of(wt0 * LANES, LANES)
    kwin = kw_ref[0, pl.ds(wstart, wlen), :]
    rel = t_g - (wstart + _iota((wlen, gw), 0))
    wmask = lax.bitcast_convert_type(rel, jnp.uint32) < WINDOW
    partial = []
    for g in range(NSA_GROUPS):
        lo = g * NSA_DH
        s = jnp.dot(kwin, qT[g], preferred_element_type=F32)
        s = jnp.where(wmask, s, NEG_INF)
        p = jnp.exp(s - jnp.max(s, axis=0, keepdims=True))
        pb = p.astype(BF16)
        ow = jnp.dot(vwt_ref[0, wt0], pb[:LANES], preferred_element_type=F32)
        for i in range(1, wlen // LANES):
            ow = ow + jnp.dot(vwt_ref[0, wt0 + i], pb[i * LANES:(i + 1) * LANES], preferred_element_type=F32)
        ow = ow * (1.0 / jnp.sum(p, axis=0, keepdims=True))
        partial.append(gate_row(0, g) * ocT[g][lo:lo + NSA_DH] + gate_row(2, g) * ow[lo:lo + NSA_DH])

    nt = (q0 + qb + tk - 1) // tk

    def make_step(diagonal):
        def step(j, carry):
            r0 = pl.multiple_of(j * tk, tk)
            kt = ks_ref[0, pl.ds(r0, tk), :]
            expand = jnp.where(((r0 + _iota((tk, ns), 0)) >> 6) == _iota((tk, ns), 1), 1.0, 0.0).astype(BF16)
            new = []
            for g in range(NSA_GROUPS):
                m, l, acc = carry[g]
                hit = jnp.dot(expand, selT[g], preferred_element_type=F32)
                if diagonal:
                    hit = jnp.where((r0 + _iota((tk, qb), 0)) <= t_l, hit, 0.0)
                ok = hit > 0.5
                s = jnp.dot(kt, qT[g], preferred_element_type=F32)
                s = jnp.concatenate([jnp.where(ok, head_cols(s, r), NEG_INF) for r in range(NSA_REP)], axis=1)
                m_new = jnp.maximum(m, jnp.max(s, axis=0, keepdims=True))
                alpha = jnp.exp(m - m_new)
                p = jnp.exp(s - m_new)
                pv = jnp.dot(vst_ref[0, j], p.astype(BF16), preferred_element_type=F32)
                new.append((m_new, alpha * l + jnp.sum(p, axis=0, keepdims=True), alpha * acc + pv))
            return tuple(new)
        return step

    init = tuple((jnp.full((1, gw), NEG_INF, F32), jnp.zeros((1, gw), F32), jnp.zeros((LANES, gw), F32))
                 for _ in range(NSA_GROUPS))
    final = lax.fori_loop(0, nt - 1, make_step(False), init)
    final = make_step(True)(nt - 1, final)

    heads = []
    for g in range(NSA_GROUPS):
        _, l, acc = final[g]
        o_g = partial[g] + gate_row(1, g) * (acc[g * NSA_DH:(g + 1) * NSA_DH] * (1.0 / l))
        heads += [head_cols(o_g, r) for r in range(NSA_REP)]
    for c in range(NSA_HEADS // 2):
        o_ref[0, :, c * LANES:(c + 1) * LANES] = jnp.concatenate(heads[2 * c:2 * c + 2], axis=0).T


def _nsa_attention(q, kc, vct, ovt, kvs, vst, vwt, small, n_sel):
    B, S, _ = q.shape
    ns, ncp = ovt.shape
    tk = vst.shape[3]
    assert S >= WINDOW + Q_BLOCK and S % tk == 0 and tk % LANES == 0
    nt128 = S // LANES

    def whole(shape):
        return pl.BlockSpec((1,) + shape, lambda b, i: (b,) + (0,) * len(shape))

    return pl.pallas_call(
        functools.partial(_nsa_kernel, n_sel=n_sel, tk=tk),
        out_shape=jax.ShapeDtypeStruct((B, S, NSA_QW), F32),
        grid=(B, S // Q_BLOCK),
        in_specs=[pl.BlockSpec((1, Q_BLOCK, NSA_QW), lambda b, i: (b, i, 0)),
                  whole((ncp, LANES)), whole((LANES, ncp)),
                  pl.BlockSpec((ns, ncp), lambda b, i: (0, 0)),
                  pl.BlockSpec((1, S, LANES), lambda b, i: (b, 0, 0)),
                  whole((S // tk, LANES, tk)),
                  pl.BlockSpec((1, S, LANES), lambda b, i: (b, 0, 2)),
                  whole((nt128, LANES, LANES)),
                  pl.BlockSpec((1, Q_BLOCK, LANES), lambda b, i: (b, i, 0))],
        out_specs=pl.BlockSpec((1, Q_BLOCK, NSA_QW), lambda b, i: (b, i, 0)),
        compiler_params=_cparams(("parallel", "parallel"), VMEM_LIMIT),
        name="nsa_attention",
    )(q, kc, vct, ovt, kvs, vst, kvs, vwt, small)


def _mlstm_kernel(gb_ref, q_ref, k_ref, v_ref, o_ref, grow_ref, gcol_ref, cw_ref, cb_ref, ng_ref, out_ref,
                  c_scr, n_scr, m_scr, hq_scr, hk_scr):
    L = ML_CHUNK
    H = ML_HEADS
    ts = q_ref.shape[1]
    kscale = ML_DH ** -0.5
    row = _iota((L, L), 0)
    col = _iota((L, L), 1)
    tri = row >= col
    halo_rows = SUBLANES

    @pl.when(pl.program_id(1) == 0)
    def _():
        c_scr[...] = jnp.zeros_like(c_scr)
        n_scr[...] = jnp.zeros_like(n_scr)
        m_scr[...] = jnp.zeros_like(m_scr)
        hq_scr[...] = jnp.zeros_like(hq_scr)
        hk_scr[...] = jnp.zeros_like(hk_scr)

    def conv_silu(x_ref, prev_ref, woff, h, c, r0):
        lanes = slice(h * ML_DH, (h + 1) * ML_DH)
        wl = slice(woff + h * ML_DH, woff + (h + 1) * ML_DH)
        main = x_ref[0, pl.ds(r0, L), lanes]
        h0 = pl.multiple_of(jnp.maximum(r0 - halo_rows, 0), halo_rows)
        halo = jnp.where(c > 0, x_ref[0, pl.ds(h0, halo_rows), lanes], prev_ref[:, lanes])
        cat = jnp.concatenate([halo, main], axis=0)
        y = main * cw_ref[CONV_WIDTH - 1:CONV_WIDTH, wl] + cb_ref[:, wl]
        for w in range(CONV_WIDTH - 1):
            sft = CONV_WIDTH - 1 - w
            y = y + pltpu.roll(cat, sft, 0)[halo_rows:, :] * cw_ref[w:w + 1, wl]
        return y * _sigmoid(y)

    def logsig(x):
        return -(jnp.maximum(-x, 0.0) + jnp.log(1.0 + jnp.exp(-jnp.abs(x))))

    def head_step(h, c, r0, C, n, m):
        lanes = slice(h * ML_DH, (h + 1) * ML_DH)
        gb_i = gb_ref[h]
        gb_f = gb_ref[H + h]
        qc = conv_silu(q_ref, hq_scr, 0, h, c, r0)
        kc = conv_silu(k_ref, hk_scr, ML_W, h, c, r0) * kscale
        vc = v_ref[0, pl.ds(r0, L), lanes]
        li_row = grow_ref[0, h, pl.ds(c, 1), :] + gb_i
        lf_row = logsig(grow_ref[0, H + h, pl.ds(c, 1), :] + gb_f)
        li_col = gcol_ref[0, h, pl.ds(r0, L), :] + gb_i
        lf_col = logsig(gcol_ref[0, H + h, pl.ds(r0, L), :] + gb_f)
        b_col = jnp.sum(jnp.where(tri, lf_row, 0.0), axis=1, keepdims=True)
        b_row = jnp.sum(jnp.where(row <= col, lf_col, 0.0), axis=0, keepdims=True)
        Dm = jnp.where(tri, b_col - b_row + li_row, NEG_INF)
        inter = b_col + m
        m_t = jnp.maximum(inter, jnp.max(Dm, axis=1, keepdims=True))
        Dw = jnp.exp(Dm - m_t)
        inter_w = jnp.exp(inter - m_t)
        qb16 = qc.astype(BF16)
        kb16 = kc.astype(BF16)
        vb16 = vc.astype(BF16)
        qk = lax.dot_general(qb16, kb16, (((1,), (1,)), ((), ())), preferred_element_type=F32) * Dw
        num = (jnp.dot(qk.astype(BF16), vb16, preferred_element_type=F32)
               + inter_w * jnp.dot(qb16, C.astype(BF16), preferred_element_type=F32))
        den = jnp.sum(qk, axis=1, keepdims=True) + inter_w * jnp.sum(qc * n, axis=1, keepdims=True)
        hc = num / jnp.maximum(jnp.abs(den), jnp.exp(-m_t))
        bL = b_col[L - 1:L, :]
        a_col = bL - b_col + li_col
        m_new = jnp.maximum(bL + m, jnp.max(a_col, axis=0, keepdims=True))
        aw = jnp.exp(a_col - m_new)
        decay = jnp.exp(bL + m - m_new)
        awk = aw * kc
        C_new = decay * C + jnp.dot(awk.T.astype(BF16), vb16, preferred_element_type=F32)
        n_new = decay * n + jnp.sum(awk, axis=0, keepdims=True)
        hn = hc * lax.rsqrt(jnp.mean(hc * hc, axis=1, keepdims=True) + NORM_EPS) * ng_ref[:, lanes]
        out_ref[0, pl.ds(r0, L), lanes] = hn * _sigmoid(o_ref[0, pl.ds(r0, L), lanes])
        return C_new, n_new, m_new

    def step(c, carry):
        r0 = pl.multiple_of(c * L, L)
        return tuple(head_step(h, c, r0, *carry[h]) for h in range(H))

    init = tuple((c_scr[h], n_scr[h], m_scr[h][:, 0:1]) for h in range(H))
    final = lax.fori_loop(0, ts // L, step, init)
    for h in range(H):
        C, n, m = final[h]
        c_scr[h] = C
        n_scr[h] = n
        m_scr[h] = jnp.broadcast_to(m, (1, LANES))
    hq_scr[...] = q_ref[0, ts - halo_rows:ts, :]
    hk_scr[...] = k_ref[0, ts - halo_rows:ts, :]


def _mlstm(ml_qk, ml_v, ml_o, gate_row, gate_col, gate_b, conv_w, conv_b, norm_g, ts=1024):
    B, S, _ = ml_v.shape
    H = ML_HEADS
    ts = min(ts, S)

    def rows(cb):
        return pl.BlockSpec((1, ts, ML_W), lambda b, i, gb: (b, i, cb))

    def full(shape):
        return pl.BlockSpec(shape, lambda b, i, gb: (0,) * len(shape))

    grid_spec = pltpu.PrefetchScalarGridSpec(
        num_scalar_prefetch=1,
        grid=(B, S // ts),
        in_specs=[rows(0), rows(1), rows(0), rows(0),
                  pl.BlockSpec((1, 2 * H, ts // ML_CHUNK, ML_CHUNK), lambda b, i, gb: (b, 0, i, 0)),
                  pl.BlockSpec((1, 2 * H, ts, 1), lambda b, i, gb: (b, 0, i, 0)),
                  full((CONV_WIDTH, 2 * ML_W)), full((1, 2 * ML_W)), full((1, ML_W))],
        out_specs=rows(0),
        scratch_shapes=[pltpu.VMEM((H, ML_DH, ML_DH), F32), pltpu.VMEM((H, 1, ML_DH), F32),
                        pltpu.VMEM((H, 1, LANES), F32),
                        pltpu.VMEM((SUBLANES, ML_W), F32), pltpu.VMEM((SUBLANES, ML_W), F32)],
    )
    return pl.pallas_call(
        _mlstm_kernel,
        out_shape=jax.ShapeDtypeStruct((B, S, ML_W), F32),
        grid_spec=grid_spec,
        compiler_params=_cparams(("parallel", "arbitrary"), VMEM_LIMIT),
        name="mlstm",
    )(gate_b, ml_qk, ml_qk, ml_v, ml_o, gate_row, gate_col, conv_w, conv_b, norm_g)


def _pack_cols(cols, dtype):
    tm = cols[0].shape[0]
    lane = _iota((tm, LANES), 1)
    out = jnp.zeros((tm, LANES), dtype)
    for k, c in enumerate(cols):
        out = jnp.where(lane == k, c, out)
    return out


def _outproj_kernel(a_ref, b_ref, mg_ref, x_ref, gt_ref, sc_ref, sh_ref, g2_ref,
                    wa_ref, wb_ref, wo_ref, wr_ref, br_ref,
                    h_ref, n_ref, ti_ref, tg_ref):
    D = x_ref.shape[2]
    ua = jnp.dot(a_ref[0].astype(BF16), wa_ref[...], preferred_element_type=F32)
    ub = jnp.dot(b_ref[0].astype(BF16), wb_ref[...], preferred_element_type=F32)
    u = _sigmoid(mg_ref[0, :, :D]) * ua + _sigmoid(mg_ref[0, :, D:]) * ub
    mix = jnp.dot(u.astype(BF16), wo_ref[...], preferred_element_type=F32)
    h = x_ref[0] + (1.0 + gt_ref[0]) * mix
    h_ref[0] = h
    y = h * lax.rsqrt(jnp.mean(h * h, axis=-1, keepdims=True) + NORM_EPS) * g2_ref[...]
    n = y * (1.0 + sc_ref[0]) + sh_ref[0]
    _rows_to_tiles(n_ref, n, lead=(0,))
    logits = jnp.dot(n, wr_ref[...], precision=HIGHEST, preferred_element_type=F32) + br_ref[...]
    tm, ne = logits.shape
    lane = _iota((tm, ne), 1).astype(F32)
    work = logits
    vals, idxs = [], []
    for _ in range(TOP_K):
        mx = jnp.max(work, axis=-1, keepdims=True)
        ix = jnp.min(jnp.where(work == mx, lane, float(ne)), axis=-1, keepdims=True)
        vals.append(mx)
        idxs.append(ix)
        work = jnp.where(lane == ix, -jnp.inf, work)
    es = [jnp.exp(v - vals[0]) for v in vals]
    tot = es[0] + es[1] + es[2] + es[3]
    ti_ref[0] = _pack_cols(idxs, F32).astype(I32)
    tg_ref[0] = _pack_cols([e / tot for e in es], F32)


def _out_projection(o_nsa, o_ml, merge, x, gt1, sc2, sh2, g2, wa, wb, wo, wr, br, tm=256):
    B, S, D = x.shape
    vec = pl.BlockSpec((1, 1, D), lambda b, i: (b, 0, 0))

    def full(shape):
        return pl.BlockSpec(shape, lambda b, i: (0,) * len(shape))

    def rows(w):
        return pl.BlockSpec((1, tm, w), lambda b, i: (b, i, 0))

    assert D == SUBLANES * LANES
    return pl.pallas_call(
        _outproj_kernel,
        out_shape=[jax.ShapeDtypeStruct((B, S, D), F32), jax.ShapeDtypeStruct((B, S * SUBLANES, LANES), F32),
                   jax.ShapeDtypeStruct((B, S, LANES), I32), jax.ShapeDtypeStruct((B, S, LANES), F32)],
        grid=(B, S // tm),
        in_specs=[rows(NSA_QW), rows(ML_W), rows(2 * D), rows(D), vec, vec, vec, full((1, D)),
                  full(wa.shape), full(wb.shape), full(wo.shape), full(wr.shape), full((1, N_EXPERTS))],
        out_specs=[rows(D), pl.BlockSpec((1, tm * SUBLANES, LANES), lambda b, i: (b, i, 0)), rows(LANES),
                   rows(LANES)],
        compiler_params=_cparams(("parallel", "parallel"), VMEM_LIMIT),
        name="out_projection_router",
    )(o_nsa, o_ml, merge, x, gt1, sc2, sh2, g2, wa, wb, wo, wr, br)


def _rank_kernel(ti_ref, rank_ref, cnt_ref, carry_ref):
    tt = ti_ref.shape[0]

    @pl.when(pl.program_id(0) == 0)
    def _():
        carry_ref[...] = jnp.zeros_like(carry_ref)

    ti = ti_ref[...]
    lane = _iota((tt, LANES), 1)
    hots = [ti[:, k:k + 1] == lane for k in range(TOP_K)]
    member = jnp.zeros((tt, LANES), F32)
    for hot in hots:
        member = member + jnp.where(hot, 1.0, 0.0)
    strict = jnp.where(_iota((tt, tt), 0) > _iota((tt, tt), 1), 1.0, 0.0).astype(BF16)
    before = jnp.dot(strict, member.astype(BF16), preferred_element_type=F32) + carry_ref[0:1, :]
    ranks = [jnp.sum(jnp.where(hot, before, 0.0), axis=-1, keepdims=True) for hot in hots]
    rank_ref[...] = _pack_cols(ranks, F32)
    carry_ref[...] = carry_ref[...] + jnp.sum(member, axis=0, keepdims=True)
    cnt_ref[...] = carry_ref[...]


def _expert_rank(ti, tt=512):
    T = ti.shape[0]
    return pl.pallas_call(
        _rank_kernel,
        out_shape=[jax.ShapeDtypeStruct((T, LANES), F32), jax.ShapeDtypeStruct((SUBLANES, LANES), F32)],
        grid=(T // tt,),
        in_specs=[pl.BlockSpec((tt, LANES), lambda i: (i, 0))],
        out_specs=[pl.BlockSpec((tt, LANES), lambda i: (i, 0)),
                   pl.BlockSpec((SUBLANES, LANES), lambda i: (0, 0))],
        scratch_shapes=[pltpu.VMEM((SUBLANES, LANES), F32)],
        compiler_params=_cparams(("arbitrary",)),
        name="moe_rank",
    )(ti)


def _dest_kernel(cnt_ref, ti_ref, rank_ref, dest_ref, be_ref):
    tt = ti_ref.shape[0]
    nbp = be_ref.shape[0]
    cnt = cnt_ref[...].astype(I32)
    padded = ((cnt + (MOE_ROWS - 1)) >> 8) << 8
    lane8 = _iota((SUBLANES, LANES), 1)
    pad_end = padded
    sft = 1
    while sft < LANES:
        pad_end = pad_end + jnp.where(lane8 >= sft, pltpu.roll(pad_end, sft, 1), 0)
        sft *= 2
    pad_start = (pad_end - padded)[0:1, :].astype(F32)
    ti = ti_ref[...]
    lane = _iota((tt, LANES), 1)
    rank = rank_ref[...]
    dests = [jnp.sum(jnp.where(ti[:, k:k + 1] == lane, pad_start, 0.0), axis=-1, keepdims=True) + rank[:, k:k + 1]
             for k in range(TOP_K)]
    dest_ref[...] = _pack_cols(dests, F32).astype(I32) * SUBLANES
    first_row = _iota((nbp, LANES), 0) * MOE_ROWS
    done = (pad_end[0:1, :] <= first_row) & (_iota((nbp, LANES), 1) < N_EXPERTS)
    be = jnp.minimum(jnp.sum(jnp.where(done, 1.0, 0.0), axis=-1, keepdims=True), float(N_EXPERTS - 1))
    used = first_row < pad_end[0:1, N_EXPERTS - 1:N_EXPERTS]
    be_ref[...] = jnp.where(_iota((nbp, LANES), 1) == 1, jnp.where(used, 1, 0), be.astype(I32))


def _expert_dest(cnt, ti, rank, n_blk, tt=1024):
    T = ti.shape[0]
    return pl.pallas_call(
        _dest_kernel,
        out_shape=[jax.ShapeDtypeStruct((T, LANES), I32), jax.ShapeDtypeStruct((n_blk, LANES), I32)],
        grid=(T // tt,),
        in_specs=[pl.BlockSpec((SUBLANES, LANES), lambda i: (0, 0)),
                  pl.BlockSpec((tt, LANES), lambda i: (i, 0)),
                  pl.BlockSpec((tt, LANES), lambda i: (i, 0))],
        out_specs=[pl.BlockSpec((tt, LANES), lambda i: (i, 0)),
                   pl.BlockSpec((n_blk, LANES), lambda i: (0, 0))],
        compiler_params=_cparams(("arbitrary",)),
        name="moe_dest",
    )(cnt, ti, rank)


def _scatter_kernel(dest_ref, src_ref, init_ref, dst_ref, sem):
    del init_ref
    tt = src_ref.shape[0] // SUBLANES

    def copy(t, k):
        return pltpu.make_async_copy(
            src_ref.at[pl.ds(pl.multiple_of(t * SUBLANES, SUBLANES), SUBLANES)],
            dst_ref.at[pl.ds(pl.multiple_of(dest_ref[t * TOP_K + k], SUBLANES), SUBLANES)], sem)

    def issue(t, c):
        for k in range(TOP_K):
            copy(t, k).start()
        return c

    lax.fori_loop(0, tt, issue, 0)

    def drain(t, c):
        for k in range(TOP_K):
            copy(t, k).wait()
        return c

    lax.fori_loop(0, tt, drain, 0)


def _scatter_rows(dest_flat, rows, init, tt=512):
    T = rows.shape[0] // SUBLANES
    return pl.pallas_call(
        _scatter_kernel,
        out_shape=jax.ShapeDtypeStruct(init.shape, init.dtype),
        grid=(T // tt,),
        in_specs=[pl.BlockSpec((tt * TOP_K,), lambda i: (i,), memory_space=pltpu.SMEM),
                  pl.BlockSpec((tt * SUBLANES, LANES), lambda i: (i, 0)),
                  pl.BlockSpec(memory_space=pl.ANY)],
        out_specs=pl.BlockSpec(memory_space=pl.ANY),
        scratch_shapes=[pltpu.SemaphoreType.DMA],
        input_output_aliases={2: 0},
        compiler_params=_cparams(("arbitrary",)),
        name="moe_scatter_rows",
    )(dest_flat, rows, init)


def _expert_kernel(be_ref, used_ref, x_ref, w1_ref, b1_ref, w2_ref, b2_ref, y_ref, w1b_ref, w2b_ref):
    i = pl.program_id(0)

    @pl.when(jnp.logical_or(i == 0, be_ref[i] != be_ref[jnp.maximum(i - 1, 0)]))
    def _():
        w1b_ref[...] = w1_ref[0].astype(BF16)
        w2b_ref[...] = w2_ref[0].astype(BF16)

    @pl.when(used_ref[i] != 0)
    def _():
        x = _tiles_to_rows(x_ref, MOE_ROWS).astype(BF16)
        h = jnp.dot(x, w1b_ref[...], preferred_element_type=F32) + b1_ref[0]
        glu = jnp.minimum(h[:, :D_EXPERT], SWIGLU_LIMIT)
        lin = jnp.clip(h[:, D_EXPERT:], -SWIGLU_LIMIT, SWIGLU_LIMIT)
        act = glu * _sigmoid(SWIGLU_ALPHA * glu) * (lin + 1.0)
        y = jnp.dot(act.astype(BF16), w2b_ref[...], preferred_element_type=F32) + b2_ref[0]
        _rows_to_tiles(y_ref, y)

    @pl.when(used_ref[i] == 0)
    def _():
        y_ref[...] = jnp.zeros_like(y_ref)


def _expert_ffn(blk_expert, blk_used, x_rows, w1, b1, w2, b2):
    P = x_rows.shape[0] // SUBLANES
    D = SUBLANES * LANES
    nb = P // MOE_ROWS
    tile_rows = pl.BlockSpec((MOE_ROWS * SUBLANES, LANES), lambda i, be, us: (i, 0))
    grid_spec = pltpu.PrefetchScalarGridSpec(
        num_scalar_prefetch=2,
        grid=(nb,),
        in_specs=[tile_rows,
                  pl.BlockSpec((1, D, 2 * D_EXPERT), lambda i, be, us: (be[i], 0, 0)),
                  pl.BlockSpec((1, 1, 2 * D_EXPERT), lambda i, be, us: (be[i], 0, 0)),
                  pl.BlockSpec((1, D_EXPERT, D), lambda i, be, us: (be[i], 0, 0)),
                  pl.BlockSpec((1, 1, D), lambda i, be, us: (be[i], 0, 0))],
        out_specs=tile_rows,
        scratch_shapes=[pltpu.VMEM((D, 2 * D_EXPERT), BF16), pltpu.VMEM((D_EXPERT, D), BF16)],
    )
    return pl.pallas_call(
        _expert_kernel,
        out_shape=jax.ShapeDtypeStruct(x_rows.shape, F32),
        grid_spec=grid_spec,
        compiler_params=_cparams(("arbitrary",), VMEM_LIMIT),
        name="moe_expert_ffn",
    )(blk_expert, blk_used, x_rows, w1, b1, w2, b2)


def _combine_kernel(dest_ref, next_ref, gate_ref, y_ref, o_ref, buf_ref, sems):
    n = dest_ref.shape[0]
    tt = n // TOP_K
    i = pl.program_id(0)
    slot = i % 2

    def copy(idx_ref, slot_, t, k):
        src = y_ref.at[pl.ds(pl.multiple_of(idx_ref[t * TOP_K + k], SUBLANES), SUBLANES)]
        return pltpu.make_async_copy(src, buf_ref.at[slot_, k, t], sems.at[slot_])

    def issue(idx_ref, slot_):
        def body(t, c):
            for k in range(TOP_K):
                copy(idx_ref, slot_, t, k).start()
            return c
        lax.fori_loop(0, tt, body, 0)

    @pl.when(i == 0)
    def _():
        issue(dest_ref, slot)

    @pl.when(i + 1 < pl.num_programs(0))
    def _():
        issue(next_ref, 1 - slot)

    def drain(t, c):
        for k in range(TOP_K):
            copy(dest_ref, slot, t, k).wait()
        return c

    lax.fori_loop(0, tt, drain, 0)

    def mix(t, c):
        acc = gate_ref[t * TOP_K] * buf_ref[slot, 0, t]
        for k in range(1, TOP_K):
            acc = acc + gate_ref[t * TOP_K + k] * buf_ref[slot, k, t]
        o_ref[t] = acc
        return c

    lax.fori_loop(0, tt, mix, 0)


def _combine_rows(dest_flat, gate_flat, y_rows, T, tt=256):
    slab = (SUBLANES, LANES)
    last = T // tt - 1
    return pl.pallas_call(
        _combine_kernel,
        out_shape=jax.ShapeDtypeStruct((T,) + slab, F32),
        grid=(T // tt,),
        in_specs=[pl.BlockSpec((tt * TOP_K,), lambda i: (i,), memory_space=pltpu.SMEM),
                  pl.BlockSpec((tt * TOP_K,), lambda i: (jnp.minimum(i + 1, last),), memory_space=pltpu.SMEM),
                  pl.BlockSpec((tt * TOP_K,), lambda i: (i,), memory_space=pltpu.SMEM),
                  pl.BlockSpec(memory_space=pl.ANY)],
        out_specs=pl.BlockSpec((tt,) + slab, lambda i: (i, 0, 0)),
        scratch_shapes=[pltpu.VMEM((2, TOP_K, tt) + slab, F32), pltpu.SemaphoreType.DMA((2,))],
        compiler_params=_cparams(("arbitrary",)),
        name="moe_combine_rows",
    )(dest_flat, dest_flat, gate_flat, y_rows)


def _final_kernel(h_ref, y_ref, gt_ref, g_ref, o_ref, *, normalize):
    h = h_ref[0] + (1.0 + gt_ref[0]) * _tiles_to_rows(y_ref, h_ref.shape[1], lead=(0,))
    if normalize:
        h = h * lax.rsqrt(jnp.mean(h * h, axis=-1, keepdims=True) + NORM_EPS) * g_ref[...]
    o_ref[0] = h


def _residual(h1, y, gt2, g, normalize, tm=512):
    B, S, D = h1.shape
    rows = pl.BlockSpec((1, tm, D), lambda b, i: (b, i, 0))
    return pl.pallas_call(
        functools.partial(_final_kernel, normalize=normalize),
        out_shape=jax.ShapeDtypeStruct((B, S, D), F32),
        grid=(B, S // tm),
        in_specs=[rows, pl.BlockSpec((1, tm * SUBLANES, LANES), lambda b, i: (b, i, 0)),
                  pl.BlockSpec((1, 1, D), lambda b, i: (b, 0, 0)),
                  pl.BlockSpec((1, D), lambda b, i: (0, 0))],
        out_specs=rows,
        compiler_params=_cparams(("parallel", "parallel")),
        name="final_norm",
    )(h1, y, gt2, g)


def _overlap_matrix(S):
    nc = S // CMP_STRIDE - 1
    ns = S // SEL_BLOCK
    c0 = np.arange(nc) * CMP_STRIDE
    s0 = np.arange(ns) * SEL_BLOCK
    ov = np.clip(np.minimum(c0[:, None] + CMP_LEN, s0[None, :] + SEL_BLOCK)
                 - np.maximum(c0[:, None], s0[None, :]), 0, None) / CMP_LEN
    return np.concatenate([ov, np.zeros((1, ns))], axis=0).astype(np.float32)


def _layer(h, mods, norm1_g, w_in, pe_k, pe_v, cw1_k, cw2_k, cw1_v, cw2_v, conv_w, conv_b, gate_b, ml_norm_g,
           w_up_nsa, w_up_ml, w_out, norm2_g, router_w, router_b, exp_w1, exp_b1, exp_w2, exp_b2):
    B, S, D = h.shape
    T = B * S
    sh1, sc1, gt1, sh2, sc2, gt2 = mods

    w_r = jnp.concatenate([w_in[:, 0:1280], w_in[:, 1304:2328], w_in[:, 2328:2840], w_in[:, 2848:3360],
                           w_in[:, 3360:5408], w_in[:, 1280:1304], w_in[:, 2840:2848],
                           jnp.zeros((D, IN_WR - 5408), w_in.dtype)], axis=1).astype(BF16)
    q, kvc, kvs, ml_qk, ml_v, ml_o, merge, small = _in_projection(h, norm1_g.reshape(1, D), sc1, sh1, w_r)

    nb16 = S // CMP_STRIDE
    t16 = kvc.reshape(B, S, 4, NSA_DH).transpose(0, 2, 1, 3).reshape(B, 4, nb16, CMP_STRIDE * NSA_DH)
    pe = jnp.stack([pe_k.reshape(2, CMP_STRIDE * NSA_DH), pe_v.reshape(2, CMP_STRIDE * NSA_DH)])
    cmp = _compress(t16, pe, jnp.stack([cw1_k, cw1_v]), jnp.stack([cw2_k, cw2_v]))
    n_sel = min(SEL_TOPN, S // SEL_BLOCK)
    kc = cmp[:, :NSA_GROUPS].transpose(0, 2, 1, 3).reshape(B, nb16, NSA_KVW).astype(BF16)
    vct = cmp[:, NSA_GROUPS:].transpose(0, 1, 3, 2).reshape(B, NSA_KVW, nb16).astype(BF16)

    def key_major_tiles(v, w):
        return v.reshape(B, S // w, w, LANES).swapaxes(-1, -2)

    o_nsa = _nsa_attention(q, kc, vct, jnp.asarray(_overlap_matrix(S).T), kvs,
                           key_major_tiles(kvs[:, :, LANES:2 * LANES], min(8 * LANES, S)),
                           key_major_tiles(kvs[:, :, 3 * LANES:], LANES), small, n_sel)

    gates = small[:, :, 3 * NSA_HEADS:3 * NSA_HEADS + 2 * ML_HEADS].transpose(0, 2, 1)
    gate_row = gates.reshape(B, 2 * ML_HEADS, S // ML_CHUNK, ML_CHUNK)
    gate_col = gates.reshape(B, 2 * ML_HEADS, S, 1)
    o_ml = _mlstm(ml_qk, ml_v, ml_o, gate_row, gate_col, gate_b, conv_w, conv_b.reshape(1, -1),
                  ml_norm_g.reshape(1, -1))

    h1, n2, ti, tg = _out_projection(o_nsa, o_ml, merge, h, gt1, sc2, sh2, norm2_g.reshape(1, D),
                                     w_up_nsa.astype(BF16), w_up_ml.astype(BF16), w_out.astype(BF16),
                                     router_w, router_b.reshape(1, N_EXPERTS))

    A = T * TOP_K
    n_blk = -(-(A + N_EXPERTS * (MOE_ROWS - 1)) // MOE_ROWS)
    P = n_blk * MOE_ROWS
    ti2 = ti.reshape(T, LANES)
    rank, cnt = _expert_rank(ti2)
    dest, blk_e = _expert_dest(cnt, ti2, rank, n_blk)
    dest_flat = dest[:, :TOP_K].reshape(A)
    gate_flat = tg.reshape(T, LANES)[:, :TOP_K].reshape(A)
    x_rows = _scatter_rows(dest_flat, n2.reshape(T * SUBLANES, LANES), jnp.zeros((P * SUBLANES, LANES), F32))
    y_rows = _expert_ffn(blk_e[:, 0], blk_e[:, 1], x_rows, exp_w1, exp_b1.reshape(N_EXPERTS, 1, -1),
                         exp_w2, exp_b2.reshape(N_EXPERTS, 1, -1))
    y = _combine_rows(dest_flat, gate_flat, y_rows, T)
    return h1, y.reshape(B, S * SUBLANES, LANES), gt2


def kernel(x, c, ada_w, ada_b, norm1_g, w_in, nsa_pe_k, nsa_pe_v, nsa_cmp_w1_k, nsa_cmp_w2_k, nsa_cmp_w1_v, nsa_cmp_w2_v, ml_conv_w, ml_conv_b, ml_gate_b, ml_norm_g, w_up_nsa, w_up_ml, w_out, norm2_g, router_w, router_b, exp_w1, exp_b1, exp_w2, exp_b2, final_g):
    B, S, D = x.shape
    depth = ada_w.shape[0]
    h = x
    for layer in range(depth):
        mod = _modulation(c, ada_w[layer], ada_b[layer])
        mods = [mod[:, i * D:(i + 1) * D].reshape(B, 1, D) for i in range(6)]
        h1, y, gt2 = _layer(h, mods, norm1_g[layer], w_in[layer], nsa_pe_k[layer], nsa_pe_v[layer],
                            nsa_cmp_w1_k[layer], nsa_cmp_w2_k[layer], nsa_cmp_w1_v[layer], nsa_cmp_w2_v[layer],
                            ml_conv_w[layer], ml_conv_b[layer], ml_gate_b[layer], ml_norm_g[layer],
                            w_up_nsa[layer], w_up_ml[layer], w_out[layer], norm2_g[layer], router_w[layer],
                            router_b[layer], exp_w1[layer], exp_b1[layer], exp_w2[layer], exp_b2[layer])
        h = _residual(h1, y, gt2, final_g.reshape(1, D), normalize=(layer + 1 == depth))
    return h
```

```python
import functools

import numpy as np
import jax
import jax.numpy as jnp
from jax import lax
from jax.experimental import pallas as pl
from jax.experimental.pallas import tpu as pltpu

F32 = jnp.float32
BF16 = jnp.bfloat16
I32 = jnp.int32
HIGHEST = lax.Precision.HIGHEST

D_MODEL = 1024
NSA_HEADS = 8
NSA_GROUPS = 2
NSA_REP = NSA_HEADS // NSA_GROUPS
NSA_DH = 64
NSA_SCALE = NSA_DH ** -0.5
Q_PRESCALE = NSA_SCALE * float(np.log2(np.e))
ONES_ROWS = 16
CMP_STRIDE = 16
CMP_LEN = 32
CMP_HIDDEN = 128
SEL_BLOCK = 64
SEL_TOPN = 16
WINDOW = 512
Q_BLOCK = 128
ML_HEADS = 4
ML_DH = 128
ML_CHUNK = 64
CONV_WIDTH = 4
N_EXPERTS = 32
TOP_K = 4
D_EXPERT = 1024
SWIGLU_LIMIT = 7.0
SWIGLU_ALPHA = 1.702
MOE_ROWS = 256
NORM_EPS = 1e-6
NEG_INF = -1e30
FORCE_BONUS = 1e4

NSA_QW = NSA_HEADS * NSA_DH
NSA_KVW = NSA_GROUPS * NSA_DH
ML_W = ML_HEADS * ML_DH
LANES = 128
SUBLANES = 8
VMEM_LIMIT = 52 * 1024 * 1024

_C_Q = (0, 512)
_C_KVC = (512, 768)
_C_KVS = (768, 1280)
_C_MQK = (1280, 2304)
_C_MV = (2304, 2816)
_C_MO = (2816, 3328)
_C_MG = (3328, 5376)
_C_SM = (5376, 5504)
IN_WR = 5504


def _sigmoid(x):
    return 1.0 / (1.0 + jnp.exp(-x))


def _iota(shape, dim):
    return lax.broadcasted_iota(I32, shape, dim)


def _cparams(sem, vmem=None):
    return pltpu.CompilerParams(dimension_semantics=sem, vmem_limit_bytes=vmem)


def _tiles_to_rows(ref, n, lead=()):
    return jnp.concatenate([ref[lead + (pl.ds(s, n, stride=SUBLANES), slice(None))] for s in range(SUBLANES)],
                           axis=1)


def _rows_to_tiles(ref, x, lead=()):
    n = x.shape[0]
    for s in range(SUBLANES):
        ref[lead + (pl.ds(s, n, stride=SUBLANES), slice(None))] = x[:, s * LANES:(s + 1) * LANES]


def _mod_kernel(c_ref, w_ref, b_ref, o_ref):
    c = c_ref[...]
    ca = c * _sigmoid(c)
    o_ref[...] = jnp.dot(ca, w_ref[...], precision=HIGHEST, preferred_element_type=F32) + b_ref[...]


def _modulation(c, ada_w, ada_b):
    B, D = c.shape
    n = ada_w.shape[1] // D
    return pl.pallas_call(
        _mod_kernel,
        out_shape=jax.ShapeDtypeStruct((B, n * D), F32),
        grid=(n,),
        in_specs=[pl.BlockSpec((B, D), lambda j: (0, 0)),
                  pl.BlockSpec((D, D), lambda j: (0, j)),
                  pl.BlockSpec((1, D), lambda j: (0, j))],
        out_specs=pl.BlockSpec((B, D), lambda j: (0, j)),
        compiler_params=_cparams(("parallel",)),
        name="modulation",
    )(c, ada_w, ada_b.reshape(1, n * D))


def _inproj_kernel(x_ref, g_ref, sc_ref, sh_ref, w_ref,
                   q_ref, kvc_ref, kvs_ref, mqk_ref, mv_ref, mo_ref, mg_ref, sm_ref):
    x = x_ref[0]
    y = x * lax.rsqrt(jnp.mean(x * x, axis=-1, keepdims=True) + NORM_EPS) * g_ref[...]
    n = (y * (1.0 + sc_ref[0]) + sh_ref[0]).astype(BF16)

    def proj(c):
        return jnp.dot(n, w_ref[:, c[0]:c[1]], preferred_element_type=F32)

    q_ref[0] = (proj(_C_Q) * Q_PRESCALE).astype(BF16)
    kvc_ref[0] = proj(_C_KVC)
    kvs_ref[0] = proj(_C_KVS).astype(BF16)
    mqk_ref[0] = proj(_C_MQK)
    mv_ref[0] = proj(_C_MV)
    mo_ref[0] = proj(_C_MO)
    mg_ref[0] = proj(_C_MG)
    sm_ref[0] = proj(_C_SM)


def _in_projection(x, g, sc, sh, w_r, tm=256):
    B, S, D = x.shape
    widths = [c[1] - c[0] for c in (_C_Q, _C_KVC, _C_KVS, _C_MQK, _C_MV, _C_MO, _C_MG, _C_SM)]
    dtypes = [BF16, F32, BF16, F32, F32, F32, F32, F32]
    vec = pl.BlockSpec((1, 1, D), lambda b, i: (b, 0, 0))
    return pl.pallas_call(
        _inproj_kernel,
        out_shape=[jax.ShapeDtypeStruct((B, S, w), dt) for w, dt in zip(widths, dtypes)],
        grid=(B, S // tm),
        in_specs=[pl.BlockSpec((1, tm, D), lambda b, i: (b, i, 0)),
                  pl.BlockSpec((1, D), lambda b, i: (0, 0)),
                  vec, vec,
                  pl.BlockSpec((D, IN_WR), lambda b, i: (0, 0))],
        out_specs=[pl.BlockSpec((1, tm, w), lambda b, i: (b, i, 0)) for w in widths],
        compiler_params=_cparams(("parallel", "parallel"), VMEM_LIMIT),
        name="in_projection",
    )(x, g, sc, sh, w_r)


def _compress_kernel(t_ref, pe_ref, w1_ref, w2_ref, o_ref):
    half = CMP_STRIDE * NSA_DH
    t = t_ref[0, 0]
    pe = pe_ref[0]
    nb = t.shape[0]
    a = jnp.dot((t + pe[0:1]).astype(BF16), w1_ref[0, :half, :].astype(BF16), preferred_element_type=F32)
    b = jnp.dot((t + pe[1:2]).astype(BF16), w1_ref[0, half:, :].astype(BF16), preferred_element_type=F32)
    h = a + pltpu.roll(b, nb - 1, 0)
    gl = 0.5 * h * (1.0 + jnp.tanh(np.sqrt(2.0 / np.pi).astype(np.float32) * (h + 0.044715 * (h * h * h))))
    o_ref[0, 0] = jnp.dot(gl.astype(BF16), w2_ref[0].astype(BF16), preferred_element_type=F32)


def _compress(t16, pe, w1, w2):
    B, _, nb, wdt = t16.shape
    return pl.pallas_call(
        _compress_kernel,
        out_shape=jax.ShapeDtypeStruct((B, 4, nb, NSA_DH), F32),
        grid=(B, 4),
        in_specs=[pl.BlockSpec((1, 1, nb, wdt), lambda b, j: (b, j, 0, 0)),
                  pl.BlockSpec((1, 2, wdt), lambda b, j: (j // 2, 0, 0)),
                  pl.BlockSpec((1, CMP_LEN * NSA_DH, CMP_HIDDEN), lambda b, j: (j // 2, 0, 0)),
                  pl.BlockSpec((1, CMP_HIDDEN, NSA_DH), lambda b, j: (j // 2, 0, 0))],
        out_specs=pl.BlockSpec((1, 1, nb, NSA_DH), lambda b, j: (b, j, 0, 0)),
        compiler_params=_cparams(("parallel", "parallel")),
        name="nsa_compress",
    )(t16, pe, w1, w2)


def _nsa_kernel(q_ref, kc_ref, vct_ref, ovt_ref, ks_ref, vst_ref, kw_ref, vwt_ref, sm_ref, o_ref, *, n_sel, tk):
    qb = Q_BLOCK
    ncp = kc_ref.shape[1]
    ns = ovt_ref.shape[0]
    q0 = pl.program_id(1) * qb
    t_l = q0 + _iota((1, qb), 1)
    sub = _iota((LANES, qb), 0)
    gT = _sigmoid(sm_ref[0].T)

    gw = NSA_REP * qb
    q_heads = []
    for c in range(NSA_HEADS // 2):
        qc = q_ref[0, :, c * LANES:(c + 1) * LANES].astype(F32).T
        for e in range(2):
            g = (2 * c + e) // NSA_REP
            x = qc if e == g else pltpu.roll(qc, NSA_DH, 0)
            q_heads.append(jnp.where((sub >> 6) == g, x, 0.0).astype(BF16))
    qT = [jnp.concatenate(q_heads[g * NSA_REP:(g + 1) * NSA_REP], axis=1) for g in range(NSA_GROUPS)]
    t_g = q0 + (_iota((1, gw), 1) & (qb - 1))

    def head_cols(x, r):
        return x[:, r * qb:(r + 1) * qb]

    def gate_row(branch, g):
        rows = [gT[branch * NSA_HEADS + g * NSA_REP + r:branch * NSA_HEADS + g * NSA_REP + r + 1]
                for r in range(NSA_REP)]
        return jnp.concatenate(rows, axis=1)

    kc = kc_ref[0]
    vct = vct_ref[0]
    cmask = (_iota((ncp, gw), 0) * CMP_STRIDE + (CMP_LEN - 1)) <= t_g
    blk = _iota((ns, qb), 0)
    blk_f = blk.astype(F32)
    cur = t_l >> 6
    forced = (blk == 0) | (blk == cur) | (blk == cur - 1)
    ocT = []
    selT = []
    for g in range(NSA_GROUPS):
        s = jnp.dot(kc, qT[g], preferred_element_type=F32)
        s = jnp.where(cmask, s, NEG_INF)
        e = jnp.exp2(s - jnp.max(s, axis=0, keepdims=True))
        p = e * (1.0 / jnp.sum(e, axis=0, keepdims=True))
        p = jnp.where(cmask, p, 0.0)
        ocT.append(jnp.dot(vct, p.astype(BF16), preferred_element_type=F32))
        psum = head_cols(p, 0)
        for r in range(1, NSA_REP):
            psum = psum + head_cols(p, r)
        imp = jnp.dot(ovt_ref[...], psum, precision=HIGHEST, preferred_element_type=F32)
        imp = jnp.where(blk > cur, NEG_INF, imp + jnp.where(forced, FORCE_BONUS, 0.0))
        sel = jnp.zeros((ns, qb), F32)
        for _ in range(n_sel):
            mx = jnp.max(imp, axis=0, keepdims=True)
            first = jnp.min(jnp.where(imp == mx, blk_f, float(ns)), axis=0, keepdims=True)
            pick = blk_f == first
            sel = jnp.where(pick, 1.0, sel)
            imp = jnp.where(pick, -jnp.inf, imp)
        selT.append(sel.astype(BF16))

    wlen = WINDOW + qb
    wt0 = jnp.maximum(q0 - WINDOW, 0) // LANES
    wstart = pl.multiple_of(wt0 * LANES, LANES)
    kwin = kw_ref[0, pl.ds(wstart, wlen), :]
    rel = t_g - (wstart + _iota((wlen, gw), 0))
    wmask = lax.bitcast_convert_type(rel, jnp.uint32) < WINDOW
    partial = []
    for g in range(NSA_GROUPS):
        lo = g * NSA_DH
        s = jnp.dot(kwin, qT[g], preferred_element_type=F32)
        s = jnp.where(wmask, s, NEG_INF)
        pb = jnp.exp2(s - jnp.max(s, axis=0, keepdims=True)).astype(BF16)
        ow = jnp.dot(vwt_ref[0, wt0], pb[:LANES], preferred_element_type=F32)
        for i in range(1, wlen // LANES):
            ow = ow + jnp.dot(vwt_ref[0, wt0 + i], pb[i * LANES:(i + 1) * LANES], preferred_element_type=F32)
        ow = ow[:LANES] * (1.0 / ow[LANES:LANES + 1])
        partial.append(gate_row(0, g) * ocT[g][lo:lo + NSA_DH] + gate_row(2, g) * ow[lo:lo + NSA_DH])

    nt = (q0 + qb + tk - 1) // tk

    def make_step(diagonal):
        def step(j, carry):
            r0 = pl.multiple_of(j * tk, tk)
            kt = ks_ref[0, pl.ds(r0, tk), :]
            expand = jnp.where(((r0 + _iota((tk, ns), 0)) >> 6) == _iota((tk, ns), 1), 1.0, 0.0).astype(BF16)
            new = []
            for g in range(NSA_GROUPS):
                m, l, acc = carry[g]
                hit = jnp.dot(expand, selT[g], preferred_element_type=F32)
                if diagonal:
                    hit = jnp.where((r0 + _iota((tk, qb), 0)) <= t_l, hit, 0.0)
                ok = hit > 0.5
                s = jnp.dot(kt, qT[g], preferred_element_type=F32)
                s = jnp.concatenate([jnp.where(ok, head_cols(s, r), NEG_INF) for r in range(NSA_REP)], axis=1)
                m_new = jnp.maximum(m, jnp.max(s, axis=0, keepdims=True))
                alpha = jnp.exp2(m - m_new)
                pv = jnp.dot(vst_ref[0, j], jnp.exp2(s - m_new).astype(BF16), preferred_element_type=F32)
                new.append((m_new, alpha * l + pv[LANES:LANES + 1], alpha * acc + pv[:LANES]))
            return tuple(new)
        return step

    init = tuple((jnp.full((1, gw), NEG_INF, F32), jnp.zeros((1, gw), F32), jnp.zeros((LANES, gw), F32))
                 for _ in range(NSA_GROUPS))
    final = lax.fori_loop(0, nt - 1, make_step(False), init)
    final = make_step(True)(nt - 1, final)

    heads = []
    for g in range(NSA_GROUPS):
        _, l, acc = final[g]
        o_g = partial[g] + gate_row(1, g) * (acc[g * NSA_DH:(g + 1) * NSA_DH] * (1.0 / l))
        heads += [head_cols(o_g, r) for r in range(NSA_REP)]
    for c in range(NSA_HEADS // 2):
        o_ref[0, :, c * LANES:(c + 1) * LANES] = jnp.concatenate(heads[2 * c:2 * c + 2], axis=0).T


def _nsa_attention(q, kc, vct, ovt, kvs, vst, vwt, small, n_sel):
    B, S, _ = q.shape
    ns, ncp = ovt.shape
    tk = vst.shape[3]
    assert S >= WINDOW + Q_BLOCK and S % tk == 0 and tk % LANES == 0
    nt128 = S // LANES

    def whole(shape):
        return pl.BlockSpec((1,) + shape, lambda b, i: (b,) + (0,) * len(shape))

    return pl.pallas_call(
        functools.partial(_nsa_kernel, n_sel=n_sel, tk=tk),
        out_shape=jax.ShapeDtypeStruct((B, S, NSA_QW), F32),
        grid=(B, S // Q_BLOCK),
        in_specs=[pl.BlockSpec((1, Q_BLOCK, NSA_QW), lambda b, i: (b, i, 0)),
                  whole((ncp, LANES)), whole((LANES, ncp)),
                  pl.BlockSpec((ns, ncp), lambda b, i: (0, 0)),
                  pl.BlockSpec((1, S, LANES), lambda b, i: (b, 0, 0)),
                  whole((S // tk, LANES + ONES_ROWS, tk)),
                  pl.BlockSpec((1, S, LANES), lambda b, i: (b, 0, 2)),
                  whole((nt128, LANES + ONES_ROWS, LANES)),
                  pl.BlockSpec((1, Q_BLOCK, LANES), lambda b, i: (b, i, 0))],
        out_specs=pl.BlockSpec((1, Q_BLOCK, NSA_QW), lambda b, i: (b, i, 0)),
        compiler_params=_cparams(("parallel", "parallel"), VMEM_LIMIT),
        name="nsa_attention",
    )(q, kc, vct, ovt, kvs, vst, kvs, vwt, small)


def _mlstm_kernel(gb_ref, q_ref, k_ref, v_ref, o_ref, grow_ref, gcol_ref, cw_ref, cb_ref, ng_ref, out_ref,
                  c_scr, n_scr, m_scr, hq_scr, hk_scr):
    L = ML_CHUNK
    H = ML_HEADS
    ts = q_ref.shape[1]
    kscale = ML_DH ** -0.5
    row = _iota((L, L), 0)
    col = _iota((L, L), 1)
    tri = row >= col
    halo_rows = SUBLANES

    @pl.when(pl.program_id(1) == 0)
    def _():
        c_scr[...] = jnp.zeros_like(c_scr)
        n_scr[...] = jnp.zeros_like(n_scr)
        m_scr[...] = jnp.zeros_like(m_scr)
        hq_scr[...] = jnp.zeros_like(hq_scr)
        hk_scr[...] = jnp.zeros_like(hk_scr)

    def conv_silu(x_ref, prev_ref, woff, h, c, r0):
        lanes = slice(h * ML_DH, (h + 1) * ML_DH)
        wl = slice(woff + h * ML_DH, woff + (h + 1) * ML_DH)
        main = x_ref[0, pl.ds(r0, L), lanes]
        h0 = pl.multiple_of(jnp.maximum(r0 - halo_rows, 0), halo_rows)
        halo = jnp.where(c > 0, x_ref[0, pl.ds(h0, halo_rows), lanes], prev_ref[:, lanes])
        cat = jnp.concatenate([halo, main], axis=0)
        y = main * cw_ref[CONV_WIDTH - 1:CONV_WIDTH, wl] + cb_ref[:, wl]
        for w in range(CONV_WIDTH - 1):
            sft = CONV_WIDTH - 1 - w
            y = y + pltpu.roll(cat, sft, 0)[halo_rows:, :] * cw_ref[w:w + 1, wl]
        return y * _sigmoid(y)

    def logsig(x):
        return -(jnp.maximum(-x, 0.0) + jnp.log(1.0 + jnp.exp(-jnp.abs(x))))

    def head_step(h, c, r0, C, n, m):
        lanes = slice(h * ML_DH, (h + 1) * ML_DH)
        gb_i = gb_ref[h]
        gb_f = gb_ref[H + h]
        qc = conv_silu(q_ref, hq_scr, 0, h, c, r0)
        kc = conv_silu(k_ref, hk_scr, ML_W, h, c, r0) * kscale
        vc = v_ref[0, pl.ds(r0, L), lanes]
        li_row = grow_ref[0, h, pl.ds(c, 1), :] + gb_i
        lf_row = logsig(grow_ref[0, H + h, pl.ds(c, 1), :] + gb_f)
        li_col = gcol_ref[0, h, pl.ds(r0, L), :] + gb_i
        lf_col = logsig(gcol_ref[0, H + h, pl.ds(r0, L), :] + gb_f)
        b_col = jnp.sum(jnp.where(tri, lf_row, 0.0), axis=1, keepdims=True)
        b_row = jnp.sum(jnp.where(row <= col, lf_col, 0.0), axis=0, keepdims=True)
        Dm = jnp.where(tri, b_col - b_row + li_row, NEG_INF)
        inter = b_col + m
        m_t = jnp.maximum(inter, jnp.max(Dm, axis=1, keepdims=True))
        Dw = jnp.exp(Dm - m_t)
        inter_w = jnp.exp(inter - m_t)
        qb16 = qc.astype(BF16)
        kb16 = kc.astype(BF16)
        vb16 = vc.astype(BF16)
        qk = lax.dot_general(qb16, kb16, (((1,), (1,)), ((), ())), preferred_element_type=F32) * Dw
        num = (jnp.dot(qk.astype(BF16), vb16, preferred_element_type=F32)
               + inter_w * jnp.dot(qb16, C.astype(BF16), preferred_element_type=F32))
        den = jnp.sum(qk, axis=1, keepdims=True) + inter_w * jnp.sum(qc * n, axis=1, keepdims=True)
        hc = num / jnp.maximum(jnp.abs(den), jnp.exp(-m_t))
        bL = b_col[L - 1:L, :]
        a_col = bL - b_col + li_col
        m_new = jnp.maximum(bL + m, jnp.max(a_col, axis=0, keepdims=True))
        aw = jnp.exp(a_col - m_new)
        decay = jnp.exp(bL + m - m_new)
        awk = aw * kc
        C_new = decay * C + jnp.dot(awk.T.astype(BF16), vb16, preferred_element_type=F32)
        n_new = decay * n + jnp.sum(awk, axis=0, keepdims=True)
        hn = hc * lax.rsqrt(jnp.mean(hc * hc, axis=1, keepdims=True) + NORM_EPS) * ng_ref[:, lanes]
        out_ref[0, pl.ds(r0, L), lanes] = hn * _sigmoid(o_ref[0, pl.ds(r0, L), lanes])
        return C_new, n_new, m_new

    def step(c, carry):
        r0 = pl.multiple_of(c * L, L)
        return tuple(head_step(h, c, r0, *carry[h]) for h in range(H))

    init = tuple((c_scr[h], n_scr[h], m_scr[h][:, 0:1]) for h in range(H))
    final = lax.fori_loop(0, ts // L, step, init)
    for h in range(H):
        C, n, m = final[h]
        c_scr[h] = C
        n_scr[h] = n
        m_scr[h] = jnp.broadcast_to(m, (1, LANES))
    hq_scr[...] = q_ref[0, ts - halo_rows:ts, :]
    hk_scr[...] = k_ref[0, ts - halo_rows:ts, :]


def _mlstm(ml_qk, ml_v, ml_o, gate_row, gate_col, gate_b, conv_w, conv_b, norm_g, ts=1024):
    B, S, _ = ml_v.shape
    H = ML_HEADS
    ts = min(ts, S)

    def rows(cb):
        return pl.BlockSpec((1, ts, ML_W), lambda b, i, gb: (b, i, cb))

    def full(shape):
        return pl.BlockSpec(shape, lambda b, i, gb: (0,) * len(shape))

    grid_spec = pltpu.PrefetchScalarGridSpec(
        num_scalar_prefetch=1,
        grid=(B, S // ts),
        in_specs=[rows(0), rows(1), rows(0), rows(0),
                  pl.BlockSpec((1, 2 * H, ts // ML_CHUNK, ML_CHUNK), lambda b, i, gb: (b, 0, i, 0)),
                  pl.BlockSpec((1, 2 * H, ts, 1), lambda b, i, gb: (b, 0, i, 0)),
                  full((CONV_WIDTH, 2 * ML_W)), full((1, 2 * ML_W)), full((1, ML_W))],
        out_specs=rows(0),
        scratch_shapes=[pltpu.VMEM((H, ML_DH, ML_DH), F32), pltpu.VMEM((H, 1, ML_DH), F32),
                        pltpu.VMEM((H, 1, LANES), F32),
                        pltpu.VMEM((SUBLANES, ML_W), F32), pltpu.VMEM((SUBLANES, ML_W), F32)],
    )
    return pl.pallas_call(
        _mlstm_kernel,
        out_shape=jax.ShapeDtypeStruct((B, S, ML_W), F32),
        grid_spec=grid_spec,
        compiler_params=_cparams(("parallel", "arbitrary"), VMEM_LIMIT),
        name="mlstm",
    )(gate_b, ml_qk, ml_qk, ml_v, ml_o, gate_row, gate_col, conv_w, conv_b, norm_g)


def _pack_cols(cols, dtype):
    tm = cols[0].shape[0]
    lane = _iota((tm, LANES), 1)
    out = jnp.zeros((tm, LANES), dtype)
    for k, c in enumerate(cols):
        out = jnp.where(lane == k, c, out)
    return out


def _outproj_kernel(a_ref, b_ref, mg_ref, x_ref, gt_ref, sc_ref, sh_ref, g2_ref,
                    wa_ref, wb_ref, wo_ref, wr_ref, br_ref,
                    h_ref, n_ref, ti_ref, tg_ref):
    D = x_ref.shape[2]
    ua = jnp.dot(a_ref[0].astype(BF16), wa_ref[...], preferred_element_type=F32)
    ub = jnp.dot(b_ref[0].astype(BF16), wb_ref[...], preferred_element_type=F32)
    u = _sigmoid(mg_ref[0, :, :D]) * ua + _sigmoid(mg_ref[0, :, D:]) * ub
    mix = jnp.dot(u.astype(BF16), wo_ref[...], preferred_element_type=F32)
    h = x_ref[0] + (1.0 + gt_ref[0]) * mix
    h_ref[0] = h
    y = h * lax.rsqrt(jnp.mean(h * h, axis=-1, keepdims=True) + NORM_EPS) * g2_ref[...]
    n = y * (1.0 + sc_ref[0]) + sh_ref[0]
    _rows_to_tiles(n_ref, n, lead=(0,))
    logits = jnp.dot(n, wr_ref[...], precision=HIGHEST, preferred_element_type=F32) + br_ref[...]
    tm, ne = logits.shape
    lane = _iota((tm, ne), 1).astype(F32)
    work = logits
    vals, idxs = [], []
    for _ in range(TOP_K):
        mx = jnp.max(work, axis=-1, keepdims=True)
        ix = jnp.min(jnp.where(work == mx, lane, float(ne)), axis=-1, keepdims=True)
        vals.append(mx)
        idxs.append(ix)
        work = jnp.where(lane == ix, -jnp.inf, work)
    es = [jnp.exp(v - vals[0]) for v in vals]
    tot = es[0] + es[1] + es[2] + es[3]
    ti_ref[0] = _pack_cols(idxs, F32).astype(I32)
    tg_ref[0] = _pack_cols([e / tot for e in es], F32)


def _out_projection(o_nsa, o_ml, merge, x, gt1, sc2, sh2, g2, wa, wb, wo, wr, br, tm=256):
    B, S, D = x.shape
    vec = pl.BlockSpec((1, 1, D), lambda b, i: (b, 0, 0))

    def full(shape):
        return pl.BlockSpec(shape, lambda b, i: (0,) * len(shape))

    def rows(w):
        return pl.BlockSpec((1, tm, w), lambda b, i: (b, i, 0))

    assert D == SUBLANES * LANES
    return pl.pallas_call(
        _outproj_kernel,
        out_shape=[jax.ShapeDtypeStruct((B, S, D), F32), jax.ShapeDtypeStruct((B, S * SUBLANES, LANES), F32),
                   jax.ShapeDtypeStruct((B, S, LANES), I32), jax.ShapeDtypeStruct((B, S, LANES), F32)],
        grid=(B, S // tm),
        in_specs=[rows(NSA_QW), rows(ML_W), rows(2 * D), rows(D), vec, vec, vec, full((1, D)),
                  full(wa.shape), full(wb.shape), full(wo.shape), full(wr.shape), full((1, N_EXPERTS))],
        out_specs=[rows(D), pl.BlockSpec((1, tm * SUBLANES, LANES), lambda b, i: (b, i, 0)), rows(LANES),
                   rows(LANES)],
        compiler_params=_cparams(("parallel", "parallel"), VMEM_LIMIT),
        name="out_projection_router",
    )(o_nsa, o_ml, merge, x, gt1, sc2, sh2, g2, wa, wb, wo, wr, br)


def _rank_kernel(ti_ref, rank_ref, cnt_ref, carry_ref):
    tt = ti_ref.shape[0]

    @pl.when(pl.program_id(0) == 0)
    def _():
        carry_ref[...] = jnp.zeros_like(carry_ref)

    ti = ti_ref[...]
    lane = _iota((tt, LANES), 1)
    hots = [ti[:, k:k + 1] == lane for k in range(TOP_K)]
    member = jnp.zeros((tt, LANES), F32)
    for hot in hots:
        member = member + jnp.where(hot, 1.0, 0.0)
    strict = jnp.where(_iota((tt, tt), 0) > _iota((tt, tt), 1), 1.0, 0.0).astype(BF16)
    before = jnp.dot(strict, member.astype(BF16), preferred_element_type=F32) + carry_ref[0:1, :]
    ranks = [jnp.sum(jnp.where(hot, before, 0.0), axis=-1, keepdims=True) for hot in hots]
    rank_ref[...] = _pack_cols(ranks, F32)
    carry_ref[...] = carry_ref[...] + jnp.sum(member, axis=0, keepdims=True)
    cnt_ref[...] = carry_ref[...]


def _expert_rank(ti, tt=512):
    T = ti.shape[0]
    return pl.pallas_call(
        _rank_kernel,
        out_shape=[jax.ShapeDtypeStruct((T, LANES), F32), jax.ShapeDtypeStruct((SUBLANES, LANES), F32)],
        grid=(T // tt,),
        in_specs=[pl.BlockSpec((tt, LANES), lambda i: (i, 0))],
        out_specs=[pl.BlockSpec((tt, LANES), lambda i: (i, 0)),
                   pl.BlockSpec((SUBLANES, LANES), lambda i: (0, 0))],
        scratch_shapes=[pltpu.VMEM((SUBLANES, LANES), F32)],
        compiler_params=_cparams(("arbitrary",)),
        name="moe_rank",
    )(ti)


def _dest_kernel(cnt_ref, ti_ref, rank_ref, dest_ref, be_ref):
    tt = ti_ref.shape[0]
    nbp = be_ref.shape[0]
    cnt = cnt_ref[...].astype(I32)
    padded = ((cnt + (MOE_ROWS - 1)) >> 8) << 8
    lane8 = _iota((SUBLANES, LANES), 1)
    pad_end = padded
    sft = 1
    while sft < LANES:
        pad_end = pad_end + jnp.where(lane8 >= sft, pltpu.roll(pad_end, sft, 1), 0)
        sft *= 2
    pad_start = (pad_end - padded)[0:1, :].astype(F32)
    ti = ti_ref[...]
    lane = _iota((tt, LANES), 1)
    rank = rank_ref[...]
    dests = [jnp.sum(jnp.where(ti[:, k:k + 1] == lane, pad_start, 0.0), axis=-1, keepdims=True) + rank[:, k:k + 1]
             for k in range(TOP_K)]
    dest_ref[...] = _pack_cols(dests, F32).astype(I32) * SUBLANES
    first_row = _iota((nbp, LANES), 0) * MOE_ROWS
    done = (pad_end[0:1, :] <= first_row) & (_iota((nbp, LANES), 1) < N_EXPERTS)
    be = jnp.minimum(jnp.sum(jnp.where(done, 1.0, 0.0), axis=-1, keepdims=True), float(N_EXPERTS - 1))
    used = first_row < pad_end[0:1, N_EXPERTS - 1:N_EXPERTS]
    be_ref[...] = jnp.where(_iota((nbp, LANES), 1) == 1, jnp.where(used, 1, 0), be.astype(I32))


def _expert_dest(cnt, ti, rank, n_blk, tt=1024):
    T = ti.shape[0]
    return pl.pallas_call(
        _dest_kernel,
        out_shape=[jax.ShapeDtypeStruct((T, LANES), I32), jax.ShapeDtypeStruct((n_blk, LANES), I32)],
        grid=(T // tt,),
        in_specs=[pl.BlockSpec((SUBLANES, LANES), lambda i: (0, 0)),
                  pl.BlockSpec((tt, LANES), lambda i: (i, 0)),
                  pl.BlockSpec((tt, LANES), lambda i: (i, 0))],
        out_specs=[pl.BlockSpec((tt, LANES), lambda i: (i, 0)),
                   pl.BlockSpec((n_blk, LANES), lambda i: (0, 0))],
        compiler_params=_cparams(("arbitrary",)),
        name="moe_dest",
    )(cnt, ti, rank)


def _scatter_kernel(dest_ref, src_ref, init_ref, dst_ref, sem):
    del init_ref
    tt = src_ref.shape[0] // SUBLANES

    def copy(t, k):
        return pltpu.make_async_copy(
            src_ref.at[pl.ds(pl.multiple_of(t * SUBLANES, SUBLANES), SUBLANES)],
            dst_ref.at[pl.ds(pl.multiple_of(dest_ref[t * TOP_K + k], SUBLANES), SUBLANES)], sem)

    def issue(t, c):
        for k in range(TOP_K):
            copy(t, k).start()
        return c

    lax.fori_loop(0, tt, issue, 0)

    def drain(t, c):
        for k in range(TOP_K):
            copy(t, k).wait()
        return c

    lax.fori_loop(0, tt, drain, 0)


def _scatter_rows(dest_flat, rows, init, tt=512):
    T = rows.shape[0] // SUBLANES
    return pl.pallas_call(
        _scatter_kernel,
        out_shape=jax.ShapeDtypeStruct(init.shape, init.dtype),
        grid=(T // tt,),
        in_specs=[pl.BlockSpec((tt * TOP_K,), lambda i: (i,), memory_space=pltpu.SMEM),
                  pl.BlockSpec((tt * SUBLANES, LANES), lambda i: (i, 0)),
                  pl.BlockSpec(memory_space=pl.ANY)],
        out_specs=pl.BlockSpec(memory_space=pl.ANY),
        scratch_shapes=[pltpu.SemaphoreType.DMA],
        input_output_aliases={2: 0},
        compiler_params=_cparams(("arbitrary",)),
        name="moe_scatter_rows",
    )(dest_flat, rows, init)


def _expert_kernel(be_ref, used_ref, x_ref, w1_ref, b1_ref, w2_ref, b2_ref, y_ref, w1b_ref, w2b_ref):
    i = pl.program_id(0)

    @pl.when(jnp.logical_or(i == 0, be_ref[i] != be_ref[jnp.maximum(i - 1, 0)]))
    def _():
        w1b_ref[...] = w1_ref[0].astype(BF16)
        w2b_ref[...] = w2_ref[0].astype(BF16)

    @pl.when(used_ref[i] != 0)
    def _():
        x = _tiles_to_rows(x_ref, MOE_ROWS).astype(BF16)
        h = jnp.dot(x, w1b_ref[...], preferred_element_type=F32) + b1_ref[0]
        glu = jnp.minimum(h[:, :D_EXPERT], SWIGLU_LIMIT)
        lin = jnp.clip(h[:, D_EXPERT:], -SWIGLU_LIMIT, SWIGLU_LIMIT)
        act = glu * _sigmoid(SWIGLU_ALPHA * glu) * (lin + 1.0)
        y = jnp.dot(act.astype(BF16), w2b_ref[...], preferred_element_type=F32) + b2_ref[0]
        _rows_to_tiles(y_ref, y)

    @pl.when(used_ref[i] == 0)
    def _():
        y_ref[...] = jnp.zeros_like(y_ref)


def _expert_ffn(blk_expert, blk_used, x_rows, w1, b1, w2, b2):
    P = x_rows.shape[0] // SUBLANES
    D = SUBLANES * LANES
    nb = P // MOE_ROWS
    tile_rows = pl.BlockSpec((MOE_ROWS * SUBLANES, LANES), lambda i, be, us: (i, 0))
    grid_spec = pltpu.PrefetchScalarGridSpec(
        num_scalar_prefetch=2,
        grid=(nb,),
        in_specs=[tile_rows,
                  pl.BlockSpec((1, D, 2 * D_EXPERT), lambda i, be, us: (be[i], 0, 0)),
                  pl.BlockSpec((1, 1, 2 * D_EXPERT), lambda i, be, us: (be[i], 0, 0)),
                  pl.BlockSpec((1, D_EXPERT, D), lambda i, be, us: (be[i], 0, 0)),
                  pl.BlockSpec((1, 1, D), lambda i, be, us: (be[i], 0, 0))],
        out_specs=tile_rows,
        scratch_shapes=[pltpu.VMEM((D, 2 * D_EXPERT), BF16), pltpu.VMEM((D_EXPERT, D), BF16)],
    )
    return pl.pallas_call(
        _expert_kernel,
        out_shape=jax.ShapeDtypeStruct(x_rows.shape, F32),
        grid_spec=grid_spec,
        compiler_params=_cparams(("arbitrary",), VMEM_LIMIT),
        name="moe_expert_ffn",
    )(blk_expert, blk_used, x_rows, w1, b1, w2, b2)


def _combine_kernel(dest_ref, next_ref, gate_ref, y_ref, h_ref, gt_ref, g_ref, o_ref, buf_ref, sems, *, normalize):
    tt = h_ref.shape[0]
    i = pl.program_id(0)
    slot = i % 2

    def copy(idx_ref, slot_, t, k):
        src = y_ref.at[pl.ds(pl.multiple_of(idx_ref[t * TOP_K + k], SUBLANES), SUBLANES)]
        dst = buf_ref.at[slot_, k, pl.ds(pl.multiple_of(t * SUBLANES, SUBLANES), SUBLANES)]
        return pltpu.make_async_copy(src, dst, sems.at[slot_])

    def loop(body):
        def wrapped(t, c):
            for k in range(TOP_K):
                body(t, k)
            return c
        lax.fori_loop(0, tt, wrapped, 0)

    @pl.when(i == 0)
    def _():
        loop(lambda t, k: copy(dest_ref, slot, t, k).start())

    loop(lambda t, k: copy(dest_ref, slot, t, k).wait())

    @pl.when(i + 1 < pl.num_programs(0))
    def _():
        loop(lambda t, k: copy(next_ref, 1 - slot, t, k).start())

    gates = gate_ref[...]
    y = gates[:, 0:1] * _tiles_to_rows(buf_ref, tt, lead=(slot, 0))
    for k in range(1, TOP_K):
        y = y + gates[:, k:k + 1] * _tiles_to_rows(buf_ref, tt, lead=(slot, k))
    h = h_ref[...] + (1.0 + gt_ref[0]) * y
    if normalize:
        h = h * lax.rsqrt(jnp.mean(h * h, axis=-1, keepdims=True) + NORM_EPS) * g_ref[...]
    o_ref[...] = h


def _combine_rows(dest_flat, gates, y_rows, h1, gt2, g, normalize, tt=256):
    B, S, D = h1.shape
    T = B * S
    assert S % tt == 0
    last = T // tt - 1
    out = pl.pallas_call(
        functools.partial(_combine_kernel, normalize=normalize),
        out_shape=jax.ShapeDtypeStruct((T, D), F32),
        grid=(T // tt,),
        in_specs=[pl.BlockSpec((tt * TOP_K,), lambda i: (i,), memory_space=pltpu.SMEM),
                  pl.BlockSpec((tt * TOP_K,), lambda i: (jnp.minimum(i + 1, last),), memory_space=pltpu.SMEM),
                  pl.BlockSpec((tt, LANES), lambda i: (i, 0)),
                  pl.BlockSpec(memory_space=pl.ANY),
                  pl.BlockSpec((tt, D), lambda i: (i, 0)),
                  pl.BlockSpec((1, 1, D), lambda i: ((i * tt) // S, 0, 0)),
                  pl.BlockSpec((1, D), lambda i: (0, 0))],
        out_specs=pl.BlockSpec((tt, D), lambda i: (i, 0)),
        scratch_shapes=[pltpu.VMEM((2, TOP_K, tt * SUBLANES, LANES), F32), pltpu.SemaphoreType.DMA((2,))],
        compiler_params=_cparams(("arbitrary",)),
        name="moe_combine_rows",
    )(dest_flat, dest_flat, gates, y_rows, h1.reshape(T, D), gt2, g)
    return out.reshape(B, S, D)


def _overlap_matrix(S):
    nc = S // CMP_STRIDE - 1
    ns = S // SEL_BLOCK
    c0 = np.arange(nc) * CMP_STRIDE
    s0 = np.arange(ns) * SEL_BLOCK
    ov = np.clip(np.minimum(c0[:, None] + CMP_LEN, s0[None, :] + SEL_BLOCK)
                 - np.maximum(c0[:, None], s0[None, :]), 0, None) / CMP_LEN
    return np.concatenate([ov, np.zeros((1, ns))], axis=0).astype(np.float32)


def _layer(h, mods, norm1_g, w_in, pe_k, pe_v, cw1_k, cw2_k, cw1_v, cw2_v, conv_w, conv_b, gate_b, ml_norm_g,
           w_up_nsa, w_up_ml, w_out, norm2_g, router_w, router_b, exp_w1, exp_b1, exp_w2, exp_b2,
           final_g, last_layer):
    B, S, D = h.shape
    T = B * S
    sh1, sc1, gt1, sh2, sc2, gt2 = mods

    w_r = jnp.concatenate([w_in[:, 0:1280], w_in[:, 1304:2328], w_in[:, 2328:2840], w_in[:, 2848:3360],
                           w_in[:, 3360:5408], w_in[:, 1280:1304], w_in[:, 2840:2848],
                           jnp.zeros((D, IN_WR - 5408), w_in.dtype)], axis=1).astype(BF16)
    q, kvc, kvs, ml_qk, ml_v, ml_o, merge, small = _in_projection(h, norm1_g.reshape(1, D), sc1, sh1, w_r)

    nb16 = S // CMP_STRIDE
    t16 = kvc.reshape(B, S, 4, NSA_DH).transpose(0, 2, 1, 3).reshape(B, 4, nb16, CMP_STRIDE * NSA_DH)
    pe = jnp.stack([pe_k.reshape(2, CMP_STRIDE * NSA_DH), pe_v.reshape(2, CMP_STRIDE * NSA_DH)])
    cmp = _compress(t16, pe, jnp.stack([cw1_k, cw1_v]), jnp.stack([cw2_k, cw2_v]))
    n_sel = min(SEL_TOPN, S // SEL_BLOCK)
    kc = cmp[:, :NSA_GROUPS].transpose(0, 2, 1, 3).reshape(B, nb16, NSA_KVW).astype(BF16)
    vct = cmp[:, NSA_GROUPS:].transpose(0, 1, 3, 2).reshape(B, NSA_KVW, nb16).astype(BF16)

    def key_major_tiles(v, w):
        vt = v.reshape(B, S // w, w, LANES).swapaxes(-1, -2)
        return jnp.concatenate([vt, jnp.ones((B, S // w, ONES_ROWS, w), vt.dtype)], axis=2)

    o_nsa = _nsa_attention(q, kc, vct, jnp.asarray(_overlap_matrix(S).T), kvs,
                           key_major_tiles(kvs[:, :, LANES:2 * LANES], min(8 * LANES, S)),
                           key_major_tiles(kvs[:, :, 3 * LANES:], LANES), small, n_sel)

    gates = small[:, :, 3 * NSA_HEADS:3 * NSA_HEADS + 2 * ML_HEADS].transpose(0, 2, 1)
    gate_row = gates.reshape(B, 2 * ML_HEADS, S // ML_CHUNK, ML_CHUNK)
    gate_col = gates.reshape(B, 2 * ML_HEADS, S, 1)
    o_ml = _mlstm(ml_qk, ml_v, ml_o, gate_row, gate_col, gate_b, conv_w, conv_b.reshape(1, -1),
                  ml_norm_g.reshape(1, -1))

    h1, n2, ti, tg = _out_projection(o_nsa, o_ml, merge, h, gt1, sc2, sh2, norm2_g.reshape(1, D),
                                     w_up_nsa.astype(BF16), w_up_ml.astype(BF16), w_out.astype(BF16),
                                     router_w, router_b.reshape(1, N_EXPERTS))

    A = T * TOP_K
    n_blk = -(-(A + N_EXPERTS * (MOE_ROWS - 1)) // MOE_ROWS)
    P = n_blk * MOE_ROWS
    ti2 = ti.reshape(T, LANES)
    rank, cnt = _expert_rank(ti2)
    dest, blk_e = _expert_dest(cnt, ti2, rank, n_blk)
    dest_flat = dest[:, :TOP_K].reshape(A)
    x_rows = _scatter_rows(dest_flat, n2.reshape(T * SUBLANES, LANES), jnp.zeros((P * SUBLANES, LANES), F32))
    y_rows = _expert_ffn(blk_e[:, 0], blk_e[:, 1], x_rows, exp_w1, exp_b1.reshape(N_EXPERTS, 1, -1),
                         exp_w2, exp_b2.reshape(N_EXPERTS, 1, -1))
    return _combine_rows(dest_flat, tg.reshape(T, LANES), y_rows, h1, gt2, final_g, normalize=last_layer)


def kernel(x, c, ada_w, ada_b, norm1_g, w_in, nsa_pe_k, nsa_pe_v, nsa_cmp_w1_k, nsa_cmp_w2_k, nsa_cmp_w1_v, nsa_cmp_w2_v, ml_conv_w, ml_conv_b, ml_gate_b, ml_norm_g, w_up_nsa, w_up_ml, w_out, norm2_g, router_w, router_b, exp_w1, exp_b1, exp_w2, exp_b2, final_g):
    B, S, D = x.shape
    depth = ada_w.shape[0]
    h = x
    for layer in range(depth):
        mod = _modulation(c, ada_w[layer], ada_b[layer])
        mods = [mod[:, i * D:(i + 1) * D].reshape(B, 1, D) for i in range(6)]
        h = _layer(h, mods, norm1_g[layer], w_in[layer], nsa_pe_k[layer], nsa_pe_v[layer],
                   nsa_cmp_w1_k[layer], nsa_cmp_w2_k[layer], nsa_cmp_w1_v[layer], nsa_cmp_w2_v[layer],
                   ml_conv_w[layer], ml_conv_b[layer], ml_gate_b[layer], ml_norm_g[layer],
                   w_up_nsa[layer], w_up_ml[layer], w_out[layer], norm2_g[layer], router_w[layer],
                   router_b[layer], exp_w1[layer], exp_b1[layer], exp_w2[layer], exp_b2[layer],
                   final_g.reshape(1, D), layer + 1 == depth)
    return h
```

```python
import functools

import numpy as np
import jax
import jax.numpy as jnp
from jax import lax
from jax.experimental import pallas as pl
from jax.experimental.pallas import tpu as pltpu

F32 = jnp.float32
BF16 = jnp.bfloat16
I32 = jnp.int32
HIGHEST = lax.Precision.HIGHEST

D_MODEL = 1024
NSA_HEADS = 8
NSA_GROUPS = 2
NSA_REP = NSA_HEADS // NSA_GROUPS
NSA_DH = 64
NSA_SCALE = NSA_DH ** -0.5
Q_PRESCALE = NSA_SCALE * float(np.log2(np.e))
ONES_ROWS = 16
ISSUE_UNROLL = 8
DRAIN_UNROLL = 16
CMP_STRIDE = 16
CMP_LEN = 32
CMP_HIDDEN = 128
SEL_BLOCK = 64
SEL_TOPN = 16
WINDOW = 512
Q_BLOCK = 128
ML_HEADS = 4
ML_DH = 128
ML_CHUNK = 64
CONV_WIDTH = 4
N_EXPERTS = 32
TOP_K = 4
D_EXPERT = 1024
SWIGLU_LIMIT = 7.0
SWIGLU_ALPHA = 1.702
MOE_ROWS = 256
NORM_EPS = 1e-6
NEG_INF = -1e30
FORCE_BONUS = 1e4

NSA_QW = NSA_HEADS * NSA_DH
NSA_KVW = NSA_GROUPS * NSA_DH
ML_W = ML_HEADS * ML_DH
LANES = 128
SUBLANES = 8
VMEM_LIMIT = 52 * 1024 * 1024

_C_Q = (0, 512)
_C_KVC = (512, 768)
_C_KVS = (768, 1280)
_C_MQK = (1280, 2304)
_C_MV = (2304, 2816)
_C_MO = (2816, 3328)
_C_MG = (3328, 5376)
_C_SM = (5376, 5504)
IN_WR = 5504


def _sigmoid(x):
    return 1.0 / (1.0 + jnp.exp(-x))


def _iota(shape, dim):
    return lax.broadcasted_iota(I32, shape, dim)


def _cparams(sem, vmem=None):
    return pltpu.CompilerParams(dimension_semantics=sem, vmem_limit_bytes=vmem)


def _tiles_to_rows(ref, n, lead=()):
    return jnp.concatenate([ref[lead + (pl.ds(s, n, stride=SUBLANES), slice(None))] for s in range(SUBLANES)],
                           axis=1)


def _rows_to_tiles(ref, x, lead=()):
    n = x.shape[0]
    for s in range(SUBLANES):
        ref[lead + (pl.ds(s, n, stride=SUBLANES), slice(None))] = x[:, s * LANES:(s + 1) * LANES]


def _mod_kernel(c_ref, w_ref, b_ref, o_ref):
    c = c_ref[...]
    ca = c * _sigmoid(c)
    o_ref[...] = jnp.dot(ca, w_ref[...], precision=HIGHEST, preferred_element_type=F32) + b_ref[...]


def _modulation(c, ada_w, ada_b):
    B, D = c.shape
    n = ada_w.shape[1] // D
    return pl.pallas_call(
        _mod_kernel,
        out_shape=jax.ShapeDtypeStruct((B, n * D), F32),
        grid=(n,),
        in_specs=[pl.BlockSpec((B, D), lambda j: (0, 0)),
                  pl.BlockSpec((D, D), lambda j: (0, j)),
                  pl.BlockSpec((1, D), lambda j: (0, j))],
        out_specs=pl.BlockSpec((B, D), lambda j: (0, j)),
        compiler_params=_cparams(("parallel",)),
        name="modulation",
    )(c, ada_w, ada_b.reshape(1, n * D))


def _inproj_kernel(x_ref, g_ref, sc_ref, sh_ref, w_ref,
                   q_ref, kvc_ref, kvs_ref, mqk_ref, mv_ref, mo_ref, mg_ref, sm_ref):
    x = x_ref[0]
    y = x * lax.rsqrt(jnp.mean(x * x, axis=-1, keepdims=True) + NORM_EPS) * g_ref[...]
    n = (y * (1.0 + sc_ref[0]) + sh_ref[0]).astype(BF16)

    def proj(c):
        return jnp.dot(n, w_ref[:, c[0]:c[1]], preferred_element_type=F32)

    q_ref[0] = (proj(_C_Q) * Q_PRESCALE).astype(BF16)
    kvc_ref[0] = proj(_C_KVC)
    kvs_ref[0] = proj(_C_KVS).astype(BF16)
    mqk_ref[0] = proj(_C_MQK)
    mv_ref[0] = proj(_C_MV)
    mo_ref[0] = proj(_C_MO)
    mg_ref[0] = proj(_C_MG)
    sm_ref[0] = proj(_C_SM)


def _in_projection(x, g, sc, sh, w_r, tm=256):
    B, S, D = x.shape
    widths = [c[1] - c[0] for c in (_C_Q, _C_KVC, _C_KVS, _C_MQK, _C_MV, _C_MO, _C_MG, _C_SM)]
    dtypes = [BF16, F32, BF16, F32, F32, F32, F32, F32]
    vec = pl.BlockSpec((1, 1, D), lambda b, i: (b, 0, 0))
    return pl.pallas_call(
        _inproj_kernel,
        out_shape=[jax.ShapeDtypeStruct((B, S, w), dt) for w, dt in zip(widths, dtypes)],
        grid=(B, S // tm),
        in_specs=[pl.BlockSpec((1, tm, D), lambda b, i: (b, i, 0)),
                  pl.BlockSpec((1, D), lambda b, i: (0, 0)),
                  vec, vec,
                  pl.BlockSpec((D, IN_WR), lambda b, i: (0, 0))],
        out_specs=[pl.BlockSpec((1, tm, w), lambda b, i: (b, i, 0)) for w in widths],
        compiler_params=_cparams(("parallel", "parallel"), VMEM_LIMIT),
        name="in_projection",
    )(x, g, sc, sh, w_r)


def _compress_kernel(t_ref, pe_ref, w1_ref, w2_ref, o_ref):
    half = CMP_STRIDE * NSA_DH
    t = t_ref[0, 0]
    pe = pe_ref[0]
    nb = t.shape[0]
    a = jnp.dot((t + pe[0:1]).astype(BF16), w1_ref[0, :half, :].astype(BF16), preferred_element_type=F32)
    b = jnp.dot((t + pe[1:2]).astype(BF16), w1_ref[0, half:, :].astype(BF16), preferred_element_type=F32)
    h = a + pltpu.roll(b, nb - 1, 0)
    gl = 0.5 * h * (1.0 + jnp.tanh(np.sqrt(2.0 / np.pi).astype(np.float32) * (h + 0.044715 * (h * h * h))))
    o_ref[0, 0] = jnp.dot(gl.astype(BF16), w2_ref[0].astype(BF16), preferred_element_type=F32)


def _compress(t16, pe, w1, w2):
    B, _, nb, wdt = t16.shape
    return pl.pallas_call(
        _compress_kernel,
        out_shape=jax.ShapeDtypeStruct((B, 4, nb, NSA_DH), F32),
        grid=(B, 4),
        in_specs=[pl.BlockSpec((1, 1, nb, wdt), lambda b, j: (b, j, 0, 0)),
                  pl.BlockSpec((1, 2, wdt), lambda b, j: (j // 2, 0, 0)),
                  pl.BlockSpec((1, CMP_LEN * NSA_DH, CMP_HIDDEN), lambda b, j: (j // 2, 0, 0)),
                  pl.BlockSpec((1, CMP_HIDDEN, NSA_DH), lambda b, j: (j // 2, 0, 0))],
        out_specs=pl.BlockSpec((1, 1, nb, NSA_DH), lambda b, j: (b, j, 0, 0)),
        compiler_params=_cparams(("parallel", "parallel")),
        name="nsa_compress",
    )(t16, pe, w1, w2)


def _nsa_kernel(q_ref, kc_ref, vct_ref, ovt_ref, ks_ref, vst_ref, kw_ref, vwt_ref, sm_ref, o_ref, *, n_sel, tk):
    qb = Q_BLOCK
    ncp = kc_ref.shape[1]
    ns = ovt_ref.shape[0]
    q0 = pl.program_id(1) * qb
    t_l = q0 + _iota((1, qb), 1)
    sub = _iota((LANES, qb), 0)
    gT = _sigmoid(sm_ref[0].T)

    gw = NSA_REP * qb
    q_heads = []
    for c in range(NSA_HEADS // 2):
        qc = q_ref[0, :, c * LANES:(c + 1) * LANES].astype(F32).T
        for e in range(2):
            g = (2 * c + e) // NSA_REP
            x = qc if e == g else pltpu.roll(qc, NSA_DH, 0)
            q_heads.append(jnp.where((sub >> 6) == g, x, 0.0).astype(BF16))
    qT = [jnp.concatenate(q_heads[g * NSA_REP:(g + 1) * NSA_REP], axis=1) for g in range(NSA_GROUPS)]
    t_g = q0 + (_iota((1, gw), 1) & (qb - 1))

    def head_cols(x, r):
        return x[:, r * qb:(r + 1) * qb]

    def gate_row(branch, g):
        rows = [gT[branch * NSA_HEADS + g * NSA_REP + r:branch * NSA_HEADS + g * NSA_REP + r + 1]
                for r in range(NSA_REP)]
        return jnp.concatenate(rows, axis=1)

    kc = kc_ref[0]
    vct = vct_ref[0]
    cmask = (_iota((ncp, gw), 0) * CMP_STRIDE + (CMP_LEN - 1)) <= t_g
    blk = _iota((ns, qb), 0)
    blk_f = blk.astype(F32)
    cur = t_l >> 6
    forced = (blk == 0) | (blk == cur) | (blk == cur - 1)
    ocT = []
    selT = []
    for g in range(NSA_GROUPS):
        s = jnp.dot(kc, qT[g], preferred_element_type=F32)
        s = jnp.where(cmask, s, NEG_INF)
        e = jnp.exp2(s - jnp.max(s, axis=0, keepdims=True))
        p = e * (1.0 / jnp.sum(e, axis=0, keepdims=True))
        p = jnp.where(cmask, p, 0.0)
        ocT.append(jnp.dot(vct, p.astype(BF16), preferred_element_type=F32))
        psum = head_cols(p, 0)
        for r in range(1, NSA_REP):
            psum = psum + head_cols(p, r)
        imp = jnp.dot(ovt_ref[...], psum, precision=HIGHEST, preferred_element_type=F32)
        imp = jnp.where(blk > cur, NEG_INF, imp + jnp.where(forced, FORCE_BONUS, 0.0))
        sel = jnp.zeros((ns, qb), F32)
        for _ in range(n_sel):
            mx = jnp.max(imp, axis=0, keepdims=True)
            first = jnp.min(jnp.where(imp == mx, blk_f, float(ns)), axis=0, keepdims=True)
            pick = blk_f == first
            sel = jnp.where(pick, 1.0, sel)
            imp = jnp.where(pick, -jnp.inf, imp)
        selT.append(sel.astype(BF16))

    wlen = WINDOW + qb
    wt0 = jnp.maximum(q0 - WINDOW, 0) // LANES
    wstart = pl.multiple_of(wt0 * LANES, LANES)
    kwin = kw_ref[0, pl.ds(wstart, wlen), :]
    rel = t_g - (wstart + _iota((wlen, gw), 0))
    wmask = lax.bitcast_convert_type(rel, jnp.uint32) < WINDOW
    partial = []
    for g in range(NSA_GROUPS):
        lo = g * NSA_DH
        s = jnp.dot(kwin, qT[g], preferred_element_type=F32)
        s = jnp.where(wmask, s, NEG_INF)
        pb = jnp.exp2(s - jnp.max(s, axis=0, keepdims=True)).astype(BF16)
        ow = jnp.dot(vwt_ref[0, wt0], pb[:LANES], preferred_element_type=F32)
        for i in range(1, wlen // LANES):
            ow = ow + jnp.dot(vwt_ref[0, wt0 + i], pb[i * LANES:(i + 1) * LANES], preferred_element_type=F32)
        ow = ow[:LANES] * (1.0 / ow[LANES:LANES + 1])
        partial.append(gate_row(0, g) * ocT[g][lo:lo + NSA_DH] + gate_row(2, g) * ow[lo:lo + NSA_DH])

    nt = (q0 + qb + tk - 1) // tk

    def tile_step(r0, j, lane0, width, diagonal, carry):
        kt = ks_ref[0, pl.ds(r0, width), :]
        expand = jnp.where(((r0 + _iota((width, ns), 0)) >> 6) == _iota((width, ns), 1), 1.0, 0.0).astype(BF16)
        new = []
        for g in range(NSA_GROUPS):
            m, l, acc = carry[g]
            hit = jnp.dot(expand, selT[g], preferred_element_type=F32)
            if diagonal:
                hit = jnp.where((r0 + _iota((width, qb), 0)) <= t_l, hit, 0.0)
            ok = hit > 0.5
            s = jnp.dot(kt, qT[g], preferred_element_type=F32)
            s = jnp.concatenate([jnp.where(ok, head_cols(s, r), NEG_INF) for r in range(NSA_REP)], axis=1)
            m_new = jnp.maximum(m, jnp.max(s, axis=0, keepdims=True))
            alpha = jnp.exp2(m - m_new)
            pv = jnp.dot(vst_ref[0, j, :, lane0:lane0 + width], jnp.exp2(s - m_new).astype(BF16),
                         preferred_element_type=F32)
            new.append((m_new, alpha * l + pv[LANES:LANES + 1], alpha * acc + pv[:LANES]))
        return tuple(new)

    init = tuple((jnp.full((1, gw), NEG_INF, F32), jnp.zeros((1, gw), F32), jnp.zeros((LANES, gw), F32))
                 for _ in range(NSA_GROUPS))
    final = lax.fori_loop(0, nt - 1, lambda j, c: tile_step(pl.multiple_of(j * tk, tk), j, 0, tk, False, c), init)
    half = tk // 2
    base = pl.multiple_of((nt - 1) * tk, tk)
    final = tile_step(base, nt - 1, 0, half, True, final)
    final = lax.cond(q0 + qb > base + half,
                     lambda c: tile_step(pl.multiple_of(base + half, half), nt - 1, half, half, True, c),
                     lambda c: c, final)

    heads = []
    for g in range(NSA_GROUPS):
        _, l, acc = final[g]
        o_g = partial[g] + gate_row(1, g) * (acc[g * NSA_DH:(g + 1) * NSA_DH] * (1.0 / l))
        heads += [head_cols(o_g, r) for r in range(NSA_REP)]
    for c in range(NSA_HEADS // 2):
        o_ref[0, :, c * LANES:(c + 1) * LANES] = jnp.concatenate(heads[2 * c:2 * c + 2], axis=0).T


def _nsa_attention(q, kc, vct, ovt, kvs, vst, vwt, small, n_sel):
    B, S, _ = q.shape
    ns, ncp = ovt.shape
    tk = vst.shape[3]
    assert S >= WINDOW + Q_BLOCK and S % tk == 0 and tk % LANES == 0
    nt128 = S // LANES

    def whole(shape):
        return pl.BlockSpec((1,) + shape, lambda b, i: (b,) + (0,) * len(shape))

    return pl.pallas_call(
        functools.partial(_nsa_kernel, n_sel=n_sel, tk=tk),
        out_shape=jax.ShapeDtypeStruct((B, S, NSA_QW), F32),
        grid=(B, S // Q_BLOCK),
        in_specs=[pl.BlockSpec((1, Q_BLOCK, NSA_QW), lambda b, i: (b, i, 0)),
                  whole((ncp, LANES)), whole((LANES, ncp)),
                  pl.BlockSpec((ns, ncp), lambda b, i: (0, 0)),
                  pl.BlockSpec((1, S, LANES), lambda b, i: (b, 0, 0)),
                  whole((S // tk, LANES + ONES_ROWS, tk)),
                  pl.BlockSpec((1, S, LANES), lambda b, i: (b, 0, 2)),
                  whole((nt128, LANES + ONES_ROWS, LANES)),
                  pl.BlockSpec((1, Q_BLOCK, LANES), lambda b, i: (b, i, 0))],
        out_specs=pl.BlockSpec((1, Q_BLOCK, NSA_QW), lambda b, i: (b, i, 0)),
        compiler_params=_cparams(("parallel", "parallel"), VMEM_LIMIT),
        name="nsa_attention",
    )(q, kc, vct, ovt, kvs, vst, kvs, vwt, small)


def _mlstm_kernel(gb_ref, q_ref, k_ref, v_ref, o_ref, grow_ref, gcol_ref, cw_ref, cb_ref, ng_ref, out_ref,
                  c_scr, n_scr, m_scr, hq_scr, hk_scr):
    L = ML_CHUNK
    H = ML_HEADS
    ts = q_ref.shape[1]
    kscale = ML_DH ** -0.5
    row = _iota((L, L), 0)
    col = _iota((L, L), 1)
    tri = row >= col
    halo_rows = SUBLANES

    @pl.when(pl.program_id(1) == 0)
    def _():
        c_scr[...] = jnp.zeros_like(c_scr)
        n_scr[...] = jnp.zeros_like(n_scr)
        m_scr[...] = jnp.zeros_like(m_scr)
        hq_scr[...] = jnp.zeros_like(hq_scr)
        hk_scr[...] = jnp.zeros_like(hk_scr)

    def conv_silu(x_ref, prev_ref, woff, h, c, r0):
        lanes = slice(h * ML_DH, (h + 1) * ML_DH)
        wl = slice(woff + h * ML_DH, woff + (h + 1) * ML_DH)
        main = x_ref[0, pl.ds(r0, L), lanes]
        h0 = pl.multiple_of(jnp.maximum(r0 - halo_rows, 0), halo_rows)
        halo = jnp.where(c > 0, x_ref[0, pl.ds(h0, halo_rows), lanes], prev_ref[:, lanes])
        cat = jnp.concatenate([halo, main], axis=0)
        y = main * cw_ref[CONV_WIDTH - 1:CONV_WIDTH, wl] + cb_ref[:, wl]
        for w in range(CONV_WIDTH - 1):
            sft = CONV_WIDTH - 1 - w
            y = y + pltpu.roll(cat, sft, 0)[halo_rows:, :] * cw_ref[w:w + 1, wl]
        return y * _sigmoid(y)

    def logsig(x):
        return -(jnp.maximum(-x, 0.0) + jnp.log(1.0 + jnp.exp(-jnp.abs(x))))

    def head_step(h, c, r0, C, n, m):
        lanes = slice(h * ML_DH, (h + 1) * ML_DH)
        gb_i = gb_ref[h]
        gb_f = gb_ref[H + h]
        qc = conv_silu(q_ref, hq_scr, 0, h, c, r0)
        kc = conv_silu(k_ref, hk_scr, ML_W, h, c, r0) * kscale
        vc = v_ref[0, pl.ds(r0, L), lanes]
        li_row = grow_ref[0, h, pl.ds(c, 1), :] + gb_i
        lf_row = logsig(grow_ref[0, H + h, pl.ds(c, 1), :] + gb_f)
        li_col = gcol_ref[0, h, pl.ds(r0, L), :] + gb_i
        lf_col = logsig(gcol_ref[0, H + h, pl.ds(r0, L), :] + gb_f)
        b_col = jnp.sum(jnp.where(tri, lf_row, 0.0), axis=1, keepdims=True)
        b_row = jnp.sum(jnp.where(row <= col, lf_col, 0.0), axis=0, keepdims=True)
        Dm = jnp.where(tri, b_col - b_row + li_row, NEG_INF)
        inter = b_col + m
        m_t = jnp.maximum(inter, jnp.max(Dm, axis=1, keepdims=True))
        Dw = jnp.exp(Dm - m_t)
        inter_w = jnp.exp(inter - m_t)
        qb16 = qc.astype(BF16)
        kb16 = kc.astype(BF16)
        vb16 = vc.astype(BF16)
        qk = lax.dot_general(qb16, kb16, (((1,), (1,)), ((), ())), preferred_element_type=F32) * Dw
        num = (jnp.dot(qk.astype(BF16), vb16, preferred_element_type=F32)
               + inter_w * jnp.dot(qb16, C.astype(BF16), preferred_element_type=F32))
        den = jnp.sum(qk, axis=1, keepdims=True) + inter_w * jnp.sum(qc * n, axis=1, keepdims=True)
        hc = num / jnp.maximum(jnp.abs(den), jnp.exp(-m_t))
        bL = b_col[L - 1:L, :]
        a_col = bL - b_col + li_col
        m_new = jnp.maximum(bL + m, jnp.max(a_col, axis=0, keepdims=True))
        aw = jnp.exp(a_col - m_new)
        decay = jnp.exp(bL + m - m_new)
        awk = aw * kc
        C_new = decay * C + jnp.dot(awk.T.astype(BF16), vb16, preferred_element_type=F32)
        n_new = decay * n + jnp.sum(awk, axis=0, keepdims=True)
        hn = hc * lax.rsqrt(jnp.mean(hc * hc, axis=1, keepdims=True) + NORM_EPS) * ng_ref[:, lanes]
        out_ref[0, pl.ds(r0, L), lanes] = hn * _sigmoid(o_ref[0, pl.ds(r0, L), lanes])
        return C_new, n_new, m_new

    def step(c, carry):
        r0 = pl.multiple_of(c * L, L)
        return tuple(head_step(h, c, r0, *carry[h]) for h in range(H))

    init = tuple((c_scr[h], n_scr[h], m_scr[h][:, 0:1]) for h in range(H))
    final = lax.fori_loop(0, ts // L, step, init)
    for h in range(H):
        C, n, m = final[h]
        c_scr[h] = C
        n_scr[h] = n
        m_scr[h] = jnp.broadcast_to(m, (1, LANES))
    hq_scr[...] = q_ref[0, ts - halo_rows:ts, :]
    hk_scr[...] = k_ref[0, ts - halo_rows:ts, :]


def _mlstm(ml_qk, ml_v, ml_o, gate_row, gate_col, gate_b, conv_w, conv_b, norm_g, ts=1024):
    B, S, _ = ml_v.shape
    H = ML_HEADS
    ts = min(ts, S)

    def rows(cb):
        return pl.BlockSpec((1, ts, ML_W), lambda b, i, gb: (b, i, cb))

    def full(shape):
        return pl.BlockSpec(shape, lambda b, i, gb: (0,) * len(shape))

    grid_spec = pltpu.PrefetchScalarGridSpec(
        num_scalar_prefetch=1,
        grid=(B, S // ts),
        in_specs=[rows(0), rows(1), rows(0), rows(0),
                  pl.BlockSpec((1, 2 * H, ts // ML_CHUNK, ML_CHUNK), lambda b, i, gb: (b, 0, i, 0)),
                  pl.BlockSpec((1, 2 * H, ts, 1), lambda b, i, gb: (b, 0, i, 0)),
                  full((CONV_WIDTH, 2 * ML_W)), full((1, 2 * ML_W)), full((1, ML_W))],
        out_specs=rows(0),
        scratch_shapes=[pltpu.VMEM((H, ML_DH, ML_DH), F32), pltpu.VMEM((H, 1, ML_DH), F32),
                        pltpu.VMEM((H, 1, LANES), F32),
                        pltpu.VMEM((SUBLANES, ML_W), F32), pltpu.VMEM((SUBLANES, ML_W), F32)],
    )
    return pl.pallas_call(
        _mlstm_kernel,
        out_shape=jax.ShapeDtypeStruct((B, S, ML_W), F32),
        grid_spec=grid_spec,
        compiler_params=_cparams(("parallel", "arbitrary"), VMEM_LIMIT),
        name="mlstm",
    )(gate_b, ml_qk, ml_qk, ml_v, ml_o, gate_row, gate_col, conv_w, conv_b, norm_g)


def _pack_cols(cols, dtype):
    tm = cols[0].shape[0]
    lane = _iota((tm, LANES), 1)
    out = jnp.zeros((tm, LANES), dtype)
    for k, c in enumerate(cols):
        out = jnp.where(lane == k, c, out)
    return out


def _outproj_kernel(a_ref, b_ref, mg_ref, x_ref, gt_ref, sc_ref, sh_ref, g2_ref,
                    wa_ref, wb_ref, wo_ref, wr_ref, br_ref,
                    h_ref, n_ref, ti_ref, tg_ref):
    D = x_ref.shape[2]
    ua = jnp.dot(a_ref[0].astype(BF16), wa_ref[...], preferred_element_type=F32)
    ub = jnp.dot(b_ref[0].astype(BF16), wb_ref[...], preferred_element_type=F32)
    u = _sigmoid(mg_ref[0, :, :D]) * ua + _sigmoid(mg_ref[0, :, D:]) * ub
    mix = jnp.dot(u.astype(BF16), wo_ref[...], preferred_element_type=F32)
    h = x_ref[0] + (1.0 + gt_ref[0]) * mix
    h_ref[0] = h
    y = h * lax.rsqrt(jnp.mean(h * h, axis=-1, keepdims=True) + NORM_EPS) * g2_ref[...]
    n = y * (1.0 + sc_ref[0]) + sh_ref[0]
    _rows_to_tiles(n_ref, n, lead=(0,))
    logits = jnp.dot(n, wr_ref[...], precision=HIGHEST, preferred_element_type=F32) + br_ref[...]
    tm, ne = logits.shape
    lane = _iota((tm, ne), 1).astype(F32)
    work = logits
    vals, idxs = [], []
    for _ in range(TOP_K):
        mx = jnp.max(work, axis=-1, keepdims=True)
        ix = jnp.min(jnp.where(work == mx, lane, float(ne)), axis=-1, keepdims=True)
        vals.append(mx)
        idxs.append(ix)
        work = jnp.where(lane == ix, -jnp.inf, work)
    es = [jnp.exp(v - vals[0]) for v in vals]
    tot = es[0] + es[1] + es[2] + es[3]
    ti_ref[0] = _pack_cols(idxs, F32).astype(I32)
    tg_ref[0] = _pack_cols([e / tot for e in es], F32)


def _out_projection(o_nsa, o_ml, merge, x, gt1, sc2, sh2, g2, wa, wb, wo, wr, br, tm=256):
    B, S, D = x.shape
    vec = pl.BlockSpec((1, 1, D), lambda b, i: (b, 0, 0))

    def full(shape):
        return pl.BlockSpec(shape, lambda b, i: (0,) * len(shape))

    def rows(w):
        return pl.BlockSpec((1, tm, w), lambda b, i: (b, i, 0))

    assert D == SUBLANES * LANES
    return pl.pallas_call(
        _outproj_kernel,
        out_shape=[jax.ShapeDtypeStruct((B, S, D), F32), jax.ShapeDtypeStruct((B, S * SUBLANES, LANES), F32),
                   jax.ShapeDtypeStruct((B, S, LANES), I32), jax.ShapeDtypeStruct((B, S, LANES), F32)],
        grid=(B, S // tm),
        in_specs=[rows(NSA_QW), rows(ML_W), rows(2 * D), rows(D), vec, vec, vec, full((1, D)),
                  full(wa.shape), full(wb.shape), full(wo.shape), full(wr.shape), full((1, N_EXPERTS))],
        out_specs=[rows(D), pl.BlockSpec((1, tm * SUBLANES, LANES), lambda b, i: (b, i, 0)), rows(LANES),
                   rows(LANES)],
        compiler_params=_cparams(("parallel", "parallel"), VMEM_LIMIT),
        name="out_projection_router",
    )(o_nsa, o_ml, merge, x, gt1, sc2, sh2, g2, wa, wb, wo, wr, br)


def _rank_kernel(ti_ref, rank_ref, cnt_ref, carry_ref):
    tt = ti_ref.shape[0]

    @pl.when(pl.program_id(0) == 0)
    def _():
        carry_ref[...] = jnp.zeros_like(carry_ref)

    ti = ti_ref[...]
    lane = _iota((tt, LANES), 1)
    hots = [ti[:, k:k + 1] == lane for k in range(TOP_K)]
    member = jnp.zeros((tt, LANES), F32)
    for hot in hots:
        member = member + jnp.where(hot, 1.0, 0.0)
    strict = jnp.where(_iota((tt, tt), 0) > _iota((tt, tt), 1), 1.0, 0.0).astype(BF16)
    before = jnp.dot(strict, member.astype(BF16), preferred_element_type=F32) + carry_ref[0:1, :]
    ranks = [jnp.sum(jnp.where(hot, before, 0.0), axis=-1, keepdims=True) for hot in hots]
    rank_ref[...] = _pack_cols(ranks, F32)
    carry_ref[...] = carry_ref[...] + jnp.sum(member, axis=0, keepdims=True)
    cnt_ref[...] = carry_ref[...]


def _expert_rank(ti, tt=512):
    T = ti.shape[0]
    return pl.pallas_call(
        _rank_kernel,
        out_shape=[jax.ShapeDtypeStruct((T, LANES), F32), jax.ShapeDtypeStruct((SUBLANES, LANES), F32)],
        grid=(T // tt,),
        in_specs=[pl.BlockSpec((tt, LANES), lambda i: (i, 0))],
        out_specs=[pl.BlockSpec((tt, LANES), lambda i: (i, 0)),
                   pl.BlockSpec((SUBLANES, LANES), lambda i: (0, 0))],
        scratch_shapes=[pltpu.VMEM((SUBLANES, LANES), F32)],
        compiler_params=_cparams(("arbitrary",)),
        name="moe_rank",
    )(ti)


def _dest_kernel(cnt_ref, ti_ref, rank_ref, dest_ref, be_ref):
    tt = ti_ref.shape[0]
    nbp = be_ref.shape[0]
    cnt = cnt_ref[...].astype(I32)
    padded = ((cnt + (MOE_ROWS - 1)) >> 8) << 8
    lane8 = _iota((SUBLANES, LANES), 1)
    pad_end = padded
    sft = 1
    while sft < LANES:
        pad_end = pad_end + jnp.where(lane8 >= sft, pltpu.roll(pad_end, sft, 1), 0)
        sft *= 2
    pad_start = (pad_end - padded)[0:1, :].astype(F32)
    ti = ti_ref[...]
    lane = _iota((tt, LANES), 1)
    rank = rank_ref[...]
    dests = [jnp.sum(jnp.where(ti[:, k:k + 1] == lane, pad_start, 0.0), axis=-1, keepdims=True) + rank[:, k:k + 1]
             for k in range(TOP_K)]
    dest_ref[...] = _pack_cols(dests, F32).astype(I32) * SUBLANES
    first_row = _iota((nbp, LANES), 0) * MOE_ROWS
    done = (pad_end[0:1, :] <= first_row) & (_iota((nbp, LANES), 1) < N_EXPERTS)
    be = jnp.minimum(jnp.sum(jnp.where(done, 1.0, 0.0), axis=-1, keepdims=True), float(N_EXPERTS - 1))
    used = first_row < pad_end[0:1, N_EXPERTS - 1:N_EXPERTS]
    be_ref[...] = jnp.where(_iota((nbp, LANES), 1) == 1, jnp.where(used, 1, 0), be.astype(I32))


def _expert_dest(cnt, ti, rank, n_blk, tt=1024):
    T = ti.shape[0]
    return pl.pallas_call(
        _dest_kernel,
        out_shape=[jax.ShapeDtypeStruct((T, LANES), I32), jax.ShapeDtypeStruct((n_blk, LANES), I32)],
        grid=(T // tt,),
        in_specs=[pl.BlockSpec((SUBLANES, LANES), lambda i: (0, 0)),
                  pl.BlockSpec((tt, LANES), lambda i: (i, 0)),
                  pl.BlockSpec((tt, LANES), lambda i: (i, 0))],
        out_specs=[pl.BlockSpec((tt, LANES), lambda i: (i, 0)),
                   pl.BlockSpec((n_blk, LANES), lambda i: (0, 0))],
        compiler_params=_cparams(("arbitrary",)),
        name="moe_dest",
    )(cnt, ti, rank)


def _scatter_kernel(dest_ref, src_ref, init_ref, dst_ref, sem):
    del init_ref
    tt = src_ref.shape[0] // SUBLANES

    def copy(t, k):
        return pltpu.make_async_copy(
            src_ref.at[pl.ds(pl.multiple_of(t * SUBLANES, SUBLANES), SUBLANES)],
            dst_ref.at[pl.ds(pl.multiple_of(dest_ref[t * TOP_K + k], SUBLANES), SUBLANES)], sem)

    def issue(t, c):
        for k in range(TOP_K):
            copy(t, k).start()
        return c

    lax.fori_loop(0, tt, issue, 0, unroll=ISSUE_UNROLL)

    def drain(t, c):
        for k in range(TOP_K):
            copy(t, k).wait()
        return c

    lax.fori_loop(0, tt, drain, 0, unroll=DRAIN_UNROLL)


def _scatter_rows(dest_flat, rows, init, tt=512):
    T = rows.shape[0] // SUBLANES
    return pl.pallas_call(
        _scatter_kernel,
        out_shape=jax.ShapeDtypeStruct(init.shape, init.dtype),
        grid=(T // tt,),
        in_specs=[pl.BlockSpec((tt * TOP_K,), lambda i: (i,), memory_space=pltpu.SMEM),
                  pl.BlockSpec((tt * SUBLANES, LANES), lambda i: (i, 0)),
                  pl.BlockSpec(memory_space=pl.ANY)],
        out_specs=pl.BlockSpec(memory_space=pl.ANY),
        scratch_shapes=[pltpu.SemaphoreType.DMA],
        input_output_aliases={2: 0},
        compiler_params=_cparams(("arbitrary",)),
        name="moe_scatter_rows",
    )(dest_flat, rows, init)


def _expert_kernel(be_ref, used_ref, x_ref, w1_hbm, b1_ref, w2_hbm, b2_ref, y_ref,
                   w1f_ref, w2f_ref, w1b_ref, w2b_ref, sems, slot_ref):
    i = pl.program_id(0)
    nb = pl.num_programs(0)
    e = be_ref[i]

    def fetch(expert, slot):
        return (pltpu.make_async_copy(w1_hbm.at[expert], w1f_ref.at[slot], sems.at[0, slot]),
                pltpu.make_async_copy(w2_hbm.at[expert], w2f_ref.at[slot], sems.at[1, slot]))

    @pl.when(i == 0)
    def _():
        slot_ref[0] = 0
        for cp in fetch(e, 0):
            cp.start()

    @pl.when(jnp.logical_or(i == 0, e != be_ref[jnp.maximum(i - 1, 0)]))
    def _():
        slot = slot_ref[0]
        for cp in fetch(e, slot):
            cp.wait()
        w1b_ref[...] = w1f_ref[slot].astype(BF16)
        w2b_ref[...] = w2f_ref[slot].astype(BF16)
        j = lax.while_loop(lambda j: jnp.logical_and(j < nb, be_ref[jnp.minimum(j, nb - 1)] == e),
                           lambda j: j + 1, i + 1)

        @pl.when(j < nb)
        def _():
            for cp in fetch(be_ref[jnp.minimum(j, nb - 1)], 1 - slot):
                cp.start()

        slot_ref[0] = 1 - slot

    @pl.when(used_ref[i] != 0)
    def _():
        x = _tiles_to_rows(x_ref, MOE_ROWS).astype(BF16)
        h = jnp.dot(x, w1b_ref[...], preferred_element_type=F32) + b1_ref[0]
        glu = jnp.minimum(h[:, :D_EXPERT], SWIGLU_LIMIT)
        lin = jnp.clip(h[:, D_EXPERT:], -SWIGLU_LIMIT, SWIGLU_LIMIT)
        act = glu * _sigmoid(SWIGLU_ALPHA * glu) * (lin + 1.0)
        y = jnp.dot(act.astype(BF16), w2b_ref[...], preferred_element_type=F32) + b2_ref[0]
        _rows_to_tiles(y_ref, y)

    @pl.when(used_ref[i] == 0)
    def _():
        y_ref[...] = jnp.zeros_like(y_ref)


def _expert_ffn(blk_expert, blk_used, x_rows, w1, b1, w2, b2):
    P = x_rows.shape[0] // SUBLANES
    D = SUBLANES * LANES
    nb = P // MOE_ROWS
    tile_rows = pl.BlockSpec((MOE_ROWS * SUBLANES, LANES), lambda i, be, us: (i, 0))
    grid_spec = pltpu.PrefetchScalarGridSpec(
        num_scalar_prefetch=2,
        grid=(nb,),
        in_specs=[tile_rows,
                  pl.BlockSpec(memory_space=pl.ANY),
                  pl.BlockSpec((1, 1, 2 * D_EXPERT), lambda i, be, us: (be[i], 0, 0)),
                  pl.BlockSpec(memory_space=pl.ANY),
                  pl.BlockSpec((1, 1, D), lambda i, be, us: (be[i], 0, 0))],
        out_specs=tile_rows,
        scratch_shapes=[pltpu.VMEM((2, D, 2 * D_EXPERT), F32), pltpu.VMEM((2, D_EXPERT, D), F32),
                        pltpu.VMEM((D, 2 * D_EXPERT), BF16), pltpu.VMEM((D_EXPERT, D), BF16),
                        pltpu.SemaphoreType.DMA((2, 2)), pltpu.SMEM((1,), I32)],
    )
    return pl.pallas_call(
        _expert_kernel,
        out_shape=jax.ShapeDtypeStruct(x_rows.shape, F32),
        grid_spec=grid_spec,
        compiler_params=_cparams(("arbitrary",), VMEM_LIMIT),
        name="moe_expert_ffn",
    )(blk_expert, blk_used, x_rows, w1, b1, w2, b2)


def _combine_kernel(dest_ref, next_ref, gate_ref, y_ref, h_ref, gt_ref, g_ref, o_ref, buf_ref, sems, *, normalize):
    tt = h_ref.shape[0]
    i = pl.program_id(0)
    slot = i % 2

    def copy(idx_ref, slot_, t, k):
        src = y_ref.at[pl.ds(pl.multiple_of(idx_ref[t * TOP_K + k], SUBLANES), SUBLANES)]
        dst = buf_ref.at[slot_, k, pl.ds(pl.multiple_of(t * SUBLANES, SUBLANES), SUBLANES)]
        return pltpu.make_async_copy(src, dst, sems.at[slot_])

    def loop(body, unroll):
        def wrapped(t, c):
            for k in range(TOP_K):
                body(t, k)
            return c
        lax.fori_loop(0, tt, wrapped, 0, unroll=unroll)

    @pl.when(i == 0)
    def _():
        loop(lambda t, k: copy(dest_ref, slot, t, k).start(), ISSUE_UNROLL)

    loop(lambda t, k: copy(dest_ref, slot, t, k).wait(), DRAIN_UNROLL)

    @pl.when(i + 1 < pl.num_programs(0))
    def _():
        loop(lambda t, k: copy(next_ref, 1 - slot, t, k).start(), ISSUE_UNROLL)

    gates = gate_ref[...]
    y = gates[:, 0:1] * _tiles_to_rows(buf_ref, tt, lead=(slot, 0))
    for k in range(1, TOP_K):
        y = y + gates[:, k:k + 1] * _tiles_to_rows(buf_ref, tt, lead=(slot, k))
    h = h_ref[...] + (1.0 + gt_ref[0]) * y
    if normalize:
        h = h * lax.rsqrt(jnp.mean(h * h, axis=-1, keepdims=True) + NORM_EPS) * g_ref[...]
    o_ref[...] = h


def _combine_rows(dest_flat, gates, y_rows, h1, gt2, g, normalize, tt=256):
    B, S, D = h1.shape
    T = B * S
    assert S % tt == 0
    last = T // tt - 1
    out = pl.pallas_call(
        functools.partial(_combine_kernel, normalize=normalize),
        out_shape=jax.ShapeDtypeStruct((T, D), F32),
        grid=(T // tt,),
        in_specs=[pl.BlockSpec((tt * TOP_K,), lambda i: (i,), memory_space=pltpu.SMEM),
                  pl.BlockSpec((tt * TOP_K,), lambda i: (jnp.minimum(i + 1, last),), memory_space=pltpu.SMEM),
                  pl.BlockSpec((tt, LANES), lambda i: (i, 0)),
                  pl.BlockSpec(memory_space=pl.ANY),
                  pl.BlockSpec((tt, D), lambda i: (i, 0)),
                  pl.BlockSpec((1, 1, D), lambda i: ((i * tt) // S, 0, 0)),
                  pl.BlockSpec((1, D), lambda i: (0, 0))],
        out_specs=pl.BlockSpec((tt, D), lambda i: (i, 0)),
        scratch_shapes=[pltpu.VMEM((2, TOP_K, tt * SUBLANES, LANES), F32), pltpu.SemaphoreType.DMA((2,))],
        compiler_params=_cparams(("arbitrary",)),
        name="moe_combine_rows",
    )(dest_flat, dest_flat, gates, y_rows, h1.reshape(T, D), gt2, g)
    return out.reshape(B, S, D)


def _overlap_matrix(S):
    nc = S // CMP_STRIDE - 1
    ns = S // SEL_BLOCK
    c0 = np.arange(nc) * CMP_STRIDE
    s0 = np.arange(ns) * SEL_BLOCK
    ov = np.clip(np.minimum(c0[:, None] + CMP_LEN, s0[None, :] + SEL_BLOCK)
                 - np.maximum(c0[:, None], s0[None, :]), 0, None) / CMP_LEN
    return np.concatenate([ov, np.zeros((1, ns))], axis=0).astype(np.float32)


def _layer(h, mods, norm1_g, w_in, pe_k, pe_v, cw1_k, cw2_k, cw1_v, cw2_v, conv_w, conv_b, gate_b, ml_norm_g,
           w_up_nsa, w_up_ml, w_out, norm2_g, router_w, router_b, exp_w1, exp_b1, exp_w2, exp_b2,
           final_g, last_layer):
    B, S, D = h.shape
    T = B * S
    sh1, sc1, gt1, sh2, sc2, gt2 = mods

    w_r = jnp.concatenate([w_in[:, 0:1280], w_in[:, 1304:2328], w_in[:, 2328:2840], w_in[:, 2848:3360],
                           w_in[:, 3360:5408], w_in[:, 1280:1304], w_in[:, 2840:2848],
                           jnp.zeros((D, IN_WR - 5408), w_in.dtype)], axis=1).astype(BF16)
    q, kvc, kvs, ml_qk, ml_v, ml_o, merge, small = _in_projection(h, norm1_g.reshape(1, D), sc1, sh1, w_r)

    nb16 = S // CMP_STRIDE
    t16 = kvc.reshape(B, S, 4, NSA_DH).transpose(0, 2, 1, 3).reshape(B, 4, nb16, CMP_STRIDE * NSA_DH)
    pe = jnp.stack([pe_k.reshape(2, CMP_STRIDE * NSA_DH), pe_v.reshape(2, CMP_STRIDE * NSA_DH)])
    cmp = _compress(t16, pe, jnp.stack([cw1_k, cw1_v]), jnp.stack([cw2_k, cw2_v]))
    n_sel = min(SEL_TOPN, S // SEL_BLOCK)
    kc = cmp[:, :NSA_GROUPS].transpose(0, 2, 1, 3).reshape(B, nb16, NSA_KVW).astype(BF16)
    vct = cmp[:, NSA_GROUPS:].transpose(0, 1, 3, 2).reshape(B, NSA_KVW, nb16).astype(BF16)

    def key_major_tiles(v, w):
        vt = v.reshape(B, S // w, w, LANES).swapaxes(-1, -2)
        return jnp.concatenate([vt, jnp.ones((B, S // w, ONES_ROWS, w), vt.dtype)], axis=2)

    o_nsa = _nsa_attention(q, kc, vct, jnp.asarray(_overlap_matrix(S).T), kvs,
                           key_major_tiles(kvs[:, :, LANES:2 * LANES], min(8 * LANES, S)),
                           key_major_tiles(kvs[:, :, 3 * LANES:], LANES), small, n_sel)

    gates = small[:, :, 3 * NSA_HEADS:3 * NSA_HEADS + 2 * ML_HEADS].transpose(0, 2, 1)
    gate_row = gates.reshape(B, 2 * ML_HEADS, S // ML_CHUNK, ML_CHUNK)
    gate_col = gates.reshape(B, 2 * ML_HEADS, S, 1)
    o_ml = _mlstm(ml_qk, ml_v, ml_o, gate_row, gate_col, gate_b, conv_w, conv_b.reshape(1, -1),
                  ml_norm_g.reshape(1, -1))

    h1, n2, ti, tg = _out_projection(o_nsa, o_ml, merge, h, gt1, sc2, sh2, norm2_g.reshape(1, D),
                                     w_up_nsa.astype(BF16), w_up_ml.astype(BF16), w_out.astype(BF16),
                                     router_w, router_b.reshape(1, N_EXPERTS))

    A = T * TOP_K
    n_blk = -(-(A + N_EXPERTS * (MOE_ROWS - 1)) // MOE_ROWS)
    P = n_blk * MOE_ROWS
    ti2 = ti.reshape(T, LANES)
    rank, cnt = _expert_rank(ti2)
    dest, blk_e = _expert_dest(cnt, ti2, rank, n_blk)
    dest_flat = dest[:, :TOP_K].reshape(A)
    x_rows = _scatter_rows(dest_flat, n2.reshape(T * SUBLANES, LANES), jnp.zeros((P * SUBLANES, LANES), F32))
    y_rows = _expert_ffn(blk_e[:, 0], blk_e[:, 1], x_rows, exp_w1, exp_b1.reshape(N_EXPERTS, 1, -1),
                         exp_w2, exp_b2.reshape(N_EXPERTS, 1, -1))
    return _combine_rows(dest_flat, tg.reshape(T, LANES), y_rows, h1, gt2, final_g, normalize=last_layer)


def kernel(x, c, ada_w, ada_b, norm1_g, w_in, nsa_pe_k, nsa_pe_v, nsa_cmp_w1_k, nsa_cmp_w2_k, nsa_cmp_w1_v, nsa_cmp_w2_v, ml_conv_w, ml_conv_b, ml_gate_b, ml_norm_g, w_up_nsa, w_up_ml, w_out, norm2_g, router_w, router_b, exp_w1, exp_b1, exp_w2, exp_b2, final_g):
    B, S, D = x.shape
    depth = ada_w.shape[0]
    h = x
    for layer in range(depth):
        mod = _modulation(c, ada_w[layer], ada_b[layer])
        mods = [mod[:, i * D:(i + 1) * D].reshape(B, 1, D) for i in range(6)]
        h = _layer(h, mods, norm1_g[layer], w_in[layer], nsa_pe_k[layer], nsa_pe_v[layer],
                   nsa_cmp_w1_k[layer], nsa_cmp_w2_k[layer], nsa_cmp_w1_v[layer], nsa_cmp_w2_v[layer],
                   ml_conv_w[layer], ml_conv_b[layer], ml_gate_b[layer], ml_norm_g[layer],
                   w_up_nsa[layer], w_up_ml[layer], w_out[layer], norm2_g[layer], router_w[layer],
                   router_b[layer], exp_w1[layer], exp_b1[layer], exp_w2[layer], exp_b2[layer],
                   final_g.reshape(1, D), layer + 1 == depth)
    return h
```

```python
import functools

import numpy as np
import jax
import jax.numpy as jnp
from jax import lax
from jax.experimental import pallas as pl
from jax.experimental.pallas import tpu as pltpu

F32 = jnp.float32
BF16 = jnp.bfloat16
I32 = jnp.int32
HIGHEST = lax.Precision.HIGHEST

D_MODEL = 1024
NSA_HEADS = 8
NSA_GROUPS = 2
NSA_REP = NSA_HEADS // NSA_GROUPS
NSA_DH = 64
NSA_SCALE = NSA_DH ** -0.5
Q_PRESCALE = NSA_SCALE * float(np.log2(np.e))
ONES_ROWS = 16
ISSUE_UNROLL = 8
DRAIN_UNROLL = 16
CMP_STRIDE = 16
CMP_LEN = 32
CMP_HIDDEN = 128
SEL_BLOCK = 64
SEL_TOPN = 16
WINDOW = 512
Q_BLOCK = 128
ML_HEADS = 4
ML_DH = 128
ML_CHUNK = 128
CONV_WIDTH = 4
N_EXPERTS = 32
TOP_K = 4
D_EXPERT = 1024
SWIGLU_LIMIT = 7.0
SWIGLU_ALPHA = 1.702
MOE_ROWS = 256
NORM_EPS = 1e-6
NEG_INF = -1e30
FORCE_BONUS = 1e4

NSA_QW = NSA_HEADS * NSA_DH
NSA_KVW = NSA_GROUPS * NSA_DH
ML_W = ML_HEADS * ML_DH
LANES = 128
SUBLANES = 8
VMEM_LIMIT = 52 * 1024 * 1024

_C_Q = (0, 512)
_C_KVC = (512, 768)
_C_KVS = (768, 1280)
_C_MQK = (1280, 2304)
_C_MV = (2304, 2816)
_C_MO = (2816, 3328)
_C_MG = (3328, 5376)
_C_SM = (5376, 5504)
IN_WR = 5504


def _sigmoid(x):
    return 1.0 / (1.0 + jnp.exp(-x))


def _iota(shape, dim):
    return lax.broadcasted_iota(I32, shape, dim)


def _cparams(sem, vmem=None):
    return pltpu.CompilerParams(dimension_semantics=sem, vmem_limit_bytes=vmem)


def _tiles_to_rows(ref, n, lead=()):
    return jnp.concatenate([ref[lead + (pl.ds(s, n, stride=SUBLANES), slice(None))] for s in range(SUBLANES)],
                           axis=1)


def _rows_to_tiles(ref, x, lead=()):
    n = x.shape[0]
    for s in range(SUBLANES):
        ref[lead + (pl.ds(s, n, stride=SUBLANES), slice(None))] = x[:, s * LANES:(s + 1) * LANES]


def _mod_kernel(c_ref, w_ref, b_ref, o_ref):
    c = c_ref[...]
    ca = c * _sigmoid(c)
    o_ref[...] = jnp.dot(ca, w_ref[...], precision=HIGHEST, preferred_element_type=F32) + b_ref[...]


def _modulation(c, ada_w, ada_b):
    B, D = c.shape
    n = ada_w.shape[1] // D
    return pl.pallas_call(
        _mod_kernel,
        out_shape=jax.ShapeDtypeStruct((B, n * D), F32),
        grid=(n,),
        in_specs=[pl.BlockSpec((B, D), lambda j: (0, 0)),
                  pl.BlockSpec((D, D), lambda j: (0, j)),
                  pl.BlockSpec((1, D), lambda j: (0, j))],
        out_specs=pl.BlockSpec((B, D), lambda j: (0, j)),
        compiler_params=_cparams(("parallel",)),
        name="modulation",
    )(c, ada_w, ada_b.reshape(1, n * D))


def _inproj_kernel(x_ref, g_ref, sc_ref, sh_ref, w_ref,
                   q_ref, kvc_ref, kvs_ref, mqk_ref, mv_ref, mo_ref, mg_ref, sm_ref):
    x = x_ref[0]
    y = x * lax.rsqrt(jnp.mean(x * x, axis=-1, keepdims=True) + NORM_EPS) * g_ref[...]
    n = (y * (1.0 + sc_ref[0]) + sh_ref[0]).astype(BF16)

    def proj(c):
        return jnp.dot(n, w_ref[:, c[0]:c[1]], preferred_element_type=F32)

    q_ref[0] = (proj(_C_Q) * Q_PRESCALE).astype(BF16)
    kvc_ref[0] = proj(_C_KVC)
    kvs_ref[0] = proj(_C_KVS).astype(BF16)
    mqk_ref[0] = proj(_C_MQK)
    mv_ref[0] = proj(_C_MV)
    mo_ref[0] = proj(_C_MO)
    mg_ref[0] = proj(_C_MG)
    sm_ref[0] = proj(_C_SM)


def _in_projection(x, g, sc, sh, w_r, tm=256):
    B, S, D = x.shape
    widths = [c[1] - c[0] for c in (_C_Q, _C_KVC, _C_KVS, _C_MQK, _C_MV, _C_MO, _C_MG, _C_SM)]
    dtypes = [BF16, F32, BF16, F32, F32, F32, F32, F32]
    vec = pl.BlockSpec((1, 1, D), lambda b, i: (b, 0, 0))
    return pl.pallas_call(
        _inproj_kernel,
        out_shape=[jax.ShapeDtypeStruct((B, S, w), dt) for w, dt in zip(widths, dtypes)],
        grid=(B, S // tm),
        in_specs=[pl.BlockSpec((1, tm, D), lambda b, i: (b, i, 0)),
                  pl.BlockSpec((1, D), lambda b, i: (0, 0)),
                  vec, vec,
                  pl.BlockSpec((D, IN_WR), lambda b, i: (0, 0))],
        out_specs=[pl.BlockSpec((1, tm, w), lambda b, i: (b, i, 0)) for w in widths],
        compiler_params=_cparams(("parallel", "parallel"), VMEM_LIMIT),
        name="in_projection",
    )(x, g, sc, sh, w_r)


def _compress_kernel(t_ref, pe_ref, w1_ref, w2_ref, o_ref):
    half = CMP_STRIDE * NSA_DH
    t = t_ref[0, 0]
    pe = pe_ref[0]
    nb = t.shape[0]
    a = jnp.dot((t + pe[0:1]).astype(BF16), w1_ref[0, :half, :].astype(BF16), preferred_element_type=F32)
    b = jnp.dot((t + pe[1:2]).astype(BF16), w1_ref[0, half:, :].astype(BF16), preferred_element_type=F32)
    h = a + pltpu.roll(b, nb - 1, 0)
    gl = 0.5 * h * (1.0 + jnp.tanh(np.sqrt(2.0 / np.pi).astype(np.float32) * (h + 0.044715 * (h * h * h))))
    o_ref[0, 0] = jnp.dot(gl.astype(BF16), w2_ref[0].astype(BF16), preferred_element_type=F32)


def _compress(t16, pe, w1, w2):
    B, _, nb, wdt = t16.shape
    return pl.pallas_call(
        _compress_kernel,
        out_shape=jax.ShapeDtypeStruct((B, 4, nb, NSA_DH), F32),
        grid=(B, 4),
        in_specs=[pl.BlockSpec((1, 1, nb, wdt), lambda b, j: (b, j, 0, 0)),
                  pl.BlockSpec((1, 2, wdt), lambda b, j: (j // 2, 0, 0)),
                  pl.BlockSpec((1, CMP_LEN * NSA_DH, CMP_HIDDEN), lambda b, j: (j // 2, 0, 0)),
                  pl.BlockSpec((1, CMP_HIDDEN, NSA_DH), lambda b, j: (j // 2, 0, 0))],
        out_specs=pl.BlockSpec((1, 1, nb, NSA_DH), lambda b, j: (b, j, 0, 0)),
        compiler_params=_cparams(("parallel", "parallel")),
        name="nsa_compress",
    )(t16, pe, w1, w2)


def _nsa_kernel(q_ref, kc_ref, vct_ref, ovt_ref, ks_ref, vst_ref, kw_ref, vwt_ref, sm_ref, o_ref, *, n_sel, tk):
    qb = Q_BLOCK
    ncp = kc_ref.shape[1]
    ns = ovt_ref.shape[0]
    q0 = pl.program_id(1) * qb
    t_l = q0 + _iota((1, qb), 1)
    sub = _iota((LANES, qb), 0)
    gT = _sigmoid(sm_ref[0].T)

    gw = NSA_REP * qb
    q_heads = []
    for c in range(NSA_HEADS // 2):
        qc = q_ref[0, :, c * LANES:(c + 1) * LANES].astype(F32).T
        for e in range(2):
            g = (2 * c + e) // NSA_REP
            x = qc if e == g else pltpu.roll(qc, NSA_DH, 0)
            q_heads.append(jnp.where((sub >> 6) == g, x, 0.0).astype(BF16))
    qT = [jnp.concatenate(q_heads[g * NSA_REP:(g + 1) * NSA_REP], axis=1) for g in range(NSA_GROUPS)]
    t_g = q0 + (_iota((1, gw), 1) & (qb - 1))

    def head_cols(x, r):
        return x[:, r * qb:(r + 1) * qb]

    def gate_row(branch, g):
        rows = [gT[branch * NSA_HEADS + g * NSA_REP + r:branch * NSA_HEADS + g * NSA_REP + r + 1]
                for r in range(NSA_REP)]
        return jnp.concatenate(rows, axis=1)

    kc = kc_ref[0]
    vct = vct_ref[0]
    cmask = (_iota((ncp, gw), 0) * CMP_STRIDE + (CMP_LEN - 1)) <= t_g
    blk = _iota((ns, qb), 0)
    blk_f = blk.astype(F32)
    cur = t_l >> 6
    forced = (blk == 0) | (blk == cur) | (blk == cur - 1)
    ocT = []
    selT = []
    for g in range(NSA_GROUPS):
        s = jnp.dot(kc, qT[g], preferred_element_type=F32)
        s = jnp.where(cmask, s, NEG_INF)
        e = jnp.exp2(s - jnp.max(s, axis=0, keepdims=True))
        p = e * (1.0 / jnp.sum(e, axis=0, keepdims=True))
        p = jnp.where(cmask, p, 0.0)
        ocT.append(jnp.dot(vct, p.astype(BF16), preferred_element_type=F32))
        psum = head_cols(p, 0)
        for r in range(1, NSA_REP):
            psum = psum + head_cols(p, r)
        p_hi = psum.astype(BF16)
        rest = psum - p_hi.astype(F32)
        p_mid = rest.astype(BF16)
        p_lo = (rest - p_mid.astype(F32)).astype(BF16)
        imp = (jnp.dot(ovt_ref[...], p_hi, preferred_element_type=F32)
               + (jnp.dot(ovt_ref[...], p_mid, preferred_element_type=F32)
                  + jnp.dot(ovt_ref[...], p_lo, preferred_element_type=F32)))
        imp = jnp.where(blk > cur, NEG_INF, imp + jnp.where(forced, FORCE_BONUS, 0.0))
        sel = jnp.zeros((ns, qb), F32)
        for _ in range(n_sel):
            mx = jnp.max(imp, axis=0, keepdims=True)
            first = jnp.min(jnp.where(imp == mx, blk_f, float(ns)), axis=0, keepdims=True)
            pick = blk_f == first
            sel = jnp.where(pick, 1.0, sel)
            imp = jnp.where(pick, -jnp.inf, imp)
        selT.append(sel.astype(BF16))

    wlen = WINDOW + qb
    wt0 = jnp.maximum(q0 - WINDOW, 0) // LANES
    wstart = pl.multiple_of(wt0 * LANES, LANES)
    kwin = kw_ref[0, pl.ds(wstart, wlen), :]
    rel = t_g - (wstart + _iota((wlen, gw), 0))
    wmask = lax.bitcast_convert_type(rel, jnp.uint32) < WINDOW
    partial = []
    for g in range(NSA_GROUPS):
        lo = g * NSA_DH
        s = jnp.dot(kwin, qT[g], preferred_element_type=F32)
        s = jnp.where(wmask, s, NEG_INF)
        pb = jnp.exp2(s - jnp.max(s, axis=0, keepdims=True)).astype(BF16)
        ow = jnp.dot(vwt_ref[0, wt0], pb[:LANES], preferred_element_type=F32)
        for i in range(1, wlen // LANES):
            ow = ow + jnp.dot(vwt_ref[0, wt0 + i], pb[i * LANES:(i + 1) * LANES], preferred_element_type=F32)
        ow = ow[:LANES] * (1.0 / ow[LANES:LANES + 1])
        partial.append(gate_row(0, g) * ocT[g][lo:lo + NSA_DH] + gate_row(2, g) * ow[lo:lo + NSA_DH])

    nt = (q0 + qb + tk - 1) // tk

    def tile_step(r0, j, lane0, width, diagonal, carry):
        kt = ks_ref[0, pl.ds(r0, width), :]
        expand = jnp.where(((r0 + _iota((width, ns), 0)) >> 6) == _iota((width, ns), 1), 1.0, 0.0).astype(BF16)
        new = []
        for g in range(NSA_GROUPS):
            m, l, acc = carry[g]
            hit = jnp.dot(expand, selT[g], preferred_element_type=F32)
            if diagonal:
                hit = jnp.where((r0 + _iota((width, qb), 0)) <= t_l, hit, 0.0)
            ok = hit > 0.5
            s = jnp.dot(kt, qT[g], preferred_element_type=F32)
            s = jnp.concatenate([jnp.where(ok, head_cols(s, r), NEG_INF) for r in range(NSA_REP)], axis=1)
            m_new = jnp.maximum(m, jnp.max(s, axis=0, keepdims=True))
            alpha = jnp.exp2(m - m_new)
            pv = jnp.dot(vst_ref[0, j, :, lane0:lane0 + width], jnp.exp2(s - m_new).astype(BF16),
                         preferred_element_type=F32)
            new.append((m_new, alpha * l + pv[LANES:LANES + 1], alpha * acc + pv[:LANES]))
        return tuple(new)

    init = tuple((jnp.full((1, gw), NEG_INF, F32), jnp.zeros((1, gw), F32), jnp.zeros((LANES, gw), F32))
                 for _ in range(NSA_GROUPS))
    final = lax.fori_loop(0, nt - 1, lambda j, c: tile_step(pl.multiple_of(j * tk, tk), j, 0, tk, False, c), init)
    half = tk // 2
    base = pl.multiple_of((nt - 1) * tk, tk)
    final = tile_step(base, nt - 1, 0, half, True, final)
    final = lax.cond(q0 + qb > base + half,
                     lambda c: tile_step(pl.multiple_of(base + half, half), nt - 1, half, half, True, c),
                     lambda c: c, final)

    heads = []
    for g in range(NSA_GROUPS):
        _, l, acc = final[g]
        o_g = partial[g] + gate_row(1, g) * (acc[g * NSA_DH:(g + 1) * NSA_DH] * (1.0 / l))
        heads += [head_cols(o_g, r) for r in range(NSA_REP)]
    for c in range(NSA_HEADS // 2):
        o_ref[0, :, c * LANES:(c + 1) * LANES] = jnp.concatenate(heads[2 * c:2 * c + 2], axis=0).T


def _nsa_attention(q, kc, vct, ovt, kvs, vst, vwt, small, n_sel):
    B, S, _ = q.shape
    ns, ncp = ovt.shape
    tk = vst.shape[3]
    assert S >= WINDOW + Q_BLOCK and S % tk == 0 and tk % LANES == 0
    nt128 = S // LANES

    def whole(shape):
        return pl.BlockSpec((1,) + shape, lambda b, i: (b,) + (0,) * len(shape))

    return pl.pallas_call(
        functools.partial(_nsa_kernel, n_sel=n_sel, tk=tk),
        out_shape=jax.ShapeDtypeStruct((B, S, NSA_QW), F32),
        grid=(B, S // Q_BLOCK),
        in_specs=[pl.BlockSpec((1, Q_BLOCK, NSA_QW), lambda b, i: (b, i, 0)),
                  whole((ncp, LANES)), whole((LANES, ncp)),
                  pl.BlockSpec((ns, ncp), lambda b, i: (0, 0)),
                  pl.BlockSpec((1, S, LANES), lambda b, i: (b, 0, 0)),
                  whole((S // tk, LANES + ONES_ROWS, tk)),
                  pl.BlockSpec((1, S, LANES), lambda b, i: (b, 0, 2)),
                  whole((nt128, LANES + ONES_ROWS, LANES)),
                  pl.BlockSpec((1, Q_BLOCK, LANES), lambda b, i: (b, i, 0))],
        out_specs=pl.BlockSpec((1, Q_BLOCK, NSA_QW), lambda b, i: (b, i, 0)),
        compiler_params=_cparams(("parallel", "parallel"), VMEM_LIMIT),
        name="nsa_attention",
    )(q, kc, vct, ovt, kvs, vst, kvs, vwt, small)


def _mlstm_kernel(gb_ref, q_ref, k_ref, v_ref, o_ref, grow_ref, gcol_ref, cw_ref, cb_ref, ng_ref, out_ref,
                  c_scr, n_scr, m_scr, hq_scr, hk_scr):
    L = ML_CHUNK
    H = ML_HEADS
    ts = q_ref.shape[1]
    kscale = ML_DH ** -0.5
    row = _iota((L, L), 0)
    col = _iota((L, L), 1)
    tri = row >= col
    halo_rows = SUBLANES

    @pl.when(pl.program_id(1) == 0)
    def _():
        c_scr[...] = jnp.zeros_like(c_scr)
        n_scr[...] = jnp.zeros_like(n_scr)
        m_scr[...] = jnp.zeros_like(m_scr)
        hq_scr[...] = jnp.zeros_like(hq_scr)
        hk_scr[...] = jnp.zeros_like(hk_scr)

    def conv_silu(x_ref, prev_ref, woff, h, c, r0):
        lanes = slice(h * ML_DH, (h + 1) * ML_DH)
        wl = slice(woff + h * ML_DH, woff + (h + 1) * ML_DH)
        main = x_ref[0, pl.ds(r0, L), lanes]
        h0 = pl.multiple_of(jnp.maximum(r0 - halo_rows, 0), halo_rows)
        halo = jnp.where(c > 0, x_ref[0, pl.ds(h0, halo_rows), lanes], prev_ref[:, lanes])
        cat = jnp.concatenate([halo, main], axis=0)
        y = main * cw_ref[CONV_WIDTH - 1:CONV_WIDTH, wl] + cb_ref[:, wl]
        for w in range(CONV_WIDTH - 1):
            sft = CONV_WIDTH - 1 - w
            y = y + pltpu.roll(cat, sft, 0)[halo_rows:, :] * cw_ref[w:w + 1, wl]
        return y * _sigmoid(y)

    def logsig(x):
        return -(jnp.maximum(-x, 0.0) + jnp.log(1.0 + jnp.exp(-jnp.abs(x))))

    def head_step(h, c, r0, C, n, m):
        lanes = slice(h * ML_DH, (h + 1) * ML_DH)
        gb_i = gb_ref[h]
        gb_f = gb_ref[H + h]
        qc = conv_silu(q_ref, hq_scr, 0, h, c, r0)
        kc = conv_silu(k_ref, hk_scr, ML_W, h, c, r0) * kscale
        vc = v_ref[0, pl.ds(r0, L), lanes]
        li_row = grow_ref[0, h, pl.ds(c, 1), :] + gb_i
        lf_row = logsig(grow_ref[0, H + h, pl.ds(c, 1), :] + gb_f)
        li_col = gcol_ref[0, h, pl.ds(r0, L), :] + gb_i
        lf_col = logsig(gcol_ref[0, H + h, pl.ds(r0, L), :] + gb_f)
        b_col = jnp.sum(jnp.where(tri, lf_row, 0.0), axis=1, keepdims=True)
        b_row = jnp.sum(jnp.where(row <= col, lf_col, 0.0), axis=0, keepdims=True)
        Dm = jnp.where(tri, b_col - b_row + li_row, NEG_INF)
        inter = b_col + m
        m_t = jnp.maximum(inter, jnp.max(Dm, axis=1, keepdims=True))
        Dw = jnp.exp(Dm - m_t)
        inter_w = jnp.exp(inter - m_t)
        qb16 = qc.astype(BF16)
        kb16 = kc.astype(BF16)
        vb16 = vc.astype(BF16)
        qk = lax.dot_general(qb16, kb16, (((1,), (1,)), ((), ())), preferred_element_type=F32) * Dw
        num = (jnp.dot(qk.astype(BF16), vb16, preferred_element_type=F32)
               + inter_w * jnp.dot(qb16, C.astype(BF16), preferred_element_type=F32))
        den = jnp.sum(qk, axis=1, keepdims=True) + inter_w * jnp.sum(qc * n, axis=1, keepdims=True)
        hc = num / jnp.maximum(jnp.abs(den), jnp.exp(-m_t))
        bL = b_col[L - 1:L, :]
        a_col = bL - b_col + li_col
        m_new = jnp.maximum(bL + m, jnp.max(a_col, axis=0, keepdims=True))
        aw = jnp.exp(a_col - m_new)
        decay = jnp.exp(bL + m - m_new)
        awk = aw * kc
        C_new = decay * C + jnp.dot(awk.T.astype(BF16), vb16, preferred_element_type=F32)
        n_new = decay * n + jnp.sum(awk, axis=0, keepdims=True)
        hn = hc * lax.rsqrt(jnp.mean(hc * hc, axis=1, keepdims=True) + NORM_EPS) * ng_ref[:, lanes]
        out_ref[0, pl.ds(r0, L), lanes] = hn * _sigmoid(o_ref[0, pl.ds(r0, L), lanes])
        return C_new, n_new, m_new

    def step(c, carry):
        r0 = pl.multiple_of(c * L, L)
        return tuple(head_step(h, c, r0, *carry[h]) for h in range(H))

    init = tuple((c_scr[h], n_scr[h], m_scr[h][:, 0:1]) for h in range(H))
    final = lax.fori_loop(0, ts // L, step, init)
    for h in range(H):
        C, n, m = final[h]
        c_scr[h] = C
        n_scr[h] = n
        m_scr[h] = jnp.broadcast_to(m, (1, LANES))
    hq_scr[...] = q_ref[0, ts - halo_rows:ts, :]
    hk_scr[...] = k_ref[0, ts - halo_rows:ts, :]


def _mlstm(ml_qk, ml_v, ml_o, gate_row, gate_col, gate_b, conv_w, conv_b, norm_g, ts=1024):
    B, S, _ = ml_v.shape
    H = ML_HEADS
    ts = min(ts, S)

    def rows(cb):
        return pl.BlockSpec((1, ts, ML_W), lambda b, i, gb: (b, i, cb))

    def full(shape):
        return pl.BlockSpec(shape, lambda b, i, gb: (0,) * len(shape))

    grid_spec = pltpu.PrefetchScalarGridSpec(
        num_scalar_prefetch=1,
        grid=(B, S // ts),
        in_specs=[rows(0), rows(1), rows(0), rows(0),
                  pl.BlockSpec((1, 2 * H, ts // ML_CHUNK, ML_CHUNK), lambda b, i, gb: (b, 0, i, 0)),
                  pl.BlockSpec((1, 2 * H, ts, 1), lambda b, i, gb: (b, 0, i, 0)),
                  full((CONV_WIDTH, 2 * ML_W)), full((1, 2 * ML_W)), full((1, ML_W))],
        out_specs=rows(0),
        scratch_shapes=[pltpu.VMEM((H, ML_DH, ML_DH), F32), pltpu.VMEM((H, 1, ML_DH), F32),
                        pltpu.VMEM((H, 1, LANES), F32),
                        pltpu.VMEM((SUBLANES, ML_W), F32), pltpu.VMEM((SUBLANES, ML_W), F32)],
    )
    return pl.pallas_call(
        _mlstm_kernel,
        out_shape=jax.ShapeDtypeStruct((B, S, ML_W), F32),
        grid_spec=grid_spec,
        compiler_params=_cparams(("parallel", "arbitrary"), VMEM_LIMIT),
        name="mlstm",
    )(gate_b, ml_qk, ml_qk, ml_v, ml_o, gate_row, gate_col, conv_w, conv_b, norm_g)


def _pack_cols(cols, dtype):
    tm = cols[0].shape[0]
    lane = _iota((tm, LANES), 1)
    out = jnp.zeros((tm, LANES), dtype)
    for k, c in enumerate(cols):
        out = jnp.where(lane == k, c, out)
    return out


def _outproj_kernel(a_ref, b_ref, mg_ref, x_ref, gt_ref, sc_ref, sh_ref, g2_ref,
                    wa_ref, wb_ref, wo_ref, wr_ref, br_ref,
                    h_ref, n_ref, ti_ref, tg_ref):
    D = x_ref.shape[2]
    ua = jnp.dot(a_ref[0].astype(BF16), wa_ref[...], preferred_element_type=F32)
    ub = jnp.dot(b_ref[0].astype(BF16), wb_ref[...], preferred_element_type=F32)
    u = _sigmoid(mg_ref[0, :, :D]) * ua + _sigmoid(mg_ref[0, :, D:]) * ub
    mix = jnp.dot(u.astype(BF16), wo_ref[...], preferred_element_type=F32)
    h = x_ref[0] + (1.0 + gt_ref[0]) * mix
    h_ref[0] = h
    y = h * lax.rsqrt(jnp.mean(h * h, axis=-1, keepdims=True) + NORM_EPS) * g2_ref[...]
    n = y * (1.0 + sc_ref[0]) + sh_ref[0]
    _rows_to_tiles(n_ref, n, lead=(0,))
    n_hi = n.astype(BF16)
    n_lo = (n - n_hi.astype(F32)).astype(BF16)
    logits = (jnp.dot(n_hi, wr_ref[0], preferred_element_type=F32)
              + (jnp.dot(n_hi, wr_ref[1], preferred_element_type=F32)
                 + jnp.dot(n_lo, wr_ref[0], preferred_element_type=F32))) + br_ref[...]
    tm, ne = logits.shape
    lane = _iota((tm, ne), 1).astype(F32)
    work = logits
    vals, idxs = [], []
    for _ in range(TOP_K):
        mx = jnp.max(work, axis=-1, keepdims=True)
        ix = jnp.min(jnp.where(work == mx, lane, float(ne)), axis=-1, keepdims=True)
        vals.append(mx)
        idxs.append(ix)
        work = jnp.where(lane == ix, -jnp.inf, work)
    es = [jnp.exp(v - vals[0]) for v in vals]
    tot = es[0] + es[1] + es[2] + es[3]
    ti_ref[0] = _pack_cols(idxs, F32).astype(I32)
    tg_ref[0] = _pack_cols([e / tot for e in es], F32)


def _out_projection(o_nsa, o_ml, merge, x, gt1, sc2, sh2, g2, wa, wb, wo, wr, br, tm=256):
    B, S, D = x.shape
    vec = pl.BlockSpec((1, 1, D), lambda b, i: (b, 0, 0))

    def full(shape):
        return pl.BlockSpec(shape, lambda b, i: (0,) * len(shape))

    def rows(w):
        return pl.BlockSpec((1, tm, w), lambda b, i: (b, i, 0))

    assert D == SUBLANES * LANES
    return pl.pallas_call(
        _outproj_kernel,
        out_shape=[jax.ShapeDtypeStruct((B, S, D), F32), jax.ShapeDtypeStruct((B, S * SUBLANES, LANES), F32),
                   jax.ShapeDtypeStruct((B, S, LANES), I32), jax.ShapeDtypeStruct((B, S, LANES), F32)],
        grid=(B, S // tm),
        in_specs=[rows(NSA_QW), rows(ML_W), rows(2 * D), rows(D), vec, vec, vec, full((1, D)),
                  full(wa.shape), full(wb.shape), full(wo.shape), full(wr.shape), full((1, N_EXPERTS))],
        out_specs=[rows(D), pl.BlockSpec((1, tm * SUBLANES, LANES), lambda b, i: (b, i, 0)), rows(LANES),
                   rows(LANES)],
        compiler_params=_cparams(("parallel", "parallel"), VMEM_LIMIT),
        name="out_projection_router",
    )(o_nsa, o_ml, merge, x, gt1, sc2, sh2, g2, wa, wb, wo, wr, br)


def _rank_kernel(ti_ref, rank_ref, cnt_ref, carry_ref):
    tt = ti_ref.shape[0]

    @pl.when(pl.program_id(0) == 0)
    def _():
        carry_ref[...] = jnp.zeros_like(carry_ref)

    ti = ti_ref[...]
    lane = _iota((tt, LANES), 1)
    hots = [ti[:, k:k + 1] == lane for k in range(TOP_K)]
    member = jnp.zeros((tt, LANES), F32)
    for hot in hots:
        member = member + jnp.where(hot, 1.0, 0.0)
    strict = jnp.where(_iota((tt, tt), 0) > _iota((tt, tt), 1), 1.0, 0.0).astype(BF16)
    before = jnp.dot(strict, member.astype(BF16), preferred_element_type=F32) + carry_ref[0:1, :]
    ranks = [jnp.sum(jnp.where(hot, before, 0.0), axis=-1, keepdims=True) for hot in hots]
    rank_ref[...] = _pack_cols(ranks, F32)
    carry_ref[...] = carry_ref[...] + jnp.sum(member, axis=0, keepdims=True)
    cnt_ref[...] = carry_ref[...]


def _expert_rank(ti, tt=512):
    T = ti.shape[0]
    return pl.pallas_call(
        _rank_kernel,
        out_shape=[jax.ShapeDtypeStruct((T, LANES), F32), jax.ShapeDtypeStruct((SUBLANES, LANES), F32)],
        grid=(T // tt,),
        in_specs=[pl.BlockSpec((tt, LANES), lambda i: (i, 0))],
        out_specs=[pl.BlockSpec((tt, LANES), lambda i: (i, 0)),
                   pl.BlockSpec((SUBLANES, LANES), lambda i: (0, 0))],
        scratch_shapes=[pltpu.VMEM((SUBLANES, LANES), F32)],
        compiler_params=_cparams(("arbitrary",)),
        name="moe_rank",
    )(ti)


def _dest_kernel(cnt_ref, ti_ref, rank_ref, dest_ref, be_ref, pad_ref):
    tt = ti_ref.shape[0]
    nbp = be_ref.shape[0]
    cnt = cnt_ref[...].astype(I32)
    padded = ((cnt + (MOE_ROWS - 1)) >> 8) << 8
    lane8 = _iota((SUBLANES, LANES), 1)
    pad_end = padded
    sft = 1
    while sft < LANES:
        pad_end = pad_end + jnp.where(lane8 >= sft, pltpu.roll(pad_end, sft, 1), 0)
        sft *= 2
    pad_start = (pad_end - padded)[0:1, :].astype(F32)
    ti = ti_ref[...]
    lane = _iota((tt, LANES), 1)
    rank = rank_ref[...]
    dests = [jnp.sum(jnp.where(ti[:, k:k + 1] == lane, pad_start, 0.0), axis=-1, keepdims=True) + rank[:, k:k + 1]
             for k in range(TOP_K)]
    dest_ref[...] = _pack_cols(dests, F32).astype(I32) * SUBLANES
    first_row = _iota((nbp, LANES), 0) * MOE_ROWS
    done = (pad_end[0:1, :] <= first_row) & (_iota((nbp, LANES), 1) < N_EXPERTS)
    be = jnp.minimum(jnp.sum(jnp.where(done, 1.0, 0.0), axis=-1, keepdims=True), float(N_EXPERTS - 1))
    total = pad_end[:, N_EXPERTS - 1:N_EXPERTS]
    used = first_row < total[0:1]
    be_ref[...] = jnp.where(_iota((nbp, LANES), 1) == 1, jnp.where(used, 1, 0), be.astype(I32))
    prow = _iota((SUBLANES, LANES), 0)
    pad_ref[...] = jnp.where(prow == 0, (pad_end - padded + cnt) * SUBLANES,
                             jnp.where(prow == 1, padded - cnt, jnp.broadcast_to(total, (SUBLANES, LANES))))


def _expert_dest(cnt, ti, rank, n_blk, tt=1024):
    T = ti.shape[0]
    return pl.pallas_call(
        _dest_kernel,
        out_shape=[jax.ShapeDtypeStruct((T, LANES), I32), jax.ShapeDtypeStruct((n_blk, LANES), I32),
                   jax.ShapeDtypeStruct((SUBLANES, LANES), I32)],
        grid=(T // tt,),
        in_specs=[pl.BlockSpec((SUBLANES, LANES), lambda i: (0, 0)),
                  pl.BlockSpec((tt, LANES), lambda i: (i, 0)),
                  pl.BlockSpec((tt, LANES), lambda i: (i, 0))],
        out_specs=[pl.BlockSpec((tt, LANES), lambda i: (i, 0)),
                   pl.BlockSpec((n_blk, LANES), lambda i: (0, 0)),
                   pl.BlockSpec((SUBLANES, LANES), lambda i: (0, 0))],
        compiler_params=_cparams(("arbitrary",)),
        name="moe_dest",
    )(cnt, ti, rank)


def _scatter_kernel(dest_ref, pad_ref, src_ref, dst_ref, zero_ref, sem, zsems):
    tt = src_ref.shape[0] // SUBLANES
    zrows = zero_ref.shape[0]
    n_tail = (dst_ref.shape[0] - pad_ref[2, 0] * SUBLANES) // zrows

    def zero_row(e, r):
        off = pl.multiple_of(pad_ref[0, e] + r * SUBLANES, SUBLANES)
        return pltpu.make_async_copy(zero_ref.at[pl.ds(0, SUBLANES)], dst_ref.at[pl.ds(off, SUBLANES)], zsems.at[0])

    def zero_block(b):
        off = pl.multiple_of(pad_ref[2, 0] * SUBLANES + b * zrows, zrows)
        return pltpu.make_async_copy(zero_ref, dst_ref.at[pl.ds(off, zrows)], zsems.at[1])

    def zero_fill(wait):
        def per_expert(e, c):
            def one(r, c2):
                zero_row(e, r).wait() if wait else zero_row(e, r).start()
                return c2
            return lax.fori_loop(0, pad_ref[1, e], one, c)
        lax.fori_loop(0, N_EXPERTS, per_expert, 0)

        def tail(b, c):
            zero_block(b).wait() if wait else zero_block(b).start()
            return c
        lax.fori_loop(0, n_tail, tail, 0)

    @pl.when(pl.program_id(0) == 0)
    def _():
        zero_ref[...] = jnp.zeros_like(zero_ref)
        zero_fill(wait=False)

    def copy(t, k):
        return pltpu.make_async_copy(
            src_ref.at[pl.ds(pl.multiple_of(t * SUBLANES, SUBLANES), SUBLANES)],
            dst_ref.at[pl.ds(pl.multiple_of(dest_ref[t * TOP_K + k], SUBLANES), SUBLANES)], sem)

    def issue(t, c):
        for k in range(TOP_K):
            copy(t, k).start()
        return c

    lax.fori_loop(0, tt, issue, 0, unroll=ISSUE_UNROLL)

    def drain(t, c):
        for k in range(TOP_K):
            copy(t, k).wait()
        return c

    lax.fori_loop(0, tt, drain, 0, unroll=DRAIN_UNROLL)

    @pl.when(pl.program_id(0) == 0)
    def _():
        zero_fill(wait=True)


def _scatter_rows(dest_flat, pad_info, rows, n_rows, tt=512):
    T = rows.shape[0] // SUBLANES
    return pl.pallas_call(
        _scatter_kernel,
        out_shape=jax.ShapeDtypeStruct((n_rows * SUBLANES, LANES), rows.dtype),
        grid=(T // tt,),
        in_specs=[pl.BlockSpec((tt * TOP_K,), lambda i: (i,), memory_space=pltpu.SMEM),
                  pl.BlockSpec((SUBLANES, LANES), lambda i: (0, 0), memory_space=pltpu.SMEM),
                  pl.BlockSpec((tt * SUBLANES, LANES), lambda i: (i, 0))],
        out_specs=pl.BlockSpec(memory_space=pl.ANY),
        scratch_shapes=[pltpu.VMEM((MOE_ROWS * SUBLANES, LANES), rows.dtype), pltpu.SemaphoreType.DMA,
                        pltpu.SemaphoreType.DMA((2,))],
        compiler_params=_cparams(("arbitrary",)),
        name="moe_scatter_rows",
    )(dest_flat, pad_info, rows)


def _expert_kernel(be_ref, used_ref, x_ref, w1_hbm, b1_ref, w2_hbm, b2_ref, y_ref,
                   w1f_ref, w2f_ref, w1b_ref, w2b_ref, sems, slot_ref):
    i = pl.program_id(0)
    nb = pl.num_programs(0)
    e = be_ref[i]

    def fetch(expert, slot):
        return (pltpu.make_async_copy(w1_hbm.at[expert], w1f_ref.at[slot], sems.at[0, slot]),
                pltpu.make_async_copy(w2_hbm.at[expert], w2f_ref.at[slot], sems.at[1, slot]))

    @pl.when(i == 0)
    def _():
        slot_ref[0] = 0
        for cp in fetch(e, 0):
            cp.start()

    @pl.when(jnp.logical_or(i == 0, e != be_ref[jnp.maximum(i - 1, 0)]))
    def _():
        slot = slot_ref[0]
        for cp in fetch(e, slot):
            cp.wait()
        w1b_ref[...] = w1f_ref[slot].astype(BF16)
        w2b_ref[...] = w2f_ref[slot].astype(BF16)
        j = lax.while_loop(lambda j: jnp.logical_and(j < nb, be_ref[jnp.minimum(j, nb - 1)] == e),
                           lambda j: j + 1, i + 1)

        @pl.when(j < nb)
        def _():
            for cp in fetch(be_ref[jnp.minimum(j, nb - 1)], 1 - slot):
                cp.start()

        slot_ref[0] = 1 - slot

    @pl.when(used_ref[i] != 0)
    def _():
        x = _tiles_to_rows(x_ref, MOE_ROWS).astype(BF16)
        h = jnp.dot(x, w1b_ref[...], preferred_element_type=F32) + b1_ref[0]
        glu = jnp.minimum(h[:, :D_EXPERT], SWIGLU_LIMIT)
        lin = jnp.clip(h[:, D_EXPERT:], -SWIGLU_LIMIT, SWIGLU_LIMIT)
        act = glu * _sigmoid(SWIGLU_ALPHA * glu) * (lin + 1.0)
        y = jnp.dot(act.astype(BF16), w2b_ref[...], preferred_element_type=F32) + b2_ref[0]
        _rows_to_tiles(y_ref, y)

    @pl.when(used_ref[i] == 0)
    def _():
        y_ref[...] = jnp.zeros_like(y_ref)


def _expert_ffn(blk_expert, blk_used, x_rows, w1, b1, w2, b2):
    P = x_rows.shape[0] // SUBLANES
    D = SUBLANES * LANES
    nb = P // MOE_ROWS
    tile_rows = pl.BlockSpec((MOE_ROWS * SUBLANES, LANES), lambda i, be, us: (i, 0))
    grid_spec = pltpu.PrefetchScalarGridSpec(
        num_scalar_prefetch=2,
        grid=(nb,),
        in_specs=[tile_rows,
                  pl.BlockSpec(memory_space=pl.ANY),
                  pl.BlockSpec((1, 1, 2 * D_EXPERT), lambda i, be, us: (be[i], 0, 0)),
                  pl.BlockSpec(memory_space=pl.ANY),
                  pl.BlockSpec((1, 1, D), lambda i, be, us: (be[i], 0, 0))],
        out_specs=tile_rows,
        scratch_shapes=[pltpu.VMEM((2, D, 2 * D_EXPERT), F32), pltpu.VMEM((2, D_EXPERT, D), F32),
                        pltpu.VMEM((D, 2 * D_EXPERT), BF16), pltpu.VMEM((D_EXPERT, D), BF16),
                        pltpu.SemaphoreType.DMA((2, 2)), pltpu.SMEM((1,), I32)],
    )
    return pl.pallas_call(
        _expert_kernel,
        out_shape=jax.ShapeDtypeStruct(x_rows.shape, F32),
        grid_spec=grid_spec,
        compiler_params=_cparams(("arbitrary",), VMEM_LIMIT),
        name="moe_expert_ffn",
    )(blk_expert, blk_used, x_rows, w1, b1, w2, b2)


def _combine_kernel(dest_ref, next_ref, gate_ref, y_ref, h_ref, gt_ref, g_ref, o_ref, buf_ref, sems, *, normalize):
    tt = h_ref.shape[0]
    i = pl.program_id(0)
    slot = i % 2

    def copy(idx_ref, slot_, t, k):
        src = y_ref.at[pl.ds(pl.multiple_of(idx_ref[t * TOP_K + k], SUBLANES), SUBLANES)]
        dst = buf_ref.at[slot_, k, pl.ds(pl.multiple_of(t * SUBLANES, SUBLANES), SUBLANES)]
        return pltpu.make_async_copy(src, dst, sems.at[slot_])

    def loop(body, unroll):
        def wrapped(t, c):
            for k in range(TOP_K):
                body(t, k)
            return c
        lax.fori_loop(0, tt, wrapped, 0, unroll=unroll)

    @pl.when(i == 0)
    def _():
        loop(lambda t, k: copy(dest_ref, slot, t, k).start(), ISSUE_UNROLL)

    loop(lambda t, k: copy(dest_ref, slot, t, k).wait(), DRAIN_UNROLL)

    @pl.when(i + 1 < pl.num_programs(0))
    def _():
        loop(lambda t, k: copy(next_ref, 1 - slot, t, k).start(), ISSUE_UNROLL)

    gates = gate_ref[...]
    y = gates[:, 0:1] * _tiles_to_rows(buf_ref, tt, lead=(slot, 0))
    for k in range(1, TOP_K):
        y = y + gates[:, k:k + 1] * _tiles_to_rows(buf_ref, tt, lead=(slot, k))
    h = h_ref[...] + (1.0 + gt_ref[0]) * y
    if normalize:
        h = h * lax.rsqrt(jnp.mean(h * h, axis=-1, keepdims=True) + NORM_EPS) * g_ref[...]
    o_ref[...] = h


def _combine_rows(dest_flat, gates, y_rows, h1, gt2, g, normalize, tt=256):
    B, S, D = h1.shape
    T = B * S
    assert S % tt == 0
    last = T // tt - 1
    out = pl.pallas_call(
        functools.partial(_combine_kernel, normalize=normalize),
        out_shape=jax.ShapeDtypeStruct((T, D), F32),
        grid=(T // tt,),
        in_specs=[pl.BlockSpec((tt * TOP_K,), lambda i: (i,), memory_space=pltpu.SMEM),
                  pl.BlockSpec((tt * TOP_K,), lambda i: (jnp.minimum(i + 1, last),), memory_space=pltpu.SMEM),
                  pl.BlockSpec((tt, LANES), lambda i: (i, 0)),
                  pl.BlockSpec(memory_space=pl.ANY),
                  pl.BlockSpec((tt, D), lambda i: (i, 0)),
                  pl.BlockSpec((1, 1, D), lambda i: ((i * tt) // S, 0, 0)),
                  pl.BlockSpec((1, D), lambda i: (0, 0))],
        out_specs=pl.BlockSpec((tt, D), lambda i: (i, 0)),
        scratch_shapes=[pltpu.VMEM((2, TOP_K, tt * SUBLANES, LANES), F32), pltpu.SemaphoreType.DMA((2,))],
        compiler_params=_cparams(("arbitrary",)),
        name="moe_combine_rows",
    )(dest_flat, dest_flat, gates, y_rows, h1.reshape(T, D), gt2, g)
    return out.reshape(B, S, D)


def _overlap_matrix(S):
    nc = S // CMP_STRIDE - 1
    ns = S // SEL_BLOCK
    c0 = np.arange(nc) * CMP_STRIDE
    s0 = np.arange(ns) * SEL_BLOCK
    ov = np.clip(np.minimum(c0[:, None] + CMP_LEN, s0[None, :] + SEL_BLOCK)
                 - np.maximum(c0[:, None], s0[None, :]), 0, None) / CMP_LEN
    return np.concatenate([ov, np.zeros((1, ns))], axis=0).astype(np.float32)


def _layer(h, mods, norm1_g, w_in, pe_k, pe_v, cw1_k, cw2_k, cw1_v, cw2_v, conv_w, conv_b, gate_b, ml_norm_g,
           w_up_nsa, w_up_ml, w_out, norm2_g, router_w, router_b, exp_w1, exp_b1, exp_w2, exp_b2,
           final_g, last_layer):
    B, S, D = h.shape
    T = B * S
    sh1, sc1, gt1, sh2, sc2, gt2 = mods

    w_r = jnp.concatenate([w_in[:, 0:1280], w_in[:, 1304:2328], w_in[:, 2328:2840], w_in[:, 2848:3360],
                           w_in[:, 3360:5408], w_in[:, 1280:1304], w_in[:, 2840:2848],
                           jnp.zeros((D, IN_WR - 5408), w_in.dtype)], axis=1).astype(BF16)
    q, kvc, kvs, ml_qk, ml_v, ml_o, merge, small = _in_projection(h, norm1_g.reshape(1, D), sc1, sh1, w_r)

    nb16 = S // CMP_STRIDE
    t16 = kvc.reshape(B, S, 4, NSA_DH).transpose(0, 2, 1, 3).reshape(B, 4, nb16, CMP_STRIDE * NSA_DH)
    pe = jnp.stack([pe_k.reshape(2, CMP_STRIDE * NSA_DH), pe_v.reshape(2, CMP_STRIDE * NSA_DH)])
    cmp = _compress(t16, pe, jnp.stack([cw1_k, cw1_v]), jnp.stack([cw2_k, cw2_v]))
    n_sel = min(SEL_TOPN, S // SEL_BLOCK)
    kc = cmp[:, :NSA_GROUPS].transpose(0, 2, 1, 3).reshape(B, nb16, NSA_KVW).astype(BF16)
    vct = cmp[:, NSA_GROUPS:].transpose(0, 1, 3, 2).reshape(B, NSA_KVW, nb16).astype(BF16)

    def key_major_tiles(v, w):
        vt = v.reshape(B, S // w, w, LANES).swapaxes(-1, -2)
        return jnp.concatenate([vt, jnp.ones((B, S // w, ONES_ROWS, w), vt.dtype)], axis=2)

    o_nsa = _nsa_attention(q, kc, vct, jnp.asarray(_overlap_matrix(S).T, dtype=BF16), kvs,
                           key_major_tiles(kvs[:, :, LANES:2 * LANES], min(8 * LANES, S)),
                           key_major_tiles(kvs[:, :, 3 * LANES:], LANES), small, n_sel)

    gates = small[:, :, 3 * NSA_HEADS:3 * NSA_HEADS + 2 * ML_HEADS].transpose(0, 2, 1)
    gate_row = gates.reshape(B, 2 * ML_HEADS, S // ML_CHUNK, ML_CHUNK)
    gate_col = gates.reshape(B, 2 * ML_HEADS, S, 1)
    o_ml = _mlstm(ml_qk, ml_v, ml_o, gate_row, gate_col, gate_b, conv_w, conv_b.reshape(1, -1),
                  ml_norm_g.reshape(1, -1))

    rw_hi = router_w.astype(BF16)
    rw_lo = (router_w - rw_hi.astype(F32)).astype(BF16)
    h1, n2, ti, tg = _out_projection(o_nsa, o_ml, merge, h, gt1, sc2, sh2, norm2_g.reshape(1, D),
                                     w_up_nsa.astype(BF16), w_up_ml.astype(BF16), w_out.astype(BF16),
                                     jnp.stack([rw_hi, rw_lo]), router_b.reshape(1, N_EXPERTS))

    A = T * TOP_K
    n_blk = -(-(A + N_EXPERTS * (MOE_ROWS - 1)) // MOE_ROWS)
    P = n_blk * MOE_ROWS
    ti2 = ti.reshape(T, LANES)
    rank, cnt = _expert_rank(ti2)
    dest, blk_e, pad_info = _expert_dest(cnt, ti2, rank, n_blk)
    dest_flat = dest[:, :TOP_K].reshape(A)
    x_rows = _scatter_rows(dest_flat, pad_info, n2.reshape(T * SUBLANES, LANES), P)
    y_rows = _expert_ffn(blk_e[:, 0], blk_e[:, 1], x_rows, exp_w1, exp_b1.reshape(N_EXPERTS, 1, -1),
                         exp_w2, exp_b2.reshape(N_EXPERTS, 1, -1))
    return _combine_rows(dest_flat, tg.reshape(T, LANES), y_rows, h1, gt2, final_g, normalize=last_layer)


def kernel(x, c, ada_w, ada_b, norm1_g, w_in, nsa_pe_k, nsa_pe_v, nsa_cmp_w1_k, nsa_cmp_w2_k, nsa_cmp_w1_v, nsa_cmp_w2_v, ml_conv_w, ml_conv_b, ml_gate_b, ml_norm_g, w_up_nsa, w_up_ml, w_out, norm2_g, router_w, router_b, exp_w1, exp_b1, exp_w2, exp_b2, final_g):
    B, S, D = x.shape
    depth = ada_w.shape[0]
    h = x
    for layer in range(depth):
        mod = _modulation(c, ada_w[layer], ada_b[layer])
        mods = [mod[:, i * D:(i + 1) * D].reshape(B, 1, D) for i in range(6)]
        h = _layer(h, mods, norm1_g[layer], w_in[layer], nsa_pe_k[layer], nsa_pe_v[layer],
                   nsa_cmp_w1_k[layer], nsa_cmp_w2_k[layer], nsa_cmp_w1_v[layer], nsa_cmp_w2_v[layer],
                   ml_conv_w[layer], ml_conv_b[layer], ml_gate_b[layer], ml_norm_g[layer],
                   w_up_nsa[layer], w_up_ml[layer], w_out[layer], norm2_g[layer], router_w[layer],
                   router_b[layer], exp_w1[layer], exp_b1[layer], exp_w2[layer], exp_b2[layer],
                   final_g.reshape(1, D), layer + 1 == depth)
    return h
```

```python
import functools

import numpy as np
import jax
import jax.numpy as jnp
from jax import lax
from jax.experimental import pallas as pl
from jax.experimental.pallas import tpu as pltpu

F32 = jnp.float32
BF16 = jnp.bfloat16
I32 = jnp.int32
HIGHEST = lax.Precision.HIGHEST

D_MODEL = 1024
NSA_HEADS = 8
NSA_GROUPS = 2
NSA_REP = NSA_HEADS // NSA_GROUPS
NSA_DH = 64
NSA_SCALE = NSA_DH ** -0.5
Q_PRESCALE = NSA_SCALE * float(np.log2(np.e))
ONES_ROWS = 16
ISSUE_UNROLL = 8
DRAIN_UNROLL = 16
CMP_STRIDE = 16
CMP_LEN = 32
CMP_HIDDEN = 128
SEL_BLOCK = 64
SEL_TOPN = 16
WINDOW = 512
Q_BLOCK = 128
ML_HEADS = 4
ML_DH = 128
ML_CHUNK = 128
CONV_WIDTH = 4
N_EXPERTS = 32
TOP_K = 4
D_EXPERT = 1024
SWIGLU_LIMIT = 7.0
SWIGLU_ALPHA = 1.702
MOE_ROWS = 256
NORM_EPS = 1e-6
NEG_INF = -1e30
FORCE_BONUS = 1e4

NSA_QW = NSA_HEADS * NSA_DH
NSA_KVW = NSA_GROUPS * NSA_DH
ML_W = ML_HEADS * ML_DH
LANES = 128
SUBLANES = 8
VMEM_LIMIT = 52 * 1024 * 1024

_C_Q = (0, 512)
_C_KVC = (512, 768)
_C_KVS = (768, 1280)
_C_MQK = (1280, 2304)
_C_MV = (2304, 2816)
_C_MO = (2816, 3328)
_C_MG = (3328, 5376)
_C_SM = (5376, 5504)
IN_WR = 5504


def _sigmoid(x):
    return 1.0 / (1.0 + jnp.exp(-x))


def _iota(shape, dim):
    return lax.broadcasted_iota(I32, shape, dim)


def _cparams(sem, vmem=None):
    return pltpu.CompilerParams(dimension_semantics=sem, vmem_limit_bytes=vmem)


def _tiles_to_rows(ref, n, lead=()):
    return jnp.concatenate([ref[lead + (pl.ds(s, n, stride=SUBLANES), slice(None))] for s in range(SUBLANES)],
                           axis=1)


def _rows_to_tiles(ref, x, lead=()):
    n = x.shape[0]
    for s in range(SUBLANES):
        ref[lead + (pl.ds(s, n, stride=SUBLANES), slice(None))] = x[:, s * LANES:(s + 1) * LANES]


def _mod_kernel(c_ref, w_ref, b_ref, o_ref):
    c = c_ref[...]
    ca = c * _sigmoid(c)
    o_ref[...] = jnp.dot(ca, w_ref[...], precision=HIGHEST, preferred_element_type=F32) + b_ref[...]


def _modulation(c, ada_w, ada_b):
    B, D = c.shape
    n = ada_w.shape[1] // D
    return pl.pallas_call(
        _mod_kernel,
        out_shape=jax.ShapeDtypeStruct((B, n * D), F32),
        grid=(n,),
        in_specs=[pl.BlockSpec((B, D), lambda j: (0, 0)),
                  pl.BlockSpec((D, D), lambda j: (0, j)),
                  pl.BlockSpec((1, D), lambda j: (0, j))],
        out_specs=pl.BlockSpec((B, D), lambda j: (0, j)),
        compiler_params=_cparams(("parallel",)),
        name="modulation",
    )(c, ada_w, ada_b.reshape(1, n * D))


def _inproj_kernel(x_ref, g_ref, sc_ref, sh_ref, w_ref,
                   q_ref, kvc_ref, kvs_ref, mqk_ref, mv_ref, mo_ref, mg_ref, sm_ref):
    x = x_ref[0]
    y = x * lax.rsqrt(jnp.mean(x * x, axis=-1, keepdims=True) + NORM_EPS) * g_ref[...]
    n = (y * (1.0 + sc_ref[0]) + sh_ref[0]).astype(BF16)

    def proj(c):
        return jnp.dot(n, w_ref[:, c[0]:c[1]], preferred_element_type=F32)

    q_ref[0] = (proj(_C_Q) * Q_PRESCALE).astype(BF16)
    kvc_ref[0] = proj(_C_KVC)
    kvs_ref[0] = proj(_C_KVS).astype(BF16)
    mqk_ref[0] = proj(_C_MQK)
    mv_ref[0] = proj(_C_MV)
    mo_ref[0] = proj(_C_MO)
    mg_ref[0] = proj(_C_MG)
    sm_ref[0] = proj(_C_SM)


def _in_projection(x, g, sc, sh, w_r, tm=256):
    B, S, D = x.shape
    widths = [c[1] - c[0] for c in (_C_Q, _C_KVC, _C_KVS, _C_MQK, _C_MV, _C_MO, _C_MG, _C_SM)]
    dtypes = [BF16, F32, BF16, F32, F32, F32, F32, F32]
    vec = pl.BlockSpec((1, 1, D), lambda b, i: (b, 0, 0))
    return pl.pallas_call(
        _inproj_kernel,
        out_shape=[jax.ShapeDtypeStruct((B, S, w), dt) for w, dt in zip(widths, dtypes)],
        grid=(B, S // tm),
        in_specs=[pl.BlockSpec((1, tm, D), lambda b, i: (b, i, 0)),
                  pl.BlockSpec((1, D), lambda b, i: (0, 0)),
                  vec, vec,
                  pl.BlockSpec((D, IN_WR), lambda b, i: (0, 0))],
        out_specs=[pl.BlockSpec((1, tm, w), lambda b, i: (b, i, 0)) for w in widths],
        compiler_params=_cparams(("parallel", "parallel"), VMEM_LIMIT),
        name="in_projection",
    )(x, g, sc, sh, w_r)


def _compress_kernel(t_ref, pe_ref, w1_ref, w2_ref, o_ref):
    half = CMP_STRIDE * NSA_DH
    t = t_ref[0, 0]
    pe = pe_ref[0]
    nb = t.shape[0]
    a = jnp.dot((t + pe[0:1]).astype(BF16), w1_ref[0, :half, :].astype(BF16), preferred_element_type=F32)
    b = jnp.dot((t + pe[1:2]).astype(BF16), w1_ref[0, half:, :].astype(BF16), preferred_element_type=F32)
    h = a + pltpu.roll(b, nb - 1, 0)
    gl = 0.5 * h * (1.0 + jnp.tanh(np.sqrt(2.0 / np.pi).astype(np.float32) * (h + 0.044715 * (h * h * h))))
    o_ref[0, 0] = jnp.dot(gl.astype(BF16), w2_ref[0].astype(BF16), preferred_element_type=F32)


def _compress(t16, pe, w1, w2):
    B, _, nb, wdt = t16.shape
    return pl.pallas_call(
        _compress_kernel,
        out_shape=jax.ShapeDtypeStruct((B, 4, nb, NSA_DH), F32),
        grid=(B, 4),
        in_specs=[pl.BlockSpec((1, 1, nb, wdt), lambda b, j: (b, j, 0, 0)),
                  pl.BlockSpec((1, 2, wdt), lambda b, j: (j // 2, 0, 0)),
                  pl.BlockSpec((1, CMP_LEN * NSA_DH, CMP_HIDDEN), lambda b, j: (j // 2, 0, 0)),
                  pl.BlockSpec((1, CMP_HIDDEN, NSA_DH), lambda b, j: (j // 2, 0, 0))],
        out_specs=pl.BlockSpec((1, 1, nb, NSA_DH), lambda b, j: (b, j, 0, 0)),
        compiler_params=_cparams(("parallel", "parallel")),
        name="nsa_compress",
    )(t16, pe, w1, w2)


def _nsa_kernel(q_ref, kc_ref, vct_ref, ovt_ref, ks_ref, vst_ref, kw_ref, vwt_ref, sm_ref, o_ref, *, n_sel, tk):
    qb = Q_BLOCK
    ncp = kc_ref.shape[1]
    ns = ovt_ref.shape[0]
    q0 = pl.program_id(1) * qb
    t_l = q0 + _iota((1, qb), 1)
    sub = _iota((LANES, qb), 0)
    gT = _sigmoid(sm_ref[0].T)

    gw = NSA_REP * qb
    q_heads = []
    for c in range(NSA_HEADS // 2):
        qc = q_ref[0, :, c * LANES:(c + 1) * LANES].astype(F32).T
        for e in range(2):
            g = (2 * c + e) // NSA_REP
            x = qc if e == g else pltpu.roll(qc, NSA_DH, 0)
            q_heads.append(jnp.where((sub >> 6) == g, x, 0.0).astype(BF16))
    qT = [jnp.concatenate(q_heads[g * NSA_REP:(g + 1) * NSA_REP], axis=1) for g in range(NSA_GROUPS)]
    t_g = q0 + (_iota((1, gw), 1) & (qb - 1))

    def head_cols(x, r):
        return x[:, r * qb:(r + 1) * qb]

    def gate_row(branch, g):
        rows = [gT[branch * NSA_HEADS + g * NSA_REP + r:branch * NSA_HEADS + g * NSA_REP + r + 1]
                for r in range(NSA_REP)]
        return jnp.concatenate(rows, axis=1)

    kc = kc_ref[0]
    vct = vct_ref[0]
    cmask = (_iota((ncp, gw), 0) * CMP_STRIDE + (CMP_LEN - 1)) <= t_g
    blk = _iota((ns, qb), 0)
    blk_f = blk.astype(F32)
    cur = t_l >> 6
    forced = (blk == 0) | (blk == cur) | (blk == cur - 1)
    ocT = []
    selT = []
    for g in range(NSA_GROUPS):
        s = jnp.dot(kc, qT[g], preferred_element_type=F32)
        s = jnp.where(cmask, s, NEG_INF)
        e = jnp.exp2(s - jnp.max(s, axis=0, keepdims=True))
        p = e * (1.0 / jnp.sum(e, axis=0, keepdims=True))
        p = jnp.where(cmask, p, 0.0)
        ocT.append(jnp.dot(vct, p.astype(BF16), preferred_element_type=F32))
        psum = head_cols(p, 0)
        for r in range(1, NSA_REP):
            psum = psum + head_cols(p, r)
        p_hi = psum.astype(BF16)
        rest = psum - p_hi.astype(F32)
        p_mid = rest.astype(BF16)
        p_lo = (rest - p_mid.astype(F32)).astype(BF16)
        imp = (jnp.dot(ovt_ref[...], p_hi, preferred_element_type=F32)
               + (jnp.dot(ovt_ref[...], p_mid, preferred_element_type=F32)
                  + jnp.dot(ovt_ref[...], p_lo, preferred_element_type=F32)))
        imp = jnp.where(blk > cur, NEG_INF, imp + jnp.where(forced, FORCE_BONUS, 0.0))
        sel = jnp.zeros((ns, qb), F32)
        for _ in range(n_sel):
            mx = jnp.max(imp, axis=0, keepdims=True)
            first = jnp.min(jnp.where(imp == mx, blk_f, float(ns)), axis=0, keepdims=True)
            pick = blk_f == first
            sel = jnp.where(pick, 1.0, sel)
            imp = jnp.where(pick, -jnp.inf, imp)
        selT.append(sel.astype(BF16))

    wlen = WINDOW + qb
    wt0 = jnp.maximum(q0 - WINDOW, 0) // LANES
    wstart = pl.multiple_of(wt0 * LANES, LANES)
    kwin = kw_ref[0, pl.ds(wstart, wlen), :]
    rel = t_g - (wstart + _iota((wlen, gw), 0))
    wmask = lax.bitcast_convert_type(rel, jnp.uint32) < WINDOW
    partial = []
    for g in range(NSA_GROUPS):
        lo = g * NSA_DH
        s = jnp.dot(kwin, qT[g], preferred_element_type=F32)
        s = jnp.where(wmask, s, NEG_INF)
        pb = jnp.exp2(s - jnp.max(s, axis=0, keepdims=True)).astype(BF16)
        ow = jnp.dot(vwt_ref[0, wt0], pb[:LANES], preferred_element_type=F32)
        for i in range(1, wlen // LANES):
            ow = ow + jnp.dot(vwt_ref[0, wt0 + i], pb[i * LANES:(i + 1) * LANES], preferred_element_type=F32)
        ow = ow[:LANES] * (1.0 / ow[LANES:LANES + 1])
        partial.append(gate_row(0, g) * ocT[g][lo:lo + NSA_DH] + gate_row(2, g) * ow[lo:lo + NSA_DH])

    nt = (q0 + qb + tk - 1) // tk

    def tile_step(r0, j, lane0, width, diagonal, carry):
        kt = ks_ref[0, pl.ds(r0, width), :]
        expand = jnp.where(((r0 + _iota((width, ns), 0)) >> 6) == _iota((width, ns), 1), 1.0, 0.0).astype(BF16)
        new = []
        for g in range(NSA_GROUPS):
            m, l, acc = carry[g]
            hit = jnp.dot(expand, selT[g], preferred_element_type=F32)
            if diagonal:
                hit = jnp.where((r0 + _iota((width, qb), 0)) <= t_l, hit, 0.0)
            ok = hit > 0.5
            s = jnp.dot(kt, qT[g], preferred_element_type=F32)
            s = jnp.concatenate([jnp.where(ok, head_cols(s, r), NEG_INF) for r in range(NSA_REP)], axis=1)
            m_new = jnp.maximum(m, jnp.max(s, axis=0, keepdims=True))
            alpha = jnp.exp2(m - m_new)
            pv = jnp.dot(vst_ref[0, j, :, lane0:lane0 + width], jnp.exp2(s - m_new).astype(BF16),
                         preferred_element_type=F32)
            new.append((m_new, alpha * l + pv[LANES:LANES + 1], alpha * acc + pv[:LANES]))
        return tuple(new)

    init = tuple((jnp.full((1, gw), NEG_INF, F32), jnp.zeros((1, gw), F32), jnp.zeros((LANES, gw), F32))
                 for _ in range(NSA_GROUPS))
    final = lax.fori_loop(0, nt - 1, lambda j, c: tile_step(pl.multiple_of(j * tk, tk), j, 0, tk, False, c), init)
    half = tk // 2
    base = pl.multiple_of((nt - 1) * tk, tk)
    final = tile_step(base, nt - 1, 0, half, True, final)
    final = lax.cond(q0 + qb > base + half,
                     lambda c: tile_step(pl.multiple_of(base + half, half), nt - 1, half, half, True, c),
                     lambda c: c, final)

    heads = []
    for g in range(NSA_GROUPS):
        _, l, acc = final[g]
        o_g = partial[g] + gate_row(1, g) * (acc[g * NSA_DH:(g + 1) * NSA_DH] * (1.0 / l))
        heads += [head_cols(o_g, r) for r in range(NSA_REP)]
    for c in range(NSA_HEADS // 2):
        o_ref[0, :, c * LANES:(c + 1) * LANES] = jnp.concatenate(heads[2 * c:2 * c + 2], axis=0).T


def _nsa_attention(q, kc, vct, ovt, kvs, vst, vwt, small, n_sel):
    B, S, _ = q.shape
    ns, ncp = ovt.shape
    tk = vst.shape[3]
    assert S >= WINDOW + Q_BLOCK and S % tk == 0 and tk % LANES == 0
    nt128 = S // LANES

    def whole(shape):
        return pl.BlockSpec((1,) + shape, lambda b, i: (b,) + (0,) * len(shape))

    return pl.pallas_call(
        functools.partial(_nsa_kernel, n_sel=n_sel, tk=tk),
        out_shape=jax.ShapeDtypeStruct((B, S, NSA_QW), F32),
        grid=(B, S // Q_BLOCK),
        in_specs=[pl.BlockSpec((1, Q_BLOCK, NSA_QW), lambda b, i: (b, i, 0)),
                  whole((ncp, LANES)), whole((LANES, ncp)),
                  pl.BlockSpec((ns, ncp), lambda b, i: (0, 0)),
                  pl.BlockSpec((1, S, LANES), lambda b, i: (b, 0, 0)),
                  whole((S // tk, LANES + ONES_ROWS, tk)),
                  pl.BlockSpec((1, S, LANES), lambda b, i: (b, 0, 2)),
                  whole((nt128, LANES + ONES_ROWS, LANES)),
                  pl.BlockSpec((1, Q_BLOCK, LANES), lambda b, i: (b, i, 0))],
        out_specs=pl.BlockSpec((1, Q_BLOCK, NSA_QW), lambda b, i: (b, i, 0)),
        compiler_params=_cparams(("parallel", "parallel"), VMEM_LIMIT),
        name="nsa_attention",
    )(q, kc, vct, ovt, kvs, vst, kvs, vwt, small)


def _mlstm_kernel(gb_ref, q_ref, k_ref, v_ref, o_ref, grow_ref, gcol_ref, cw_ref, cb_ref, ng_ref, out_ref,
                  c_scr, n_scr, m_scr, hq_scr, hk_scr):
    L = ML_CHUNK
    H = ML_HEADS
    ts = q_ref.shape[1]
    kscale = ML_DH ** -0.5
    row = _iota((L, L), 0)
    col = _iota((L, L), 1)
    tri = row >= col
    halo_rows = SUBLANES

    @pl.when(pl.program_id(1) == 0)
    def _():
        c_scr[...] = jnp.zeros_like(c_scr)
        n_scr[...] = jnp.zeros_like(n_scr)
        m_scr[...] = jnp.zeros_like(m_scr)
        hq_scr[...] = jnp.zeros_like(hq_scr)
        hk_scr[...] = jnp.zeros_like(hk_scr)

    def conv_silu(x_ref, prev_ref, woff, h, c, r0):
        lanes = slice(h * ML_DH, (h + 1) * ML_DH)
        wl = slice(woff + h * ML_DH, woff + (h + 1) * ML_DH)
        main = x_ref[0, pl.ds(r0, L), lanes]
        h0 = pl.multiple_of(jnp.maximum(r0 - halo_rows, 0), halo_rows)
        halo = jnp.where(c > 0, x_ref[0, pl.ds(h0, halo_rows), lanes], prev_ref[:, lanes])
        cat = jnp.concatenate([halo, main], axis=0)
        y = main * cw_ref[CONV_WIDTH - 1:CONV_WIDTH, wl] + cb_ref[:, wl]
        for w in range(CONV_WIDTH - 1):
            sft = CONV_WIDTH - 1 - w
            y = y + pltpu.roll(cat, sft, 0)[halo_rows:, :] * cw_ref[w:w + 1, wl]
        return y * _sigmoid(y)

    def logsig(x):
        return -(jnp.maximum(-x, 0.0) + jnp.log(1.0 + jnp.exp(-jnp.abs(x))))

    def head_step(h, c, r0, C, n, m):
        lanes = slice(h * ML_DH, (h + 1) * ML_DH)
        gb_i = gb_ref[h]
        gb_f = gb_ref[H + h]
        qc = conv_silu(q_ref, hq_scr, 0, h, c, r0)
        kc = conv_silu(k_ref, hk_scr, ML_W, h, c, r0) * kscale
        vc = v_ref[0, pl.ds(r0, L), lanes]
        li_row = grow_ref[0, h, pl.ds(c, 1), :] + gb_i
        lf_row = logsig(grow_ref[0, H + h, pl.ds(c, 1), :] + gb_f)
        li_col = gcol_ref[0, h, pl.ds(r0, L), :] + gb_i
        lf_col = logsig(gcol_ref[0, H + h, pl.ds(r0, L), :] + gb_f)
        b_col = jnp.sum(jnp.where(tri, lf_row, 0.0), axis=1, keepdims=True)
        b_row = jnp.sum(jnp.where(row <= col, lf_col, 0.0), axis=0, keepdims=True)
        Dm = jnp.where(tri, b_col - b_row + li_row, NEG_INF)
        inter = b_col + m
        m_t = jnp.maximum(inter, jnp.max(Dm, axis=1, keepdims=True))
        Dw = jnp.exp(Dm - m_t)
        inter_w = jnp.exp(inter - m_t)
        qb16 = qc.astype(BF16)
        kb16 = kc.astype(BF16)
        vb16 = vc.astype(BF16)
        qk = lax.dot_general(qb16, kb16, (((1,), (1,)), ((), ())), preferred_element_type=F32) * Dw
        num = (jnp.dot(qk.astype(BF16), vb16, preferred_element_type=F32)
               + inter_w * jnp.dot(qb16, C.astype(BF16), preferred_element_type=F32))
        den = jnp.sum(qk, axis=1, keepdims=True) + inter_w * jnp.sum(qc * n, axis=1, keepdims=True)
        hc = num / jnp.maximum(jnp.abs(den), jnp.exp(-m_t))
        bL = b_col[L - 1:L, :]
        a_col = bL - b_col + li_col
        m_new = jnp.maximum(bL + m, jnp.max(a_col, axis=0, keepdims=True))
        aw = jnp.exp(a_col - m_new)
        decay = jnp.exp(bL + m - m_new)
        awk = aw * kc
        C_new = decay * C + jnp.dot(awk.T.astype(BF16), vb16, preferred_element_type=F32)
        n_new = decay * n + jnp.sum(awk, axis=0, keepdims=True)
        hn = hc * lax.rsqrt(jnp.mean(hc * hc, axis=1, keepdims=True) + NORM_EPS) * ng_ref[:, lanes]
        out_ref[0, pl.ds(r0, L), lanes] = hn * _sigmoid(o_ref[0, pl.ds(r0, L), lanes])
        return C_new, n_new, m_new

    def step(c, carry):
        r0 = pl.multiple_of(c * L, L)
        return tuple(head_step(h, c, r0, *carry[h]) for h in range(H))

    init = tuple((c_scr[h], n_scr[h], m_scr[h][:, 0:1]) for h in range(H))
    final = lax.fori_loop(0, ts // L, step, init)
    for h in range(H):
        C, n, m = final[h]
        c_scr[h] = C
        n_scr[h] = n
        m_scr[h] = jnp.broadcast_to(m, (1, LANES))
    hq_scr[...] = q_ref[0, ts - halo_rows:ts, :]
    hk_scr[...] = k_ref[0, ts - halo_rows:ts, :]


def _mlstm(ml_qk, ml_v, ml_o, gate_row, gate_col, gate_b, conv_w, conv_b, norm_g, ts=1024):
    B, S, _ = ml_v.shape
    H = ML_HEADS
    ts = min(ts, S)

    def rows(cb):
        return pl.BlockSpec((1, ts, ML_W), lambda b, i, gb: (b, i, cb))

    def full(shape):
        return pl.BlockSpec(shape, lambda b, i, gb: (0,) * len(shape))

    grid_spec = pltpu.PrefetchScalarGridSpec(
        num_scalar_prefetch=1,
        grid=(B, S // ts),
        in_specs=[rows(0), rows(1), rows(0), rows(0),
                  pl.BlockSpec((1, 2 * H, ts // ML_CHUNK, ML_CHUNK), lambda b, i, gb: (b, 0, i, 0)),
                  pl.BlockSpec((1, 2 * H, ts, 1), lambda b, i, gb: (b, 0, i, 0)),
                  full((CONV_WIDTH, 2 * ML_W)), full((1, 2 * ML_W)), full((1, ML_W))],
        out_specs=rows(0),
        scratch_shapes=[pltpu.VMEM((H, ML_DH, ML_DH), F32), pltpu.VMEM((H, 1, ML_DH), F32),
                        pltpu.VMEM((H, 1, LANES), F32),
                        pltpu.VMEM((SUBLANES, ML_W), F32), pltpu.VMEM((SUBLANES, ML_W), F32)],
    )
    return pl.pallas_call(
        _mlstm_kernel,
        out_shape=jax.ShapeDtypeStruct((B, S, ML_W), F32),
        grid_spec=grid_spec,
        compiler_params=_cparams(("parallel", "arbitrary"), VMEM_LIMIT),
        name="mlstm",
    )(gate_b, ml_qk, ml_qk, ml_v, ml_o, gate_row, gate_col, conv_w, conv_b, norm_g)


def _pack_cols(cols, dtype):
    tm = cols[0].shape[0]
    lane = _iota((tm, LANES), 1)
    out = jnp.zeros((tm, LANES), dtype)
    for k, c in enumerate(cols):
        out = jnp.where(lane == k, c, out)
    return out


def _outproj_kernel(a_ref, b_ref, mg_ref, x_ref, gt_ref, sc_ref, sh_ref, g2_ref,
                    wa_ref, wb_ref, wo_ref, wr_ref, br_ref,
                    h_ref, n_ref, ti_ref, tg_ref):
    D = x_ref.shape[2]
    ua = jnp.dot(a_ref[0].astype(BF16), wa_ref[...], preferred_element_type=F32)
    ub = jnp.dot(b_ref[0].astype(BF16), wb_ref[...], preferred_element_type=F32)
    u = _sigmoid(mg_ref[0, :, :D]) * ua + _sigmoid(mg_ref[0, :, D:]) * ub
    mix = jnp.dot(u.astype(BF16), wo_ref[...], preferred_element_type=F32)
    h = x_ref[0] + (1.0 + gt_ref[0]) * mix
    h_ref[0] = h
    y = h * lax.rsqrt(jnp.mean(h * h, axis=-1, keepdims=True) + NORM_EPS) * g2_ref[...]
    n = y * (1.0 + sc_ref[0]) + sh_ref[0]
    _rows_to_tiles(n_ref, n, lead=(0,))
    n_hi = n.astype(BF16)
    n_lo = (n - n_hi.astype(F32)).astype(BF16)
    logits = (jnp.dot(n_hi, wr_ref[0], preferred_element_type=F32)
              + (jnp.dot(n_hi, wr_ref[1], preferred_element_type=F32)
                 + jnp.dot(n_lo, wr_ref[0], preferred_element_type=F32))) + br_ref[...]
    tm, ne = logits.shape
    lane = _iota((tm, ne), 1).astype(F32)
    work = logits
    vals, idxs = [], []
    for _ in range(TOP_K):
        mx = jnp.max(work, axis=-1, keepdims=True)
        ix = jnp.min(jnp.where(work == mx, lane, float(ne)), axis=-1, keepdims=True)
        vals.append(mx)
        idxs.append(ix)
        work = jnp.where(lane == ix, -jnp.inf, work)
    es = [jnp.exp(v - vals[0]) for v in vals]
    tot = es[0] + es[1] + es[2] + es[3]
    ti_ref[0] = _pack_cols(idxs, F32).astype(I32)
    tg_ref[0] = _pack_cols([e / tot for e in es], F32)


def _out_projection(o_nsa, o_ml, merge, x, gt1, sc2, sh2, g2, wa, wb, wo, wr, br, tm=256):
    B, S, D = x.shape
    vec = pl.BlockSpec((1, 1, D), lambda b, i: (b, 0, 0))

    def full(shape):
        return pl.BlockSpec(shape, lambda b, i: (0,) * len(shape))

    def rows(w):
        return pl.BlockSpec((1, tm, w), lambda b, i: (b, i, 0))

    assert D == SUBLANES * LANES
    return pl.pallas_call(
        _outproj_kernel,
        out_shape=[jax.ShapeDtypeStruct((B, S, D), F32), jax.ShapeDtypeStruct((B, S * SUBLANES, LANES), F32),
                   jax.ShapeDtypeStruct((B, S, LANES), I32), jax.ShapeDtypeStruct((B, S, LANES), F32)],
        grid=(B, S // tm),
        in_specs=[rows(NSA_QW), rows(ML_W), rows(2 * D), rows(D), vec, vec, vec, full((1, D)),
                  full(wa.shape), full(wb.shape), full(wo.shape), full(wr.shape), full((1, N_EXPERTS))],
        out_specs=[rows(D), pl.BlockSpec((1, tm * SUBLANES, LANES), lambda b, i: (b, i, 0)), rows(LANES),
                   rows(LANES)],
        compiler_params=_cparams(("parallel", "parallel"), VMEM_LIMIT),
        name="out_projection_router",
    )(o_nsa, o_ml, merge, x, gt1, sc2, sh2, g2, wa, wb, wo, wr, br)


def _rank_kernel(ti_ref, rank_ref, cnt_ref, carry_ref):
    tt = ti_ref.shape[0]

    @pl.when(pl.program_id(0) == 0)
    def _():
        carry_ref[...] = jnp.zeros_like(carry_ref)

    ti = ti_ref[...]
    lane = _iota((tt, LANES), 1)
    hots = [ti[:, k:k + 1] == lane for k in range(TOP_K)]
    member = jnp.zeros((tt, LANES), F32)
    for hot in hots:
        member = member + jnp.where(hot, 1.0, 0.0)
    strict = jnp.where(_iota((tt, tt), 0) > _iota((tt, tt), 1), 1.0, 0.0).astype(BF16)
    before = jnp.dot(strict, member.astype(BF16), preferred_element_type=F32) + carry_ref[0:1, :]
    ranks = [jnp.sum(jnp.where(hot, before, 0.0), axis=-1, keepdims=True) for hot in hots]
    rank_ref[...] = _pack_cols(ranks, F32)
    carry_ref[...] = carry_ref[...] + jnp.sum(member, axis=0, keepdims=True)
    cnt_ref[...] = carry_ref[...]


def _expert_rank(ti, tt=512):
    T = ti.shape[0]
    return pl.pallas_call(
        _rank_kernel,
        out_shape=[jax.ShapeDtypeStruct((T, LANES), F32), jax.ShapeDtypeStruct((SUBLANES, LANES), F32)],
        grid=(T // tt,),
        in_specs=[pl.BlockSpec((tt, LANES), lambda i: (i, 0))],
        out_specs=[pl.BlockSpec((tt, LANES), lambda i: (i, 0)),
                   pl.BlockSpec((SUBLANES, LANES), lambda i: (0, 0))],
        scratch_shapes=[pltpu.VMEM((SUBLANES, LANES), F32)],
        compiler_params=_cparams(("arbitrary",)),
        name="moe_rank",
    )(ti)


def _dest_kernel(cnt_ref, ti_ref, rank_ref, dest_ref, be_ref, pad_ref):
    tt = ti_ref.shape[0]
    nbp = be_ref.shape[0]
    cnt = cnt_ref[...].astype(I32)
    padded = ((cnt + (MOE_ROWS - 1)) >> 8) << 8
    lane8 = _iota((SUBLANES, LANES), 1)
    pad_end = padded
    sft = 1
    while sft < LANES:
        pad_end = pad_end + jnp.where(lane8 >= sft, pltpu.roll(pad_end, sft, 1), 0)
        sft *= 2
    pad_start = (pad_end - padded)[0:1, :].astype(F32)
    ti = ti_ref[...]
    lane = _iota((tt, LANES), 1)
    rank = rank_ref[...]
    dests = [jnp.sum(jnp.where(ti[:, k:k + 1] == lane, pad_start, 0.0), axis=-1, keepdims=True) + rank[:, k:k + 1]
             for k in range(TOP_K)]
    dest_ref[...] = _pack_cols(dests, F32).astype(I32) * SUBLANES
    first_row = _iota((nbp, LANES), 0) * MOE_ROWS
    done = (pad_end[0:1, :] <= first_row) & (_iota((nbp, LANES), 1) < N_EXPERTS)
    be = jnp.minimum(jnp.sum(jnp.where(done, 1.0, 0.0), axis=-1, keepdims=True), float(N_EXPERTS - 1))
    total = pad_end[:, N_EXPERTS - 1:N_EXPERTS]
    used = first_row < total[0:1]
    be_ref[...] = jnp.where(_iota((nbp, LANES), 1) == 1, jnp.where(used, 1, 0), be.astype(I32))
    prow = _iota((SUBLANES, LANES), 0)
    pad_ref[...] = jnp.where(prow == 0, (pad_end - padded + cnt) * SUBLANES,
                             jnp.where(prow == 1, padded - cnt, jnp.broadcast_to(total, (SUBLANES, LANES))))


def _expert_dest(cnt, ti, rank, n_blk, tt=1024):
    T = ti.shape[0]
    return pl.pallas_call(
        _dest_kernel,
        out_shape=[jax.ShapeDtypeStruct((T, LANES), I32), jax.ShapeDtypeStruct((n_blk, LANES), I32),
                   jax.ShapeDtypeStruct((SUBLANES, LANES), I32)],
        grid=(T // tt,),
        in_specs=[pl.BlockSpec((SUBLANES, LANES), lambda i: (0, 0)),
                  pl.BlockSpec((tt, LANES), lambda i: (i, 0)),
                  pl.BlockSpec((tt, LANES), lambda i: (i, 0))],
        out_specs=[pl.BlockSpec((tt, LANES), lambda i: (i, 0)),
                   pl.BlockSpec((n_blk, LANES), lambda i: (0, 0)),
                   pl.BlockSpec((SUBLANES, LANES), lambda i: (0, 0))],
        compiler_params=_cparams(("arbitrary",)),
        name="moe_dest",
    )(cnt, ti, rank)


def _scatter_kernel(dest_ref, pad_ref, src_ref, dst_ref, zero_ref, sem, zsems):
    tt = src_ref.shape[0] // SUBLANES
    zrows = zero_ref.shape[0]
    n_tail = (dst_ref.shape[0] - pad_ref[2, 0] * SUBLANES) // zrows

    def zero_row(e, r):
        off = pl.multiple_of(pad_ref[0, e] + r * SUBLANES, SUBLANES)
        return pltpu.make_async_copy(zero_ref.at[pl.ds(0, SUBLANES)], dst_ref.at[pl.ds(off, SUBLANES)], zsems.at[0])

    def zero_block(b):
        off = pl.multiple_of(pad_ref[2, 0] * SUBLANES + b * zrows, zrows)
        return pltpu.make_async_copy(zero_ref, dst_ref.at[pl.ds(off, zrows)], zsems.at[1])

    def zero_fill(wait):
        group = DRAIN_UNROLL if wait else ISSUE_UNROLL

        def per_expert(e, c):
            def one(r):
                zero_row(e, r).wait() if wait else zero_row(e, r).start()

            def grouped(gi, c2):
                for u in range(group):
                    one(gi * group + u)
                return c2

            def single(r, c2):
                one(r)
                return c2

            n = pad_ref[1, e]
            lax.fori_loop(0, n // group, grouped, 0)
            lax.fori_loop((n // group) * group, n, single, 0)
            return c
        lax.fori_loop(0, N_EXPERTS, per_expert, 0)

        def tail(b, c):
            zero_block(b).wait() if wait else zero_block(b).start()
            return c
        lax.fori_loop(0, n_tail, tail, 0)

    @pl.when(pl.program_id(0) == 0)
    def _():
        zero_ref[...] = jnp.zeros_like(zero_ref)
        zero_fill(wait=False)

    def copy(t, k):
        return pltpu.make_async_copy(
            src_ref.at[pl.ds(pl.multiple_of(t * SUBLANES, SUBLANES), SUBLANES)],
            dst_ref.at[pl.ds(pl.multiple_of(dest_ref[t * TOP_K + k], SUBLANES), SUBLANES)], sem)

    def issue(t, c):
        for k in range(TOP_K):
            copy(t, k).start()
        return c

    lax.fori_loop(0, tt, issue, 0, unroll=ISSUE_UNROLL)

    def drain(t, c):
        for k in range(TOP_K):
            copy(t, k).wait()
        return c

    lax.fori_loop(0, tt, drain, 0, unroll=DRAIN_UNROLL)

    @pl.when(pl.program_id(0) == pl.num_programs(0) - 1)
    def _():
        zero_fill(wait=True)


def _scatter_rows(dest_flat, pad_info, rows, n_rows, tt=512):
    T = rows.shape[0] // SUBLANES
    return pl.pallas_call(
        _scatter_kernel,
        out_shape=jax.ShapeDtypeStruct((n_rows * SUBLANES, LANES), rows.dtype),
        grid=(T // tt,),
        in_specs=[pl.BlockSpec((tt * TOP_K,), lambda i: (i,), memory_space=pltpu.SMEM),
                  pl.BlockSpec((SUBLANES, LANES), lambda i: (0, 0), memory_space=pltpu.SMEM),
                  pl.BlockSpec((tt * SUBLANES, LANES), lambda i: (i, 0))],
        out_specs=pl.BlockSpec(memory_space=pl.ANY),
        scratch_shapes=[pltpu.VMEM((MOE_ROWS * SUBLANES, LANES), rows.dtype), pltpu.SemaphoreType.DMA,
                        pltpu.SemaphoreType.DMA((2,))],
        compiler_params=_cparams(("arbitrary",)),
        name="moe_scatter_rows",
    )(dest_flat, pad_info, rows)


def _expert_kernel(be_ref, used_ref, x_ref, w1_hbm, b1_ref, w2_hbm, b2_ref, y_ref,
                   w1f_ref, w2f_ref, w1b_ref, w2b_ref, sems, slot_ref):
    i = pl.program_id(0)
    nb = pl.num_programs(0)
    e = be_ref[i]

    def fetch(expert, slot):
        return (pltpu.make_async_copy(w1_hbm.at[expert], w1f_ref.at[slot], sems.at[0, slot]),
                pltpu.make_async_copy(w2_hbm.at[expert], w2f_ref.at[slot], sems.at[1, slot]))

    @pl.when(i == 0)
    def _():
        slot_ref[0] = 0
        for cp in fetch(e, 0):
            cp.start()

    @pl.when(jnp.logical_or(i == 0, e != be_ref[jnp.maximum(i - 1, 0)]))
    def _():
        slot = slot_ref[0]
        for cp in fetch(e, slot):
            cp.wait()
        w1b_ref[...] = w1f_ref[slot].astype(BF16)
        w2b_ref[...] = w2f_ref[slot].astype(BF16)
        j = lax.while_loop(lambda j: jnp.logical_and(j < nb, be_ref[jnp.minimum(j, nb - 1)] == e),
                           lambda j: j + 1, i + 1)

        @pl.when(j < nb)
        def _():
            for cp in fetch(be_ref[jnp.minimum(j, nb - 1)], 1 - slot):
                cp.start()

        slot_ref[0] = 1 - slot

    @pl.when(used_ref[i] != 0)
    def _():
        x = _tiles_to_rows(x_ref, MOE_ROWS).astype(BF16)
        h = jnp.dot(x, w1b_ref[...], preferred_element_type=F32) + b1_ref[0]
        glu = jnp.minimum(h[:, :D_EXPERT], SWIGLU_LIMIT)
        lin = jnp.clip(h[:, D_EXPERT:], -SWIGLU_LIMIT, SWIGLU_LIMIT)
        act = glu * _sigmoid(SWIGLU_ALPHA * glu) * (lin + 1.0)
        y = jnp.dot(act.astype(BF16), w2b_ref[...], preferred_element_type=F32) + b2_ref[0]
        _rows_to_tiles(y_ref, y)

    @pl.when(used_ref[i] == 0)
    def _():
        y_ref[...] = jnp.zeros_like(y_ref)


def _expert_ffn(blk_expert, blk_used, x_rows, w1, b1, w2, b2):
    P = x_rows.shape[0] // SUBLANES
    D = SUBLANES * LANES
    nb = P // MOE_ROWS
    tile_rows = pl.BlockSpec((MOE_ROWS * SUBLANES, LANES), lambda i, be, us: (i, 0))
    grid_spec = pltpu.PrefetchScalarGridSpec(
        num_scalar_prefetch=2,
        grid=(nb,),
        in_specs=[tile_rows,
                  pl.BlockSpec(memory_space=pl.ANY),
                  pl.BlockSpec((1, 1, 2 * D_EXPERT), lambda i, be, us: (be[i], 0, 0)),
                  pl.BlockSpec(memory_space=pl.ANY),
                  pl.BlockSpec((1, 1, D), lambda i, be, us: (be[i], 0, 0))],
        out_specs=tile_rows,
        scratch_shapes=[pltpu.VMEM((2, D, 2 * D_EXPERT), F32), pltpu.VMEM((2, D_EXPERT, D), F32),
                        pltpu.VMEM((D, 2 * D_EXPERT), BF16), pltpu.VMEM((D_EXPERT, D), BF16),
                        pltpu.SemaphoreType.DMA((2, 2)), pltpu.SMEM((1,), I32)],
    )
    return pl.pallas_call(
        _expert_kernel,
        out_shape=jax.ShapeDtypeStruct(x_rows.shape, F32),
        grid_spec=grid_spec,
        compiler_params=_cparams(("arbitrary",), VMEM_LIMIT),
        name="moe_expert_ffn",
    )(blk_expert, blk_used, x_rows, w1, b1, w2, b2)


def _combine_kernel(dest_ref, next_ref, gate_ref, y_ref, h_ref, gt_ref, g_ref, o_ref, buf_ref, sems, *, normalize):
    tt = h_ref.shape[0]
    i = pl.program_id(0)
    slot = i % 2

    def copy(idx_ref, slot_, t, k):
        src = y_ref.at[pl.ds(pl.multiple_of(idx_ref[t * TOP_K + k], SUBLANES), SUBLANES)]
        dst = buf_ref.at[slot_, k, pl.ds(pl.multiple_of(t * SUBLANES, SUBLANES), SUBLANES)]
        return pltpu.make_async_copy(src, dst, sems.at[slot_])

    def loop(body, unroll):
        def wrapped(t, c):
            for k in range(TOP_K):
                body(t, k)
            return c
        lax.fori_loop(0, tt, wrapped, 0, unroll=unroll)

    @pl.when(i == 0)
    def _():
        loop(lambda t, k: copy(dest_ref, slot, t, k).start(), ISSUE_UNROLL)

    loop(lambda t, k: copy(dest_ref, slot, t, k).wait(), DRAIN_UNROLL)

    @pl.when(i + 1 < pl.num_programs(0))
    def _():
        loop(lambda t, k: copy(next_ref, 1 - slot, t, k).start(), ISSUE_UNROLL)

    gates = gate_ref[...]
    y = gates[:, 0:1] * _tiles_to_rows(buf_ref, tt, lead=(slot, 0))
    for k in range(1, TOP_K):
        y = y + gates[:, k:k + 1] * _tiles_to_rows(buf_ref, tt, lead=(slot, k))
    h = h_ref[...] + (1.0 + gt_ref[0]) * y
    if normalize:
        h = h * lax.rsqrt(jnp.mean(h * h, axis=-1, keepdims=True) + NORM_EPS) * g_ref[...]
    o_ref[...] = h


def _combine_rows(dest_flat, gates, y_rows, h1, gt2, g, normalize, tt=512):
    B, S, D = h1.shape
    T = B * S
    assert S % tt == 0
    last = T // tt - 1
    out = pl.pallas_call(
        functools.partial(_combine_kernel, normalize=normalize),
        out_shape=jax.ShapeDtypeStruct((T, D), F32),
        grid=(T // tt,),
        in_specs=[pl.BlockSpec((tt * TOP_K,), lambda i: (i,), memory_space=pltpu.SMEM),
                  pl.BlockSpec((tt * TOP_K,), lambda i: (jnp.minimum(i + 1, last),), memory_space=pltpu.SMEM),
                  pl.BlockSpec((tt, LANES), lambda i: (i, 0)),
                  pl.BlockSpec(memory_space=pl.ANY),
                  pl.BlockSpec((tt, D), lambda i: (i, 0)),
                  pl.BlockSpec((1, 1, D), lambda i: ((i * tt) // S, 0, 0)),
                  pl.BlockSpec((1, D), lambda i: (0, 0))],
        out_specs=pl.BlockSpec((tt, D), lambda i: (i, 0)),
        scratch_shapes=[pltpu.VMEM((2, TOP_K, tt * SUBLANES, LANES), F32), pltpu.SemaphoreType.DMA((2,))],
        compiler_params=_cparams(("arbitrary",)),
        name="moe_combine_rows",
    )(dest_flat, dest_flat, gates, y_rows, h1.reshape(T, D), gt2, g)
    return out.reshape(B, S, D)


def _overlap_matrix(S):
    nc = S // CMP_STRIDE - 1
    ns = S // SEL_BLOCK
    c0 = np.arange(nc) * CMP_STRIDE
    s0 = np.arange(ns) * SEL_BLOCK
    ov = np.clip(np.minimum(c0[:, None] + CMP_LEN, s0[None, :] + SEL_BLOCK)
                 - np.maximum(c0[:, None], s0[None, :]), 0, None) / CMP_LEN
    return np.concatenate([ov, np.zeros((1, ns))], axis=0).astype(np.float32)


def _layer(h, mods, norm1_g, w_in, pe_k, pe_v, cw1_k, cw2_k, cw1_v, cw2_v, conv_w, conv_b, gate_b, ml_norm_g,
           w_up_nsa, w_up_ml, w_out, norm2_g, router_w, router_b, exp_w1, exp_b1, exp_w2, exp_b2,
           final_g, last_layer):
    B, S, D = h.shape
    T = B * S
    sh1, sc1, gt1, sh2, sc2, gt2 = mods

    wb = w_in.astype(BF16)
    w_r = jnp.concatenate([wb[:, 0:1280], wb[:, 1304:2328], wb[:, 2328:2840], wb[:, 2848:3360],
                           wb[:, 3360:5408], wb[:, 1280:1304], wb[:, 2840:2848],
                           jnp.zeros((D, IN_WR - 5408), BF16)], axis=1)
    q, kvc, kvs, ml_qk, ml_v, ml_o, merge, small = _in_projection(h, norm1_g.reshape(1, D), sc1, sh1, w_r)

    nb16 = S // CMP_STRIDE
    t16 = kvc.reshape(B, S, 4, NSA_DH).transpose(0, 2, 1, 3).reshape(B, 4, nb16, CMP_STRIDE * NSA_DH)
    pe = jnp.stack([pe_k.reshape(2, CMP_STRIDE * NSA_DH), pe_v.reshape(2, CMP_STRIDE * NSA_DH)])
    cmp = _compress(t16, pe, jnp.stack([cw1_k, cw1_v]), jnp.stack([cw2_k, cw2_v]))
    n_sel = min(SEL_TOPN, S // SEL_BLOCK)
    kc = cmp[:, :NSA_GROUPS].transpose(0, 2, 1, 3).reshape(B, nb16, NSA_KVW).astype(BF16)
    vct = cmp[:, NSA_GROUPS:].transpose(0, 1, 3, 2).reshape(B, NSA_KVW, nb16).astype(BF16)

    def key_major_tiles(v, w):
        vt = v.reshape(B, S // w, w, LANES).swapaxes(-1, -2)
        return jnp.concatenate([vt, jnp.ones((B, S // w, ONES_ROWS, w), vt.dtype)], axis=2)

    o_nsa = _nsa_attention(q, kc, vct, jnp.asarray(_overlap_matrix(S).T, dtype=BF16), kvs,
                           key_major_tiles(kvs[:, :, LANES:2 * LANES], min(8 * LANES, S)),
                           key_major_tiles(kvs[:, :, 3 * LANES:], LANES), small, n_sel)

    gates = small[:, :, 3 * NSA_HEADS:3 * NSA_HEADS + 2 * ML_HEADS].transpose(0, 2, 1)
    gate_row = gates.reshape(B, 2 * ML_HEADS, S // ML_CHUNK, ML_CHUNK)
    gate_col = gates.reshape(B, 2 * ML_HEADS, S, 1)
    o_ml = _mlstm(ml_qk, ml_v, ml_o, gate_row, gate_col, gate_b, conv_w, conv_b.reshape(1, -1),
                  ml_norm_g.reshape(1, -1))

    rw_hi = router_w.astype(BF16)
    rw_lo = (router_w - rw_hi.astype(F32)).astype(BF16)
    h1, n2, ti, tg = _out_projection(o_nsa, o_ml, merge, h, gt1, sc2, sh2, norm2_g.reshape(1, D),
                                     w_up_nsa.astype(BF16), w_up_ml.astype(BF16), w_out.astype(BF16),
                                     jnp.stack([rw_hi, rw_lo]), router_b.reshape(1, N_EXPERTS))

    A = T * TOP_K
    n_blk = -(-(A + N_EXPERTS * (MOE_ROWS - 1)) // MOE_ROWS)
    P = n_blk * MOE_ROWS
    ti2 = ti.reshape(T, LANES)
    rank, cnt = _expert_rank(ti2)
    dest, blk_e, pad_info = _expert_dest(cnt, ti2, rank, n_blk)
    dest_flat = dest[:, :TOP_K].reshape(A)
    x_rows = _scatter_rows(dest_flat, pad_info, n2.reshape(T * SUBLANES, LANES), P)
    y_rows = _expert_ffn(blk_e[:, 0], blk_e[:, 1], x_rows, exp_w1, exp_b1.reshape(N_EXPERTS, 1, -1),
                         exp_w2, exp_b2.reshape(N_EXPERTS, 1, -1))
    return _combine_rows(dest_flat, tg.reshape(T, LANES), y_rows, h1, gt2, final_g, normalize=last_layer)


def kernel(x, c, ada_w, ada_b, norm1_g, w_in, nsa_pe_k, nsa_pe_v, nsa_cmp_w1_k, nsa_cmp_w2_k, nsa_cmp_w1_v, nsa_cmp_w2_v, ml_conv_w, ml_conv_b, ml_gate_b, ml_norm_g, w_up_nsa, w_up_ml, w_out, norm2_g, router_w, router_b, exp_w1, exp_b1, exp_w2, exp_b2, final_g):
    B, S, D = x.shape
    depth = ada_w.shape[0]
    h = x
    for layer in range(depth):
        mod = _modulation(c, ada_w[layer], ada_b[layer])
        mods = [mod[:, i * D:(i + 1) * D].reshape(B, 1, D) for i in range(6)]
        h = _layer(h, mods, norm1_g[layer], w_in[layer], nsa_pe_k[layer], nsa_pe_v[layer],
                   nsa_cmp_w1_k[layer], nsa_cmp_w2_k[layer], nsa_cmp_w1_v[layer], nsa_cmp_w2_v[layer],
                   ml_conv_w[layer], ml_conv_b[layer], ml_gate_b[layer], ml_norm_g[layer],
                   w_up_nsa[layer], w_up_ml[layer], w_out[layer], norm2_g[layer], router_w[layer],
                   router_b[layer], exp_w1[layer], exp_b1[layer], exp_w2[layer], exp_b2[layer],
                   final_g.reshape(1, D), layer + 1 == depth)
    return h
```

```python
import functools

import numpy as np
import jax
import jax.numpy as jnp
from jax import lax
from jax.experimental import pallas as pl
from jax.experimental.pallas import tpu as pltpu

F32 = jnp.float32
BF16 = jnp.bfloat16
I32 = jnp.int32
HIGHEST = lax.Precision.HIGHEST

D_MODEL = 1024
NSA_HEADS = 8
NSA_GROUPS = 2
NSA_REP = NSA_HEADS // NSA_GROUPS
NSA_DH = 64
NSA_SCALE = NSA_DH ** -0.5
Q_PRESCALE = NSA_SCALE * float(np.log2(np.e))
ONES_ROWS = 16
ISSUE_UNROLL = 8
DRAIN_UNROLL = 16
CMP_STRIDE = 16
CMP_LEN = 32
CMP_HIDDEN = 128
SEL_BLOCK = 64
SEL_TOPN = 16
WINDOW = 512
Q_BLOCK = 128
ML_HEADS = 4
ML_DH = 128
ML_CHUNK = 128
CONV_WIDTH = 4
N_EXPERTS = 32
TOP_K = 4
D_EXPERT = 1024
SWIGLU_LIMIT = 7.0
SWIGLU_ALPHA = 1.702
MOE_ROWS = 256
NORM_EPS = 1e-6
NEG_INF = -1e30
FORCE_BONUS = 1e4

NSA_QW = NSA_HEADS * NSA_DH
NSA_KVW = NSA_GROUPS * NSA_DH
ML_W = ML_HEADS * ML_DH
LANES = 128
SUBLANES = 8
VMEM_LIMIT = 52 * 1024 * 1024

_C_Q = (0, 512)
_C_KVC = (512, 768)
_C_KVS = (768, 1280)
_C_MQK = (1280, 2304)
_C_MV = (2304, 2816)
_C_MO = (2816, 3328)
_C_MG = (3328, 5376)
_C_SM = (5376, 5504)
IN_WR = 5504


def _sigmoid(x):
    return 1.0 / (1.0 + jnp.exp(-x))


def _iota(shape, dim):
    return lax.broadcasted_iota(I32, shape, dim)


def _cparams(sem, vmem=None):
    return pltpu.CompilerParams(dimension_semantics=sem, vmem_limit_bytes=vmem)


def _tiles_to_rows(ref, n, lead=()):
    return jnp.concatenate([ref[lead + (pl.ds(s, n, stride=SUBLANES), slice(None))] for s in range(SUBLANES)],
                           axis=1)


def _rows_to_tiles(ref, x, lead=()):
    n = x.shape[0]
    for s in range(SUBLANES):
        ref[lead + (pl.ds(s, n, stride=SUBLANES), slice(None))] = x[:, s * LANES:(s + 1) * LANES]


def _mod_kernel(c_ref, w_ref, b_ref, o_ref):
    c = c_ref[...]
    ca = c * _sigmoid(c)
    o_ref[...] = jnp.dot(ca, w_ref[...], precision=HIGHEST, preferred_element_type=F32) + b_ref[...]


def _modulation(c, ada_w, ada_b):
    B, D = c.shape
    n = ada_w.shape[1] // D
    return pl.pallas_call(
        _mod_kernel,
        out_shape=jax.ShapeDtypeStruct((B, n * D), F32),
        grid=(n,),
        in_specs=[pl.BlockSpec((B, D), lambda j: (0, 0)),
                  pl.BlockSpec((D, D), lambda j: (0, j)),
                  pl.BlockSpec((1, D), lambda j: (0, j))],
        out_specs=pl.BlockSpec((B, D), lambda j: (0, j)),
        compiler_params=_cparams(("parallel",)),
        name="modulation",
    )(c, ada_w, ada_b.reshape(1, n * D))


def _inproj_kernel(x_ref, g_ref, sc_ref, sh_ref, w_ref,
                   q_ref, kvc_ref, kvs_ref, mqk_ref, mv_ref, mo_ref, mg_ref, sm_ref):
    x = x_ref[0]
    y = x * lax.rsqrt(jnp.mean(x * x, axis=-1, keepdims=True) + NORM_EPS) * g_ref[...]
    n = (y * (1.0 + sc_ref[0]) + sh_ref[0]).astype(BF16)

    def proj(c):
        return jnp.dot(n, w_ref[:, c[0]:c[1]], preferred_element_type=F32)

    q_ref[0] = (proj(_C_Q) * Q_PRESCALE).astype(BF16)
    kvc_ref[0] = proj(_C_KVC)
    kvs_ref[0] = proj(_C_KVS).astype(BF16)
    mqk_ref[0] = proj(_C_MQK)
    mv_ref[0] = proj(_C_MV)
    mo_ref[0] = proj(_C_MO)
    mg_ref[0] = proj(_C_MG)
    sm_ref[0] = proj(_C_SM)


def _in_projection(x, g, sc, sh, w_r, tm=256):
    B, S, D = x.shape
    widths = [c[1] - c[0] for c in (_C_Q, _C_KVC, _C_KVS, _C_MQK, _C_MV, _C_MO, _C_MG, _C_SM)]
    dtypes = [BF16, F32, BF16, F32, F32, F32, F32, F32]
    vec = pl.BlockSpec((1, 1, D), lambda b, i: (b, 0, 0))
    return pl.pallas_call(
        _inproj_kernel,
        out_shape=[jax.ShapeDtypeStruct((B, S, w), dt) for w, dt in zip(widths, dtypes)],
        grid=(B, S // tm),
        in_specs=[pl.BlockSpec((1, tm, D), lambda b, i: (b, i, 0)),
                  pl.BlockSpec((1, D), lambda b, i: (0, 0)),
                  vec, vec,
                  pl.BlockSpec((D, IN_WR), lambda b, i: (0, 0))],
        out_specs=[pl.BlockSpec((1, tm, w), lambda b, i: (b, i, 0)) for w in widths],
        compiler_params=_cparams(("parallel", "parallel"), VMEM_LIMIT),
        name="in_projection",
    )(x, g, sc, sh, w_r)


def _compress_kernel(t_ref, pe_ref, w1_ref, w2_ref, o_ref):
    half = CMP_STRIDE * NSA_DH
    t = t_ref[0, 0]
    pe = pe_ref[0]
    nb = t.shape[0]
    a = jnp.dot((t + pe[0:1]).astype(BF16), w1_ref[0, :half, :].astype(BF16), preferred_element_type=F32)
    b = jnp.dot((t + pe[1:2]).astype(BF16), w1_ref[0, half:, :].astype(BF16), preferred_element_type=F32)
    h = a + pltpu.roll(b, nb - 1, 0)
    gl = 0.5 * h * (1.0 + jnp.tanh(np.sqrt(2.0 / np.pi).astype(np.float32) * (h + 0.044715 * (h * h * h))))
    o_ref[0, 0] = jnp.dot(gl.astype(BF16), w2_ref[0].astype(BF16), preferred_element_type=F32)


def _compress(t16, pe, w1, w2):
    B, _, nb, wdt = t16.shape
    return pl.pallas_call(
        _compress_kernel,
        out_shape=jax.ShapeDtypeStruct((B, 4, nb, NSA_DH), F32),
        grid=(B, 4),
        in_specs=[pl.BlockSpec((1, 1, nb, wdt), lambda b, j: (b, j, 0, 0)),
                  pl.BlockSpec((1, 2, wdt), lambda b, j: (j // 2, 0, 0)),
                  pl.BlockSpec((1, CMP_LEN * NSA_DH, CMP_HIDDEN), lambda b, j: (j // 2, 0, 0)),
                  pl.BlockSpec((1, CMP_HIDDEN, NSA_DH), lambda b, j: (j // 2, 0, 0))],
        out_specs=pl.BlockSpec((1, 1, nb, NSA_DH), lambda b, j: (b, j, 0, 0)),
        compiler_params=_cparams(("parallel", "parallel")),
        name="nsa_compress",
    )(t16, pe, w1, w2)


def _nsa_kernel(q_ref, kc_ref, vct_ref, ovt_ref, ks_ref, vst_ref, kw_ref, vwt_ref, sm_ref, o_ref, *, n_sel, tk):
    qb = Q_BLOCK
    ncp = kc_ref.shape[1]
    ns = ovt_ref.shape[0]
    q0 = pl.program_id(1) * qb
    t_l = q0 + _iota((1, qb), 1)
    sub = _iota((LANES, qb), 0)
    gT = _sigmoid(sm_ref[0].T)

    gw = NSA_REP * qb
    q_heads = []
    for c in range(NSA_HEADS // 2):
        qc = q_ref[0, :, c * LANES:(c + 1) * LANES].astype(F32).T
        for e in range(2):
            g = (2 * c + e) // NSA_REP
            x = qc if e == g else pltpu.roll(qc, NSA_DH, 0)
            q_heads.append(jnp.where((sub >> 6) == g, x, 0.0).astype(BF16))
    qT = [jnp.concatenate(q_heads[g * NSA_REP:(g + 1) * NSA_REP], axis=1) for g in range(NSA_GROUPS)]
    t_g = q0 + (_iota((1, gw), 1) & (qb - 1))

    def head_cols(x, r):
        return x[:, r * qb:(r + 1) * qb]

    def gate_row(branch, g):
        rows = [gT[branch * NSA_HEADS + g * NSA_REP + r:branch * NSA_HEADS + g * NSA_REP + r + 1]
                for r in range(NSA_REP)]
        return jnp.concatenate(rows, axis=1)

    kc = kc_ref[0]
    vct = vct_ref[0]
    cmask = (_iota((ncp, gw), 0) * CMP_STRIDE + (CMP_LEN - 1)) <= t_g
    blk = _iota((ns, qb), 0)
    blk_f = blk.astype(F32)
    cur = t_l >> 6
    forced = (blk == 0) | (blk == cur) | (blk == cur - 1)
    ocT = []
    selT = []
    for g in range(NSA_GROUPS):
        s = jnp.dot(kc, qT[g], preferred_element_type=F32)
        s = jnp.where(cmask, s, NEG_INF)
        e = jnp.exp2(s - jnp.max(s, axis=0, keepdims=True))
        p = e * (1.0 / jnp.sum(e, axis=0, keepdims=True))
        p = jnp.where(cmask, p, 0.0)
        ocT.append(jnp.dot(vct, p.astype(BF16), preferred_element_type=F32))
        psum = head_cols(p, 0)
        for r in range(1, NSA_REP):
            psum = psum + head_cols(p, r)
        p_hi = psum.astype(BF16)
        rest = psum - p_hi.astype(F32)
        p_mid = rest.astype(BF16)
        p_lo = (rest - p_mid.astype(F32)).astype(BF16)
        imp = (jnp.dot(ovt_ref[...], p_hi, preferred_element_type=F32)
               + (jnp.dot(ovt_ref[...], p_mid, preferred_element_type=F32)
                  + jnp.dot(ovt_ref[...], p_lo, preferred_element_type=F32)))
        imp = jnp.where(blk > cur, NEG_INF, imp + jnp.where(forced, FORCE_BONUS, 0.0))
        sel = jnp.zeros((ns, qb), F32)
        for _ in range(n_sel):
            mx = jnp.max(imp, axis=0, keepdims=True)
            first = jnp.min(jnp.where(imp == mx, blk_f, float(ns)), axis=0, keepdims=True)
            pick = blk_f == first
            sel = jnp.where(pick, 1.0, sel)
            imp = jnp.where(pick, -jnp.inf, imp)
        selT.append(sel.astype(BF16))

    wlen = WINDOW + qb
    wt0 = jnp.maximum(q0 - WINDOW, 0) // LANES
    wstart = pl.multiple_of(wt0 * LANES, LANES)
    kwin = kw_ref[0, pl.ds(wstart, wlen), :]
    rel = t_g - (wstart + _iota((wlen, gw), 0))
    wmask = lax.bitcast_convert_type(rel, jnp.uint32) < WINDOW
    partial = []
    for g in range(NSA_GROUPS):
        lo = g * NSA_DH
        s = jnp.dot(kwin, qT[g], preferred_element_type=F32)
        s = jnp.where(wmask, s, NEG_INF)
        pb = jnp.exp2(s - jnp.max(s, axis=0, keepdims=True)).astype(BF16)
        ow = jnp.dot(vwt_ref[0, wt0], pb[:LANES], preferred_element_type=F32)
        for i in range(1, wlen // LANES):
            ow = ow + jnp.dot(vwt_ref[0, wt0 + i], pb[i * LANES:(i + 1) * LANES], preferred_element_type=F32)
        ow = ow[:LANES] * (1.0 / ow[LANES:LANES + 1])
        partial.append(gate_row(0, g) * ocT[g][lo:lo + NSA_DH] + gate_row(2, g) * ow[lo:lo + NSA_DH])

    nt = (q0 + qb + tk - 1) // tk

    def tile_step(r0, j, lane0, width, diagonal, carry):
        kt = ks_ref[0, pl.ds(r0, width), :]
        expand = jnp.where(((r0 + _iota((width, ns), 0)) >> 6) == _iota((width, ns), 1), 1.0, 0.0).astype(BF16)
        new = []
        for g in range(NSA_GROUPS):
            m, l, acc = carry[g]
            hit = jnp.dot(expand, selT[g], preferred_element_type=F32)
            if diagonal:
                hit = jnp.where((r0 + _iota((width, qb), 0)) <= t_l, hit, 0.0)
            ok = hit > 0.5
            s = jnp.dot(kt, qT[g], preferred_element_type=F32)
            s = jnp.concatenate([jnp.where(ok, head_cols(s, r), NEG_INF) for r in range(NSA_REP)], axis=1)
            m_new = jnp.maximum(m, jnp.max(s, axis=0, keepdims=True))
            alpha = jnp.exp2(m - m_new)
            pv = jnp.dot(vst_ref[0, j, :, lane0:lane0 + width], jnp.exp2(s - m_new).astype(BF16),
                         preferred_element_type=F32)
            new.append((m_new, alpha * l + pv[LANES:LANES + 1], alpha * acc + pv[:LANES]))
        return tuple(new)

    init = tuple((jnp.full((1, gw), NEG_INF, F32), jnp.zeros((1, gw), F32), jnp.zeros((LANES, gw), F32))
                 for _ in range(NSA_GROUPS))
    final = lax.fori_loop(0, nt - 1, lambda j, c: tile_step(pl.multiple_of(j * tk, tk), j, 0, tk, False, c), init)
    half = tk // 2
    base = pl.multiple_of((nt - 1) * tk, tk)
    final = tile_step(base, nt - 1, 0, half, True, final)
    final = lax.cond(q0 + qb > base + half,
                     lambda c: tile_step(pl.multiple_of(base + half, half), nt - 1, half, half, True, c),
                     lambda c: c, final)

    heads = []
    for g in range(NSA_GROUPS):
        _, l, acc = final[g]
        o_g = partial[g] + gate_row(1, g) * (acc[g * NSA_DH:(g + 1) * NSA_DH] * (1.0 / l))
        heads += [head_cols(o_g, r) for r in range(NSA_REP)]
    for c in range(NSA_HEADS // 2):
        o_ref[0, :, c * LANES:(c + 1) * LANES] = jnp.concatenate(heads[2 * c:2 * c + 2], axis=0).T


def _nsa_attention(q, kc, vct, ovt, kvs, vst, vwt, small, n_sel):
    B, S, _ = q.shape
    ns, ncp = ovt.shape
    tk = vst.shape[3]
    assert S >= WINDOW + Q_BLOCK and S % tk == 0 and tk % LANES == 0
    nt128 = S // LANES

    def whole(shape):
        return pl.BlockSpec((1,) + shape, lambda b, i: (b,) + (0,) * len(shape))

    return pl.pallas_call(
        functools.partial(_nsa_kernel, n_sel=n_sel, tk=tk),
        out_shape=jax.ShapeDtypeStruct((B, S, NSA_QW), F32),
        grid=(B, S // Q_BLOCK),
        in_specs=[pl.BlockSpec((1, Q_BLOCK, NSA_QW), lambda b, i: (b, i, 0)),
                  whole((ncp, LANES)), whole((LANES, ncp)),
                  pl.BlockSpec((ns, ncp), lambda b, i: (0, 0)),
                  pl.BlockSpec((1, S, LANES), lambda b, i: (b, 0, 0)),
                  whole((S // tk, LANES + ONES_ROWS, tk)),
                  pl.BlockSpec((1, S, LANES), lambda b, i: (b, 0, 2)),
                  whole((nt128, LANES + ONES_ROWS, LANES)),
                  pl.BlockSpec((1, Q_BLOCK, LANES), lambda b, i: (b, i, 0))],
        out_specs=pl.BlockSpec((1, Q_BLOCK, NSA_QW), lambda b, i: (b, i, 0)),
        compiler_params=_cparams(("parallel", "parallel"), VMEM_LIMIT),
        name="nsa_attention",
    )(q, kc, vct, ovt, kvs, vst, kvs, vwt, small)


def _mlstm_kernel(gb_ref, q_ref, k_ref, v_ref, o_ref, grow_ref, gcol_ref, cw_ref, cb_ref, ng_ref, out_ref,
                  c_scr, n_scr, m_scr, hq_scr, hk_scr):
    L = ML_CHUNK
    H = ML_HEADS
    ts = q_ref.shape[1]
    kscale = ML_DH ** -0.5
    row = _iota((L, L), 0)
    col = _iota((L, L), 1)
    tri = row >= col
    halo_rows = SUBLANES

    @pl.when(pl.program_id(1) == 0)
    def _():
        c_scr[...] = jnp.zeros_like(c_scr)
        n_scr[...] = jnp.zeros_like(n_scr)
        m_scr[...] = jnp.zeros_like(m_scr)
        hq_scr[...] = jnp.zeros_like(hq_scr)
        hk_scr[...] = jnp.zeros_like(hk_scr)

    def conv_silu(x_ref, prev_ref, woff, h, c, r0):
        lanes = slice(h * ML_DH, (h + 1) * ML_DH)
        wl = slice(woff + h * ML_DH, woff + (h + 1) * ML_DH)
        main = x_ref[0, pl.ds(r0, L), lanes]
        h0 = pl.multiple_of(jnp.maximum(r0 - halo_rows, 0), halo_rows)
        halo = jnp.where(c > 0, x_ref[0, pl.ds(h0, halo_rows), lanes], prev_ref[:, lanes])
        cat = jnp.concatenate([halo, main], axis=0)
        y = main * cw_ref[CONV_WIDTH - 1:CONV_WIDTH, wl] + cb_ref[:, wl]
        for w in range(CONV_WIDTH - 1):
            sft = CONV_WIDTH - 1 - w
            y = y + pltpu.roll(cat, sft, 0)[halo_rows:, :] * cw_ref[w:w + 1, wl]
        return y * _sigmoid(y)

    def logsig(x):
        return -(jnp.maximum(-x, 0.0) + jnp.log(1.0 + jnp.exp(-jnp.abs(x))))

    def head_step(h, c, r0, C, n, m):
        lanes = slice(h * ML_DH, (h + 1) * ML_DH)
        gb_i = gb_ref[h]
        gb_f = gb_ref[H + h]
        qc = conv_silu(q_ref, hq_scr, 0, h, c, r0)
        kc = conv_silu(k_ref, hk_scr, ML_W, h, c, r0) * kscale
        vc = v_ref[0, pl.ds(r0, L), lanes]
        li_row = grow_ref[0, h, pl.ds(c, 1), :] + gb_i
        lf_row = logsig(grow_ref[0, H + h, pl.ds(c, 1), :] + gb_f)
        li_col = gcol_ref[0, h, pl.ds(r0, L), :] + gb_i
        lf_col = logsig(gcol_ref[0, H + h, pl.ds(r0, L), :] + gb_f)
        b_col = jnp.sum(jnp.where(tri, lf_row, 0.0), axis=1, keepdims=True)
        b_row = jnp.sum(jnp.where(row <= col, lf_col, 0.0), axis=0, keepdims=True)
        Dm = jnp.where(tri, b_col - b_row + li_row, NEG_INF)
        inter = b_col + m
        m_t = jnp.maximum(inter, jnp.max(Dm, axis=1, keepdims=True))
        Dw = jnp.exp(Dm - m_t)
        inter_w = jnp.exp(inter - m_t)
        qb16 = qc.astype(BF16)
        kb16 = kc.astype(BF16)
        vb16 = vc.astype(BF16)
        qk = lax.dot_general(qb16, kb16, (((1,), (1,)), ((), ())), preferred_element_type=F32) * Dw
        num = (jnp.dot(qk.astype(BF16), vb16, preferred_element_type=F32)
               + inter_w * jnp.dot(qb16, C.astype(BF16), preferred_element_type=F32))
        den = jnp.sum(qk, axis=1, keepdims=True) + inter_w * jnp.sum(qc * n, axis=1, keepdims=True)
        hc = num / jnp.maximum(jnp.abs(den), jnp.exp(-m_t))
        bL = b_col[L - 1:L, :]
        a_col = bL - b_col + li_col
        m_new = jnp.maximum(bL + m, jnp.max(a_col, axis=0, keepdims=True))
        aw = jnp.exp(a_col - m_new)
        decay = jnp.exp(bL + m - m_new)
        awk = aw * kc
        C_new = decay * C + jnp.dot(awk.T.astype(BF16), vb16, preferred_element_type=F32)
        n_new = decay * n + jnp.sum(awk, axis=0, keepdims=True)
        hn = hc * lax.rsqrt(jnp.mean(hc * hc, axis=1, keepdims=True) + NORM_EPS) * ng_ref[:, lanes]
        out_ref[0, pl.ds(r0, L), lanes] = hn * _sigmoid(o_ref[0, pl.ds(r0, L), lanes])
        return C_new, n_new, m_new

    def step(c, carry):
        r0 = pl.multiple_of(c * L, L)
        return tuple(head_step(h, c, r0, *carry[h]) for h in range(H))

    init = tuple((c_scr[h], n_scr[h], m_scr[h][:, 0:1]) for h in range(H))
    final = lax.fori_loop(0, ts // L, step, init)
    for h in range(H):
        C, n, m = final[h]
        c_scr[h] = C
        n_scr[h] = n
        m_scr[h] = jnp.broadcast_to(m, (1, LANES))
    hq_scr[...] = q_ref[0, ts - halo_rows:ts, :]
    hk_scr[...] = k_ref[0, ts - halo_rows:ts, :]


def _mlstm(ml_qk, ml_v, ml_o, gate_row, gate_col, gate_b, conv_w, conv_b, norm_g, ts=1024):
    B, S, _ = ml_v.shape
    H = ML_HEADS
    ts = min(ts, S)

    def rows(cb):
        return pl.BlockSpec((1, ts, ML_W), lambda b, i, gb: (b, i, cb))

    def full(shape):
        return pl.BlockSpec(shape, lambda b, i, gb: (0,) * len(shape))

    grid_spec = pltpu.PrefetchScalarGridSpec(
        num_scalar_prefetch=1,
        grid=(B, S // ts),
        in_specs=[rows(0), rows(1), rows(0), rows(0),
                  pl.BlockSpec((1, 2 * H, ts // ML_CHUNK, ML_CHUNK), lambda b, i, gb: (b, 0, i, 0)),
                  pl.BlockSpec((1, 2 * H, ts, 1), lambda b, i, gb: (b, 0, i, 0)),
                  full((CONV_WIDTH, 2 * ML_W)), full((1, 2 * ML_W)), full((1, ML_W))],
        out_specs=rows(0),
        scratch_shapes=[pltpu.VMEM((H, ML_DH, ML_DH), F32), pltpu.VMEM((H, 1, ML_DH), F32),
                        pltpu.VMEM((H, 1, LANES), F32),
                        pltpu.VMEM((SUBLANES, ML_W), F32), pltpu.VMEM((SUBLANES, ML_W), F32)],
    )
    return pl.pallas_call(
        _mlstm_kernel,
        out_shape=jax.ShapeDtypeStruct((B, S, ML_W), F32),
        grid_spec=grid_spec,
        compiler_params=_cparams(("parallel", "arbitrary"), VMEM_LIMIT),
        name="mlstm",
    )(gate_b, ml_qk, ml_qk, ml_v, ml_o, gate_row, gate_col, conv_w, conv_b, norm_g)


def _pack_cols(cols, dtype):
    tm = cols[0].shape[0]
    lane = _iota((tm, LANES), 1)
    out = jnp.zeros((tm, LANES), dtype)
    for k, c in enumerate(cols):
        out = jnp.where(lane == k, c, out)
    return out


def _outproj_kernel(a_ref, b_ref, mg_ref, x_ref, gt_ref, sc_ref, sh_ref, g2_ref,
                    wa_ref, wb_ref, wo_ref, wr_ref, br_ref,
                    h_ref, n_ref, ti_ref, tg_ref):
    D = x_ref.shape[2]
    ua = jnp.dot(a_ref[0].astype(BF16), wa_ref[...], preferred_element_type=F32)
    ub = jnp.dot(b_ref[0].astype(BF16), wb_ref[...], preferred_element_type=F32)
    u = _sigmoid(mg_ref[0, :, :D]) * ua + _sigmoid(mg_ref[0, :, D:]) * ub
    mix = jnp.dot(u.astype(BF16), wo_ref[...], preferred_element_type=F32)
    h = x_ref[0] + (1.0 + gt_ref[0]) * mix
    h_ref[0] = h
    y = h * lax.rsqrt(jnp.mean(h * h, axis=-1, keepdims=True) + NORM_EPS) * g2_ref[...]
    n = y * (1.0 + sc_ref[0]) + sh_ref[0]
    _rows_to_tiles(n_ref, n, lead=(0,))
    n_hi = n.astype(BF16)
    n_lo = (n - n_hi.astype(F32)).astype(BF16)
    logits = (jnp.dot(n_hi, wr_ref[0], preferred_element_type=F32)
              + (jnp.dot(n_hi, wr_ref[1], preferred_element_type=F32)
                 + jnp.dot(n_lo, wr_ref[0], preferred_element_type=F32))) + br_ref[...]
    tm, ne = logits.shape
    lane = _iota((tm, ne), 1).astype(F32)
    work = logits
    vals, idxs = [], []
    for _ in range(TOP_K):
        mx = jnp.max(work, axis=-1, keepdims=True)
        ix = jnp.min(jnp.where(work == mx, lane, float(ne)), axis=-1, keepdims=True)
        vals.append(mx)
        idxs.append(ix)
        work = jnp.where(lane == ix, -jnp.inf, work)
    es = [jnp.exp(v - vals[0]) for v in vals]
    tot = es[0] + es[1] + es[2] + es[3]
    ti_ref[0] = _pack_cols(idxs, F32).astype(I32)
    tg_ref[0] = _pack_cols([e / tot for e in es], F32)


def _out_projection(o_nsa, o_ml, merge, x, gt1, sc2, sh2, g2, wa, wb, wo, wr, br, tm=256):
    B, S, D = x.shape
    vec = pl.BlockSpec((1, 1, D), lambda b, i: (b, 0, 0))

    def full(shape):
        return pl.BlockSpec(shape, lambda b, i: (0,) * len(shape))

    def rows(w):
        return pl.BlockSpec((1, tm, w), lambda b, i: (b, i, 0))

    assert D == SUBLANES * LANES
    return pl.pallas_call(
        _outproj_kernel,
        out_shape=[jax.ShapeDtypeStruct((B, S, D), F32), jax.ShapeDtypeStruct((B, S * SUBLANES, LANES), F32),
                   jax.ShapeDtypeStruct((B, S, LANES), I32), jax.ShapeDtypeStruct((B, S, LANES), F32)],
        grid=(B, S // tm),
        in_specs=[rows(NSA_QW), rows(ML_W), rows(2 * D), rows(D), vec, vec, vec, full((1, D)),
                  full(wa.shape), full(wb.shape), full(wo.shape), full(wr.shape), full((1, N_EXPERTS))],
        out_specs=[rows(D), pl.BlockSpec((1, tm * SUBLANES, LANES), lambda b, i: (b, i, 0)), rows(LANES),
                   rows(LANES)],
        compiler_params=_cparams(("parallel", "parallel"), VMEM_LIMIT),
        name="out_projection_router",
    )(o_nsa, o_ml, merge, x, gt1, sc2, sh2, g2, wa, wb, wo, wr, br)


def _rank_kernel(ti_ref, rank_ref, cnt_ref, carry_ref):
    tt = ti_ref.shape[0]

    @pl.when(pl.program_id(0) == 0)
    def _():
        carry_ref[...] = jnp.zeros_like(carry_ref)

    ti = ti_ref[...]
    lane = _iota((tt, LANES), 1)
    hots = [ti[:, k:k + 1] == lane for k in range(TOP_K)]
    member = jnp.zeros((tt, LANES), F32)
    for hot in hots:
        member = member + jnp.where(hot, 1.0, 0.0)
    strict = jnp.where(_iota((tt, tt), 0) > _iota((tt, tt), 1), 1.0, 0.0).astype(BF16)
    before = jnp.dot(strict, member.astype(BF16), preferred_element_type=F32) + carry_ref[0:1, :]
    ranks = [jnp.sum(jnp.where(hot, before, 0.0), axis=-1, keepdims=True) for hot in hots]
    rank_ref[...] = _pack_cols(ranks, F32)
    carry_ref[...] = carry_ref[...] + jnp.sum(member, axis=0, keepdims=True)
    cnt_ref[...] = carry_ref[...]


def _expert_rank(ti, tt=512):
    T = ti.shape[0]
    return pl.pallas_call(
        _rank_kernel,
        out_shape=[jax.ShapeDtypeStruct((T, LANES), F32), jax.ShapeDtypeStruct((SUBLANES, LANES), F32)],
        grid=(T // tt,),
        in_specs=[pl.BlockSpec((tt, LANES), lambda i: (i, 0))],
        out_specs=[pl.BlockSpec((tt, LANES), lambda i: (i, 0)),
                   pl.BlockSpec((SUBLANES, LANES), lambda i: (0, 0))],
        scratch_shapes=[pltpu.VMEM((SUBLANES, LANES), F32)],
        compiler_params=_cparams(("arbitrary",)),
        name="moe_rank",
    )(ti)


def _dest_kernel(cnt_ref, ti_ref, rank_ref, dest_ref, be_ref, pad_ref):
    tt = ti_ref.shape[0]
    nbp = be_ref.shape[0]
    cnt = cnt_ref[...].astype(I32)
    padded = ((cnt + (MOE_ROWS - 1)) >> 8) << 8
    lane8 = _iota((SUBLANES, LANES), 1)
    pad_end = padded
    sft = 1
    while sft < LANES:
        pad_end = pad_end + jnp.where(lane8 >= sft, pltpu.roll(pad_end, sft, 1), 0)
        sft *= 2
    pad_start = (pad_end - padded)[0:1, :].astype(F32)
    ti = ti_ref[...]
    lane = _iota((tt, LANES), 1)
    rank = rank_ref[...]
    dests = [jnp.sum(jnp.where(ti[:, k:k + 1] == lane, pad_start, 0.0), axis=-1, keepdims=True) + rank[:, k:k + 1]
             for k in range(TOP_K)]
    dest_ref[...] = _pack_cols(dests, F32).astype(I32) * SUBLANES
    first_row = _iota((nbp, LANES), 0) * MOE_ROWS
    done = (pad_end[0:1, :] <= first_row) & (_iota((nbp, LANES), 1) < N_EXPERTS)
    be = jnp.minimum(jnp.sum(jnp.where(done, 1.0, 0.0), axis=-1, keepdims=True), float(N_EXPERTS - 1))
    total = pad_end[:, N_EXPERTS - 1:N_EXPERTS]
    used = first_row < total[0:1]
    be_ref[...] = jnp.where(_iota((nbp, LANES), 1) == 1, jnp.where(used, 1, 0), be.astype(I32))
    prow = _iota((SUBLANES, LANES), 0)
    pad_ref[...] = jnp.where(prow == 0, (pad_end - padded + cnt) * SUBLANES,
                             jnp.where(prow == 1, padded - cnt, jnp.broadcast_to(total, (SUBLANES, LANES))))


def _expert_dest(cnt, ti, rank, n_blk, tt=1024):
    T = ti.shape[0]
    return pl.pallas_call(
        _dest_kernel,
        out_shape=[jax.ShapeDtypeStruct((T, LANES), I32), jax.ShapeDtypeStruct((n_blk, LANES), I32),
                   jax.ShapeDtypeStruct((SUBLANES, LANES), I32)],
        grid=(T // tt,),
        in_specs=[pl.BlockSpec((SUBLANES, LANES), lambda i: (0, 0)),
                  pl.BlockSpec((tt, LANES), lambda i: (i, 0)),
                  pl.BlockSpec((tt, LANES), lambda i: (i, 0))],
        out_specs=[pl.BlockSpec((tt, LANES), lambda i: (i, 0)),
                   pl.BlockSpec((n_blk, LANES), lambda i: (0, 0)),
                   pl.BlockSpec((SUBLANES, LANES), lambda i: (0, 0))],
        compiler_params=_cparams(("arbitrary",)),
        name="moe_dest",
    )(cnt, ti, rank)


def _scatter_kernel(dest_ref, pad_ref, src_ref, dst_ref, zero_ref, sem, zsems):
    tt = src_ref.shape[0] // SUBLANES
    zrows = zero_ref.shape[0]
    n_tail = (dst_ref.shape[0] - pad_ref[2, 0] * SUBLANES) // zrows

    def zero_row(e, r):
        off = pl.multiple_of(pad_ref[0, e] + r * SUBLANES, SUBLANES)
        return pltpu.make_async_copy(zero_ref.at[pl.ds(0, SUBLANES)], dst_ref.at[pl.ds(off, SUBLANES)], zsems.at[0])

    def zero_block(b):
        off = pl.multiple_of(pad_ref[2, 0] * SUBLANES + b * zrows, zrows)
        return pltpu.make_async_copy(zero_ref, dst_ref.at[pl.ds(off, zrows)], zsems.at[1])

    def zero_fill(wait):
        group = DRAIN_UNROLL if wait else ISSUE_UNROLL

        def per_expert(e, c):
            def one(r):
                zero_row(e, r).wait() if wait else zero_row(e, r).start()

            def grouped(gi, c2):
                for u in range(group):
                    one(gi * group + u)
                return c2

            def single(r, c2):
                one(r)
                return c2

            n = pad_ref[1, e]
            lax.fori_loop(0, n // group, grouped, 0)
            lax.fori_loop((n // group) * group, n, single, 0)
            return c
        lax.fori_loop(0, N_EXPERTS, per_expert, 0)

        def tail(b, c):
            zero_block(b).wait() if wait else zero_block(b).start()
            return c
        lax.fori_loop(0, n_tail, tail, 0)

    @pl.when(pl.program_id(0) == 0)
    def _():
        zero_ref[...] = jnp.zeros_like(zero_ref)
        zero_fill(wait=False)

    def copy(t, k):
        return pltpu.make_async_copy(
            src_ref.at[pl.ds(pl.multiple_of(t * SUBLANES, SUBLANES), SUBLANES)],
            dst_ref.at[pl.ds(pl.multiple_of(dest_ref[t * TOP_K + k], SUBLANES), SUBLANES)], sem)

    def issue(t, c):
        for k in range(TOP_K):
            copy(t, k).start(priority=k % 2)
        return c

    lax.fori_loop(0, tt, issue, 0, unroll=ISSUE_UNROLL)

    def drain(t, c):
        for k in range(TOP_K):
            copy(t, k).wait()
        return c

    lax.fori_loop(0, tt, drain, 0, unroll=DRAIN_UNROLL)

    @pl.when(pl.program_id(0) == pl.num_programs(0) - 1)
    def _():
        zero_fill(wait=True)


def _scatter_rows(dest_flat, pad_info, rows, n_rows, tt=512):
    T = rows.shape[0] // SUBLANES
    return pl.pallas_call(
        _scatter_kernel,
        out_shape=jax.ShapeDtypeStruct((n_rows * SUBLANES, LANES), rows.dtype),
        grid=(T // tt,),
        in_specs=[pl.BlockSpec((tt * TOP_K,), lambda i: (i,), memory_space=pltpu.SMEM),
                  pl.BlockSpec((SUBLANES, LANES), lambda i: (0, 0), memory_space=pltpu.SMEM),
                  pl.BlockSpec((tt * SUBLANES, LANES), lambda i: (i, 0))],
        out_specs=pl.BlockSpec(memory_space=pl.ANY),
        scratch_shapes=[pltpu.VMEM((MOE_ROWS * SUBLANES, LANES), rows.dtype), pltpu.SemaphoreType.DMA,
                        pltpu.SemaphoreType.DMA((2,))],
        compiler_params=_cparams(("arbitrary",)),
        name="moe_scatter_rows",
    )(dest_flat, pad_info, rows)


def _expert_kernel(be_ref, used_ref, x_ref, w1_hbm, b1_ref, w2_hbm, b2_ref, y_ref,
                   w1f_ref, w2f_ref, w1b_ref, w2b_ref, sems, slot_ref):
    i = pl.program_id(0)
    nb = pl.num_programs(0)
    e = be_ref[i]

    def fetch(expert, slot):
        return (pltpu.make_async_copy(w1_hbm.at[expert], w1f_ref.at[slot], sems.at[0, slot]),
                pltpu.make_async_copy(w2_hbm.at[expert], w2f_ref.at[slot], sems.at[1, slot]))

    @pl.when(i == 0)
    def _():
        slot_ref[0] = 0
        for cp in fetch(e, 0):
            cp.start()

    @pl.when(jnp.logical_or(i == 0, e != be_ref[jnp.maximum(i - 1, 0)]))
    def _():
        slot = slot_ref[0]
        for cp in fetch(e, slot):
            cp.wait()
        w1b_ref[...] = w1f_ref[slot].astype(BF16)
        w2b_ref[...] = w2f_ref[slot].astype(BF16)
        j = lax.while_loop(lambda j: jnp.logical_and(j < nb, be_ref[jnp.minimum(j, nb - 1)] == e),
                           lambda j: j + 1, i + 1)

        @pl.when(j < nb)
        def _():
            for cp in fetch(be_ref[jnp.minimum(j, nb - 1)], 1 - slot):
                cp.start()

        slot_ref[0] = 1 - slot

    @pl.when(used_ref[i] != 0)
    def _():
        x = _tiles_to_rows(x_ref, MOE_ROWS).astype(BF16)
        h = jnp.dot(x, w1b_ref[...], preferred_element_type=F32) + b1_ref[0]
        glu = jnp.minimum(h[:, :D_EXPERT], SWIGLU_LIMIT)
        lin = jnp.clip(h[:, D_EXPERT:], -SWIGLU_LIMIT, SWIGLU_LIMIT)
        act = glu * _sigmoid(SWIGLU_ALPHA * glu) * (lin + 1.0)
        y = jnp.dot(act.astype(BF16), w2b_ref[...], preferred_element_type=F32) + b2_ref[0]
        _rows_to_tiles(y_ref, y)

    @pl.when(used_ref[i] == 0)
    def _():
        y_ref[...] = jnp.zeros_like(y_ref)


def _expert_ffn(blk_expert, blk_used, x_rows, w1, b1, w2, b2):
    P = x_rows.shape[0] // SUBLANES
    D = SUBLANES * LANES
    nb = P // MOE_ROWS
    tile_rows = pl.BlockSpec((MOE_ROWS * SUBLANES, LANES), lambda i, be, us: (i, 0))
    grid_spec = pltpu.PrefetchScalarGridSpec(
        num_scalar_prefetch=2,
        grid=(nb,),
        in_specs=[tile_rows,
                  pl.BlockSpec(memory_space=pl.ANY),
                  pl.BlockSpec((1, 1, 2 * D_EXPERT), lambda i, be, us: (be[i], 0, 0)),
                  pl.BlockSpec(memory_space=pl.ANY),
                  pl.BlockSpec((1, 1, D), lambda i, be, us: (be[i], 0, 0))],
        out_specs=tile_rows,
        scratch_shapes=[pltpu.VMEM((2, D, 2 * D_EXPERT), F32), pltpu.VMEM((2, D_EXPERT, D), F32),
                        pltpu.VMEM((D, 2 * D_EXPERT), BF16), pltpu.VMEM((D_EXPERT, D), BF16),
                        pltpu.SemaphoreType.DMA((2, 2)), pltpu.SMEM((1,), I32)],
    )
    return pl.pallas_call(
        _expert_kernel,
        out_shape=jax.ShapeDtypeStruct(x_rows.shape, F32),
        grid_spec=grid_spec,
        compiler_params=_cparams(("arbitrary",), VMEM_LIMIT),
        name="moe_expert_ffn",
    )(blk_expert, blk_used, x_rows, w1, b1, w2, b2)


def _combine_kernel(dest_ref, next_ref, gate_ref, y_ref, h_ref, gt_ref, g_ref, o_ref, buf_ref, sems, *, normalize):
    tt = h_ref.shape[0]
    i = pl.program_id(0)
    slot = i % 2

    def copy(idx_ref, slot_, t, k):
        src = y_ref.at[pl.ds(pl.multiple_of(idx_ref[t * TOP_K + k], SUBLANES), SUBLANES)]
        dst = buf_ref.at[slot_, k, pl.ds(pl.multiple_of(t * SUBLANES, SUBLANES), SUBLANES)]
        return pltpu.make_async_copy(src, dst, sems.at[slot_])

    def loop(body, unroll):
        def wrapped(t, c):
            for k in range(TOP_K):
                body(t, k)
            return c
        lax.fori_loop(0, tt, wrapped, 0, unroll=unroll)

    @pl.when(i == 0)
    def _():
        loop(lambda t, k: copy(dest_ref, slot, t, k).start(priority=k % 2), ISSUE_UNROLL)

    loop(lambda t, k: copy(dest_ref, slot, t, k).wait(), DRAIN_UNROLL)

    @pl.when(i + 1 < pl.num_programs(0))
    def _():
        loop(lambda t, k: copy(next_ref, 1 - slot, t, k).start(priority=k % 2), ISSUE_UNROLL)

    gates = gate_ref[...]
    y = gates[:, 0:1] * _tiles_to_rows(buf_ref, tt, lead=(slot, 0))
    for k in range(1, TOP_K):
        y = y + gates[:, k:k + 1] * _tiles_to_rows(buf_ref, tt, lead=(slot, k))
    h = h_ref[...] + (1.0 + gt_ref[0]) * y
    if normalize:
        h = h * lax.rsqrt(jnp.mean(h * h, axis=-1, keepdims=True) + NORM_EPS) * g_ref[...]
    o_ref[...] = h


def _combine_rows(dest_flat, gates, y_rows, h1, gt2, g, normalize, tt=256):
    B, S, D = h1.shape
    T = B * S
    assert S % tt == 0
    last = T // tt - 1
    out = pl.pallas_call(
        functools.partial(_combine_kernel, normalize=normalize),
        out_shape=jax.ShapeDtypeStruct((T, D), F32),
        grid=(T // tt,),
        in_specs=[pl.BlockSpec((tt * TOP_K,), lambda i: (i,), memory_space=pltpu.SMEM),
                  pl.BlockSpec((tt * TOP_K,), lambda i: (jnp.minimum(i + 1, last),), memory_space=pltpu.SMEM),
                  pl.BlockSpec((tt, LANES), lambda i: (i, 0)),
                  pl.BlockSpec(memory_space=pl.ANY),
                  pl.BlockSpec((tt, D), lambda i: (i, 0)),
                  pl.BlockSpec((1, 1, D), lambda i: ((i * tt) // S, 0, 0)),
                  pl.BlockSpec((1, D), lambda i: (0, 0))],
        out_specs=pl.BlockSpec((tt, D), lambda i: (i, 0)),
        scratch_shapes=[pltpu.VMEM((2, TOP_K, tt * SUBLANES, LANES), F32), pltpu.SemaphoreType.DMA((2,))],
        compiler_params=_cparams(("arbitrary",)),
        name="moe_combine_rows",
    )(dest_flat, dest_flat, gates, y_rows, h1.reshape(T, D), gt2, g)
    return out.reshape(B, S, D)


def _overlap_matrix(S):
    nc = S // CMP_STRIDE - 1
    ns = S // SEL_BLOCK
    c0 = np.arange(nc) * CMP_STRIDE
    s0 = np.arange(ns) * SEL_BLOCK
    ov = np.clip(np.minimum(c0[:, None] + CMP_LEN, s0[None, :] + SEL_BLOCK)
                 - np.maximum(c0[:, None], s0[None, :]), 0, None) / CMP_LEN
    return np.concatenate([ov, np.zeros((1, ns))], axis=0).astype(np.float32)


def _layer(h, mods, norm1_g, w_in, pe_k, pe_v, cw1_k, cw2_k, cw1_v, cw2_v, conv_w, conv_b, gate_b, ml_norm_g,
           w_up_nsa, w_up_ml, w_out, norm2_g, router_w, router_b, exp_w1, exp_b1, exp_w2, exp_b2,
           final_g, last_layer):
    B, S, D = h.shape
    T = B * S
    sh1, sc1, gt1, sh2, sc2, gt2 = mods

    wb = w_in.astype(BF16)
    w_r = jnp.concatenate([wb[:, 0:1280], wb[:, 1304:2328], wb[:, 2328:2840], wb[:, 2848:3360],
                           wb[:, 3360:5408], wb[:, 1280:1304], wb[:, 2840:2848],
                           jnp.zeros((D, IN_WR - 5408), BF16)], axis=1)
    q, kvc, kvs, ml_qk, ml_v, ml_o, merge, small = _in_projection(h, norm1_g.reshape(1, D), sc1, sh1, w_r)

    nb16 = S // CMP_STRIDE
    t16 = kvc.reshape(B, S, 4, NSA_DH).transpose(0, 2, 1, 3).reshape(B, 4, nb16, CMP_STRIDE * NSA_DH)
    pe = jnp.stack([pe_k.reshape(2, CMP_STRIDE * NSA_DH), pe_v.reshape(2, CMP_STRIDE * NSA_DH)])
    cmp = _compress(t16, pe, jnp.stack([cw1_k, cw1_v]), jnp.stack([cw2_k, cw2_v]))
    n_sel = min(SEL_TOPN, S // SEL_BLOCK)
    kc = cmp[:, :NSA_GROUPS].transpose(0, 2, 1, 3).reshape(B, nb16, NSA_KVW).astype(BF16)
    vct = cmp[:, NSA_GROUPS:].transpose(0, 1, 3, 2).reshape(B, NSA_KVW, nb16).astype(BF16)

    def key_major_tiles(v, w):
        vt = v.reshape(B, S // w, w, LANES).swapaxes(-1, -2)
        return jnp.concatenate([vt, jnp.ones((B, S // w, ONES_ROWS, w), vt.dtype)], axis=2)

    o_nsa = _nsa_attention(q, kc, vct, jnp.asarray(_overlap_matrix(S).T, dtype=BF16), kvs,
                           key_major_tiles(kvs[:, :, LANES:2 * LANES], min(8 * LANES, S)),
                           key_major_tiles(kvs[:, :, 3 * LANES:], LANES), small, n_sel)

    gates = small[:, :, 3 * NSA_HEADS:3 * NSA_HEADS + 2 * ML_HEADS].transpose(0, 2, 1)
    gate_row = gates.reshape(B, 2 * ML_HEADS, S // ML_CHUNK, ML_CHUNK)
    gate_col = gates.reshape(B, 2 * ML_HEADS, S, 1)
    o_ml = _mlstm(ml_qk, ml_v, ml_o, gate_row, gate_col, gate_b, conv_w, conv_b.reshape(1, -1),
                  ml_norm_g.reshape(1, -1))

    rw_hi = router_w.astype(BF16)
    rw_lo = (router_w - rw_hi.astype(F32)).astype(BF16)
    h1, n2, ti, tg = _out_projection(o_nsa, o_ml, merge, h, gt1, sc2, sh2, norm2_g.reshape(1, D),
                                     w_up_nsa.astype(BF16), w_up_ml.astype(BF16), w_out.astype(BF16),
                                     jnp.stack([rw_hi, rw_lo]), router_b.reshape(1, N_EXPERTS))

    A = T * TOP_K
    n_blk = -(-(A + N_EXPERTS * (MOE_ROWS - 1)) // MOE_ROWS)
    P = n_blk * MOE_ROWS
    ti2 = ti.reshape(T, LANES)
    rank, cnt = _expert_rank(ti2)
    dest, blk_e, pad_info = _expert_dest(cnt, ti2, rank, n_blk)
    dest_flat = dest[:, :TOP_K].reshape(A)
    x_rows = _scatter_rows(dest_flat, pad_info, n2.reshape(T * SUBLANES, LANES), P)
    y_rows = _expert_ffn(blk_e[:, 0], blk_e[:, 1], x_rows, exp_w1, exp_b1.reshape(N_EXPERTS, 1, -1),
                         exp_w2, exp_b2.reshape(N_EXPERTS, 1, -1))
    return _combine_rows(dest_flat, tg.reshape(T, LANES), y_rows, h1, gt2, final_g, normalize=last_layer)


def kernel(x, c, ada_w, ada_b, norm1_g, w_in, nsa_pe_k, nsa_pe_v, nsa_cmp_w1_k, nsa_cmp_w2_k, nsa_cmp_w1_v, nsa_cmp_w2_v, ml_conv_w, ml_conv_b, ml_gate_b, ml_norm_g, w_up_nsa, w_up_ml, w_out, norm2_g, router_w, router_b, exp_w1, exp_b1, exp_w2, exp_b2, final_g):
    B, S, D = x.shape
    depth = ada_w.shape[0]
    h = x
    for layer in range(depth):
        mod = _modulation(c, ada_w[layer], ada_b[layer])
        mods = [mod[:, i * D:(i + 1) * D].reshape(B, 1, D) for i in range(6)]
        h = _layer(h, mods, norm1_g[layer], w_in[layer], nsa_pe_k[layer], nsa_pe_v[layer],
                   nsa_cmp_w1_k[layer], nsa_cmp_w2_k[layer], nsa_cmp_w1_v[layer], nsa_cmp_w2_v[layer],
                   ml_conv_w[layer], ml_conv_b[layer], ml_gate_b[layer], ml_norm_g[layer],
                   w_up_nsa[layer], w_up_ml[layer], w_out[layer], norm2_g[layer], router_w[layer],
                   router_b[layer], exp_w1[layer], exp_b1[layer], exp_w2[layer], exp_b2[layer],
                   final_g.reshape(1, D), layer + 1 == depth)
    return h
```
